```python
import math
import jax, jax.numpy as jnp
from jax import lax
import numpy as np

D_MODEL = 1024
BATCH = 8
SEQ = 8192
DEPTH = 4

N_MIXERS = 3
N_DIL_LAYERS = (DEPTH + 2) // 3
N_DIFF_LAYERS = (DEPTH + 1) // 3
N_SSM_LAYERS = DEPTH // 3

NORM_EPS = 1e-6
ROPE_THETA = 500000.0
ROPE_FRACTION = 4
ATTN_BLOCK = 128

DIL_CONFIGS = ((128, 1), (512, 4), (2048, 16))
DIL_HEAD_DIM = 64
DIL_HEADS = D_MODEL // 128
DIL_GROUP_WIDTH = DIL_HEADS * DIL_HEAD_DIM
DIL_IN = len(DIL_CONFIGS) * 3 * DIL_GROUP_WIDTH

DIFF_HEAD_DIM = 64
DIFF_HEADS = D_MODEL // (2 * DIFF_HEAD_DIM)
DIFF_IN = 3 * D_MODEL
DIFF_NORM_EPS = 1e-5

SSM_D_INNER = 2 * D_MODEL
SSM_HEAD_DIM = 64
SSM_HEADS = SSM_D_INNER // SSM_HEAD_DIM
SSM_GROUPS = 4
SSM_HEADS_PER_GROUP = SSM_HEADS // SSM_GROUPS
SSM_STATE = 128
SSM_CONV = 4
SSM_CHUNK = 256
SSM_CONV_DIM = SSM_D_INNER + 2 * SSM_GROUPS * SSM_STATE
SSM_IN = SSM_D_INNER + SSM_CONV_DIM + SSM_HEADS

N_EXPERTS = 16
N_EXPERT_GROUPS = 4
EXPERTS_PER_GROUP = N_EXPERTS // N_EXPERT_GROUPS
TOP_K = 2
D_EXPERT = D_MODEL
MOE_BLOCK = 256

kernel_name = "hybrid_dilated_diff_ssd_grouped_moe"


def _rmsnorm(x, g, eps=NORM_EPS):
    xf = x.astype(jnp.float32)
    y = xf * lax.rsqrt(jnp.mean(xf * xf, axis=-1, keepdims=True) + eps)
    return (y * g.astype(jnp.float32)).astype(x.dtype)


def _partial_rope(x, pos):
    hd = x.shape[-1]
    r = hd // ROPE_FRACTION
    half = r // 2
    inv = jnp.power(ROPE_THETA, -jnp.arange(half, dtype=jnp.float32) * 2.0 / r)
    ang = pos.astype(jnp.float32)[:, None] * inv[None, :]
    cos = jnp.cos(ang)[None, :, None, :]
    sin = jnp.sin(ang)[None, :, None, :]
    xf = x.astype(jnp.float32)
    x1, x2 = xf[..., :half], xf[..., half:r]
    out = jnp.concatenate([x1 * cos - x2 * sin, x2 * cos + x1 * sin, xf[..., r:]], axis=-1)
    return out.astype(x.dtype)


def _dilated_band_attention(q, k, v, window, dilation):
    B, S, H, hd = q.shape
    back = window // dilation
    Q = ATTN_BLOCK
    nprev = -(-back // Q)
    unit = dilation * Q
    Sp = -(-S // unit) * unit
    L = Sp // dilation
    nb = L // Q

    def to_residue(t):
        t = jnp.pad(t, ((0, 0), (0, Sp - S), (0, 0), (0, 0)))
        t = t.reshape(B, L, dilation, H, hd).transpose(0, 2, 1, 3, 4)
        return t.reshape(B * dilation, nb, Q, H, hd)

    def band(t):
        tp = jnp.pad(t, ((0, 0), (nprev, 0), (0, 0), (0, 0), (0, 0)))
        return jnp.concatenate([tp[:, j:j + nb] for j in range(nprev + 1)], axis=2)

    qr = to_residue(q)
    kb = band(to_residue(k))
    vb = band(to_residue(v))
    qi = jnp.arange(Q)[:, None]
    kj = jnp.arange((nprev + 1) * Q)[None, :]
    dist = qi + nprev * Q - kj
    key_idx = jnp.arange(nb)[:, None, None] * Q + kj[None] - nprev * Q
    valid = (dist >= 0)[None] & (dist <= back)[None] & (key_idx >= 0)
    s = jnp.einsum('bnqhd,bnkhd->bnhqk', qr.astype(jnp.float32), kb.astype(jnp.float32)) * (hd ** -0.5)
    s = jnp.where(valid[None, :, None], s, -jnp.inf)
    lse = jax.nn.logsumexp(s, axis=-1)
    p = jnp.exp(s - lse[..., None])
    o = jnp.einsum('bnhqk,bnkhd->bnqhd', p, vb.astype(jnp.float32))

    def from_residue(t):
        rest = t.shape[3:]
        t = t.reshape(B, dilation, L, *rest).swapaxes(1, 2)
        return t.reshape(B, Sp, *rest)[:, :S]

    return from_residue(o), from_residue(lse.transpose(0, 1, 3, 2))


def _dilated_mixer(h, w_in, w_out, pos):
    B, S, _ = h.shape
    W = DIL_GROUP_WIDTH
    proj = h @ w_in
    outs, lses = [], []
    for g, (window, dilation) in enumerate(DIL_CONFIGS):
        base = g * 3 * W
        q = proj[..., base:base + W].reshape(B, S, DIL_HEADS, DIL_HEAD_DIM)
        k = proj[..., base + W:base + 2 * W].reshape(B, S, DIL_HEADS, DIL_HEAD_DIM)
        v = proj[..., base + 2 * W:base + 3 * W].reshape(B, S, DIL_HEADS, DIL_HEAD_DIM)
        o, lse = _dilated_band_attention(_partial_rope(q, pos), _partial_rope(k, pos), v, window, dilation)
        outs.append(o)
        lses.append(lse)
    alpha = jax.nn.softmax(jnp.stack(lses, axis=0), axis=0)
    o = jnp.sum(alpha[..., None] * jnp.stack(outs, axis=0), axis=0)
    return o.reshape(B, S, W).astype(h.dtype) @ w_out


def _diff_mixer(h, w_in, lam_q1, lam_k1, lam_q2, lam_k2, head_g, w_out, pos, layer_idx):
    B, S, _ = h.shape
    H, d, Q = DIFF_HEADS, DIFF_HEAD_DIM, ATTN_BLOCK
    q, k, v = jnp.split(h @ w_in, 3, axis=-1)
    q = _partial_rope(q.reshape(B, S, 2 * H, d), pos).reshape(B, S, H, 2, d)
    k = _partial_rope(k.reshape(B, S, 2 * H, d), pos).reshape(B, S, H, 2, d)
    vf = v.reshape(B, S, H, 2 * d).astype(jnp.float32)
    kf = k.astype(jnp.float32)
    lam_init = 0.8 - 0.6 * math.exp(-0.3 * layer_idx)
    lam = (jnp.exp(jnp.sum(lam_q1.astype(jnp.float32) * lam_k1.astype(jnp.float32)))
           - jnp.exp(jnp.sum(lam_q2.astype(jnp.float32) * lam_k2.astype(jnp.float32))) + lam_init)
    nb = S // Q
    qb = q.reshape(B, nb, Q, H, 2, d).transpose(1, 0, 2, 3, 4, 5)
    kpos = jnp.arange(S)

    def block(args):
        qblk, i = args
        s = jnp.einsum('bqhmd,bkhmd->bhmqk', qblk.astype(jnp.float32), kf) * (d ** -0.5)
        qpos = i * Q + jnp.arange(Q)
        s = jnp.where(kpos[None, :] <= qpos[:, None], s, -jnp.inf)
        p = jax.nn.softmax(s, axis=-1)
        a = p[:, :, 0] - lam * p[:, :, 1]
        return jnp.einsum('bhqk,bkhe->bqhe', a, vf)

    o = lax.map(block, (qb, jnp.arange(nb)))
    o = o.transpose(1, 0, 2, 3, 4).reshape(B, S, H, 2 * d)
    o = _rmsnorm(o, head_g, eps=DIFF_NORM_EPS) * (1.0 - lam_init)
    return o.reshape(B, S, H * 2 * d).astype(h.dtype) @ w_out


def _causal_depthwise_conv(x, w, b):
    C = x.shape[-1]
    y = lax.conv_general_dilated(x, w[:, None, :].astype(x.dtype), window_strides=(1,),
                                 padding=[(SSM_CONV - 1, 0)],
                                 dimension_numbers=('NWC', 'WIO', 'NWC'),
                                 feature_group_count=C)
    return y + b.astype(x.dtype)


def _ssd_chunked_scan(x, dt, A, Bm, Cm):
    B, S, G, K, P = x.shape
    N = Bm.shape[-1]
    L = SSM_CHUNK
    Sp = -(-S // L) * L
    nc = Sp // L

    def chunks(t):
        t = jnp.pad(t.astype(jnp.float32), [(0, 0), (0, Sp - S)] + [(0, 0)] * (t.ndim - 2))
        return jnp.moveaxis(t.reshape(B, nc, L, *t.shape[2:]), 1, 0)

    causal = jnp.tril(jnp.ones((L, L), dtype=bool))

    def step(state, inp):
        xc, dtc, Bc, Cc = inp
        acum = jnp.cumsum(dtc * A, axis=1)
        seg = acum[:, :, None] - acum[:, None]
        decay = jnp.exp(jnp.where(causal[None, :, :, None, None], seg, -jnp.inf))
        cb = jnp.einsum('btgn,bsgn->btsg', Cc, Bc)
        dx = dtc[..., None] * xc
        y = jnp.einsum('btsgk,bsgkp->btgkp', cb[..., None] * decay, dx)
        y = y + jnp.einsum('btgn,bgkpn->btgkp', Cc, state) * jnp.exp(acum)[..., None]
        last = acum[:, -1]
        w_end = jnp.exp(last[:, None] - acum)
        state = state * jnp.exp(last)[..., None, None] + jnp.einsum(
            'bsgkp,bsgn->bgkpn', w_end[..., None] * dx, Bc)
        return state, y

    state0 = jnp.zeros((B, G, K, P, N), jnp.float32)
    _, ys = lax.scan(step, state0, (chunks(x), chunks(dt), chunks(Bm), chunks(Cm)))
    return jnp.moveaxis(ys, 0, 1).reshape(B, Sp, G, K, P)[:, :S]


def _ssd_mixer(h, w_in, conv_w, conv_b, dt_bias, A_log, D_skip, norm_g, w_out):
    B, S, _ = h.shape
    G, K, P, N, DI = SSM_GROUPS, SSM_HEADS_PER_GROUP, SSM_HEAD_DIM, SSM_STATE, SSM_D_INNER
    proj = h @ w_in
    z = proj[..., :DI]
    xBC = proj[..., DI:DI + SSM_CONV_DIM]
    dt_raw = proj[..., DI + SSM_CONV_DIM:]
    xBC = jax.nn.silu(_causal_depthwise_conv(xBC, conv_w, conv_b))
    xs = xBC[..., :DI].reshape(B, S, G, K, P)
    Bm = xBC[..., DI:DI + G * N].reshape(B, S, G, N)
    Cm = xBC[..., DI + G * N:].reshape(B, S, G, N)
    dt = jax.nn.softplus(dt_raw.astype(jnp.float32) + dt_bias.astype(jnp.float32)).reshape(B, S, G, K)
    A = -jnp.exp(A_log.astype(jnp.float32)).reshape(G, K)
    y = _ssd_chunked_scan(xs, dt, A, Bm, Cm)
    y = y + D_skip.astype(jnp.float32).reshape(G, K, 1) * xs.astype(jnp.float32)
    y = y.reshape(B, S, DI) * jax.nn.silu(z.astype(jnp.float32))
    y = _rmsnorm(y.reshape(B, S, G, DI // G), norm_g.reshape(G, DI // G)).reshape(B, S, DI)
    return y.astype(h.dtype) @ w_out


def _moe_ffn(h, router_w, router_bias, w_gate, w_up, w_down):
    B, S, D = h.shape
    t = h.reshape(-1, D)
    T = t.shape[0]
    scores = jax.nn.sigmoid(t.astype(jnp.float32) @ router_w.astype(jnp.float32))
    biased = (scores + router_bias.astype(jnp.float32)).reshape(T, N_EXPERT_GROUPS, EXPERTS_PER_GROUP)
    group_score = lax.top_k(biased, TOP_K)[0].sum(axis=-1)
    g_sel = jnp.argmax(group_score, axis=-1)
    in_group = biased[jnp.arange(T), g_sel]
    _, local = lax.top_k(in_group, TOP_K)
    expert = g_sel[:, None] * EXPERTS_PER_GROUP + local
    gate = jnp.take_along_axis(scores, expert, axis=1)
    gate = gate / jnp.sum(gate, axis=-1, keepdims=True)

    n_assign = T * TOP_K
    e_flat = expert.reshape(-1)
    w_flat = gate.reshape(-1)
    tok_flat = jnp.broadcast_to(jnp.arange(T)[:, None], (T, TOP_K)).reshape(-1)
    order = jnp.argsort(e_flat)
    e_sorted, tok_sorted, w_sorted = e_flat[order], tok_flat[order], w_flat[order]
    counts = jnp.bincount(e_flat, length=N_EXPERTS)
    starts = jnp.cumsum(counts) - counts
    padded = (counts + MOE_BLOCK - 1) // MOE_BLOCK * MOE_BLOCK
    pends = jnp.cumsum(padded)
    pstarts = pends - padded
    dest = pstarts[e_sorted] + (jnp.arange(n_assign) - starts[e_sorted])
    n_blocks = -(-n_assign // MOE_BLOCK) + N_EXPERTS
    rows = n_blocks * MOE_BLOCK
    rows_tok = jnp.zeros((rows,), jnp.int32).at[dest].set(tok_sorted.astype(jnp.int32))
    rows_w = jnp.zeros((rows,), jnp.float32).at[dest].set(w_sorted)
    block_e = jnp.minimum(jnp.searchsorted(pends, jnp.arange(n_blocks) * MOE_BLOCK, side='right'),
                          N_EXPERTS - 1)

    def expert_block(args):
        tok, e = args
        xb = t[tok]
        hb = jax.nn.silu(xb @ w_gate[e]) * (xb @ w_up[e])
        return hb @ w_down[e]

    ys = lax.map(expert_block, (rows_tok.reshape(n_blocks, MOE_BLOCK), block_e))
    out = jnp.zeros((T, D), jnp.float32).at[rows_tok].add(
        ys.reshape(rows, D).astype(jnp.float32) * rows_w[:, None])
    return out.astype(h.dtype).reshape(B, S, D)


def setup_inputs(seed: int = 0) -> dict:
    key = jax.random.key(seed)
    ks = iter(jax.random.split(key, 40))
    f32 = jnp.float32

    def nrm(shape, scale):
        return jax.random.normal(next(ks), shape, f32) * scale

    def gain(shape):
        return 1.0 + nrm(shape, 0.05)

    D, E, F = D_MODEL, N_EXPERTS, D_EXPERT
    x = nrm((BATCH, SEQ, D), 1.0)
    c = nrm((BATCH, D), 1.0)
    ada_w = nrm((DEPTH, D, 6 * D), 0.5 * D ** -0.5)
    ada_b = nrm((DEPTH, 6 * D), 0.02)
    norm1_g = gain((DEPTH, D))
    norm2_g = gain((DEPTH, D))
    router_w = nrm((D, E), D ** -0.5)
    router_bias = nrm((E,), 0.01)
    moe_w_gate = nrm((DEPTH, E, D, F), D ** -0.5)
    moe_w_up = nrm((DEPTH, E, D, F), D ** -0.5)
    moe_w_down = nrm((DEPTH, E, F, D), F ** -0.5)
    dil_w_in = nrm((N_DIL_LAYERS, D, DIL_IN), D ** -0.5)
    dil_w_out = nrm((N_DIL_LAYERS, DIL_GROUP_WIDTH, D), DIL_GROUP_WIDTH ** -0.5)
    diff_w_in = nrm((N_DIFF_LAYERS, D, DIFF_IN), D ** -0.5)
    diff_lam_q1 = nrm((N_DIFF_LAYERS, DIFF_HEAD_DIM), 0.1)
    diff_lam_k1 = nrm((N_DIFF_LAYERS, DIFF_HEAD_DIM), 0.1)
    diff_lam_q2 = nrm((N_DIFF_LAYERS, DIFF_HEAD_DIM), 0.1)
    diff_lam_k2 = nrm((N_DIFF_LAYERS, DIFF_HEAD_DIM), 0.1)
    diff_head_norm_g = gain((N_DIFF_LAYERS, 2 * DIFF_HEAD_DIM))
    diff_w_out = nrm((N_DIFF_LAYERS, D, D), D ** -0.5)
    ssm_w_in = nrm((N_SSM_LAYERS, D, SSM_IN), D ** -0.5)
    ssm_conv_w = nrm((N_SSM_LAYERS, SSM_CONV, SSM_CONV_DIM), SSM_CONV ** -0.5)
    ssm_conv_b = nrm((N_SSM_LAYERS, SSM_CONV_DIM), 0.02)
    u = jax.random.uniform(next(ks), (N_SSM_LAYERS, SSM_HEADS), f32)
    dt0 = jnp.exp(u * (math.log(0.1) - math.log(0.001)) + math.log(0.001))
    ssm_dt_bias = dt0 + jnp.log(-jnp.expm1(-dt0))
    ssm_A_log = jnp.log(jax.random.uniform(next(ks), (N_SSM_LAYERS, SSM_HEADS), f32, 1.0, 16.0))
    ssm_D = 1.0 + nrm((N_SSM_LAYERS, SSM_HEADS), 0.1)
    ssm_norm_g = gain((N_SSM_LAYERS, SSM_D_INNER))
    ssm_w_out = nrm((N_SSM_LAYERS, SSM_D_INNER, D), SSM_D_INNER ** -0.5)
    final_norm_g = gain((D,))
    return {"x": x, "c": c, "ada_w": ada_w, "ada_b": ada_b, "norm1_g": norm1_g, "norm2_g": norm2_g,
            "router_w": router_w, "router_bias": router_bias,
            "moe_w_gate": moe_w_gate, "moe_w_up": moe_w_up, "moe_w_down": moe_w_down,
            "dil_w_in": dil_w_in, "dil_w_out": dil_w_out,
            "diff_w_in": diff_w_in, "diff_lam_q1": diff_lam_q1, "diff_lam_k1": diff_lam_k1,
            "diff_lam_q2": diff_lam_q2, "diff_lam_k2": diff_lam_k2,
            "diff_head_norm_g": diff_head_norm_g, "diff_w_out": diff_w_out,
            "ssm_w_in": ssm_w_in, "ssm_conv_w": ssm_conv_w, "ssm_conv_b": ssm_conv_b,
            "ssm_dt_bias": ssm_dt_bias, "ssm_A_log": ssm_A_log, "ssm_D": ssm_D,
            "ssm_norm_g": ssm_norm_g, "ssm_w_out": ssm_w_out, "final_norm_g": final_norm_g}


def reference(x, c, ada_w, ada_b, norm1_g, norm2_g, router_w, router_bias,
              moe_w_gate, moe_w_up, moe_w_down, dil_w_in, dil_w_out,
              diff_w_in, diff_lam_q1, diff_lam_k1, diff_lam_q2, diff_lam_k2,
              diff_head_norm_g, diff_w_out, ssm_w_in, ssm_conv_w, ssm_conv_b,
              ssm_dt_bias, ssm_A_log, ssm_D, ssm_norm_g, ssm_w_out, final_norm_g):
    B, S, D = x.shape
    pos = jnp.arange(S)
    cond = jax.nn.silu(c)
    for i in range(DEPTH):
        mod = cond @ ada_w[i] + ada_b[i]
        sh1, sc1, g1, sh2, sc2, g2 = jnp.split(mod[:, None, :], 6, axis=-1)
        hmix = _rmsnorm(x, norm1_g[i]) * (1 + sc1) + sh1
        kind, j = i % N_MIXERS, i // N_MIXERS
        if kind == 0:
            y = _dilated_mixer(hmix, dil_w_in[j], dil_w_out[j], pos)
        elif kind == 1:
            y = _diff_mixer(hmix, diff_w_in[j], diff_lam_q1[j], diff_lam_k1[j], diff_lam_q2[j],
                            diff_lam_k2[j], diff_head_norm_g[j], diff_w_out[j], pos, i)
        else:
            y = _ssd_mixer(hmix, ssm_w_in[j], ssm_conv_w[j], ssm_conv_b[j], ssm_dt_bias[j],
                           ssm_A_log[j], ssm_D[j], ssm_norm_g[j], ssm_w_out[j])
        x = x + g1 * y
        hffn = _rmsnorm(x, norm2_g[i]) * (1 + sc2) + sh2
        x = x + g2 * _moe_ffn(hffn, router_w, router_bias, moe_w_gate[i], moe_w_up[i], moe_w_down[i])
    return _rmsnorm(x, final_norm_g)
```

```python
import functools
import math

import jax
import jax.numpy as jnp
from jax import lax
from jax.experimental import pallas as pl
from jax.experimental.pallas import tpu as pltpu

F32 = jnp.float32
BF16 = jnp.bfloat16
I32 = jnp.int32
HIGHEST = lax.Precision.HIGHEST

LANES = 128
SUBLANES = 8
VMEM_LIMIT_BYTES = 56 * 1024 * 1024

NORM_EPS = 1e-6
ROPE_THETA = 500000.0
ROPE_FRACTION = 4
HEAD_DIM = 64
ATTN_BLOCK = 128
DIL_CONFIGS = ((128, 1), (512, 4), (2048, 16))
DIFF_NORM_EPS = 1e-5
SSM_HEAD_DIM = 64
SSM_GROUPS = 4
SSM_STATE = 128
SSM_CONV = 4
SSM_CHUNK = 256
N_EXPERTS = 16
N_EXPERT_GROUPS = 4
EXPERTS_PER_GROUP = 4
MOE_BLOCK = 256
COL_CHUNK = 512
NEG = -1e30

_NT = (((1,), (1,)), ((), ()))
_TN = (((0,), (0,)), ((), ()))


def _params(*sem):
    return pltpu.CompilerParams(dimension_semantics=sem, vmem_limit_bytes=VMEM_LIMIT_BYTES)


def _silu(v):
    return v * jax.nn.sigmoid(v)


def _ada_kernel(c_ref, w_ref, b_ref, o_ref):
    cond = _silu(c_ref[...])
    o_ref[0] = jnp.dot(cond, w_ref[0], preferred_element_type=F32, precision=HIGHEST) + b_ref[0]


def _ada_mod(c, ada_w, ada_b):
    depth, d, n = ada_w.shape
    b = c.shape[0]
    tn = 1536
    return pl.pallas_call(
        _ada_kernel,
        grid=(depth, n // tn),
        in_specs=[
            pl.BlockSpec((b, d), lambda i, j: (0, 0)),
            pl.BlockSpec((1, d, tn), lambda i, j: (i, 0, j)),
            pl.BlockSpec((1, 1, tn), lambda i, j: (i, 0, j)),
        ],
        out_specs=pl.BlockSpec((1, b, tn), lambda i, j: (i, 0, j)),
        out_shape=jax.ShapeDtypeStruct((depth, b, n), F32),
        compiler_params=_params("arbitrary", "arbitrary"),
        name="ada_mod",
    )(c, ada_w, ada_b.reshape(depth, 1, n))


def _rope_tables(seq):
    r = HEAD_DIM // ROPE_FRACTION
    half = r // 2
    inv = jnp.power(ROPE_THETA, -jnp.arange(half, dtype=F32) * 2.0 / r)
    ang = jnp.arange(seq, dtype=F32)[:, None] * inv[None, :]
    cos, sin = jnp.cos(ang), jnp.sin(ang)
    ones = jnp.ones((seq, HEAD_DIM - r), F32)
    zeros = jnp.zeros((seq, HEAD_DIM - r), F32)
    zh = jnp.zeros((seq, half), F32)
    cos_t = jnp.concatenate([cos, cos, ones], axis=1)
    sin_a = jnp.concatenate([zh, sin, zeros], axis=1)
    sin_b = jnp.concatenate([-sin, zh, zeros], axis=1)
    rep = LANES // HEAD_DIM
    return tuple(jnp.tile(t, (1, rep)) for t in (cos_t, sin_a, sin_b))


def _modulated_norm(x, g, sc, sh):
    ms = jnp.mean(x * x, axis=-1, keepdims=True)
    return x * lax.rsqrt(ms + NORM_EPS) * g * (1.0 + sc) + sh


def _proj_kernel(*refs, chunks, n_out, use_rope):
    x_ref, sh_ref, sc_ref, g_ref, w_ref = refs[:5]
    pos = 5
    if use_rope:
        cos_ref, sa_ref, sb_ref = refs[5:8]
        pos = 8
    out_refs = refs[pos:pos + n_out]
    h = _modulated_norm(x_ref[0], g_ref[...], sc_ref[0], sh_ref[0]).astype(BF16)
    half = HEAD_DIM // ROPE_FRACTION // 2
    for c, (oi, off, scale) in enumerate(chunks):
        acc = jnp.dot(h, w_ref[:, c * COL_CHUNK:(c + 1) * COL_CHUNK], preferred_element_type=F32)
        if scale is None:
            out_refs[oi][0, :, off:off + COL_CHUNK] = acc.astype(out_refs[oi].dtype)
            continue
        cos, sa, sb = cos_ref[...], sa_ref[...], sb_ref[...]
        for s in range(COL_CHUNK // LANES):
            a = acc[:, s * LANES:(s + 1) * LANES]
            r = a * cos + pltpu.roll(a, half, 1) * sa + pltpu.roll(a, LANES - half, 1) * sb
            if scale != 1.0:
                r = r * scale
            out_refs[oi][0, :, off + s * LANES:off + (s + 1) * LANES] = r.astype(out_refs[oi].dtype)


def _proj(x, sh, sc, g, w, chunks, outs, tables, tm=512):
    b, s, d = x.shape
    n = w.shape[1]
    tm = min(tm, s)
    use_rope = any(c[2] is not None for c in chunks)
    in_specs = [
        pl.BlockSpec((1, tm, d), lambda bi, i: (bi, i, 0)),
        pl.BlockSpec((1, 1, d), lambda bi, i: (bi, 0, 0)),
        pl.BlockSpec((1, 1, d), lambda bi, i: (bi, 0, 0)),
        pl.BlockSpec((1, d), lambda bi, i: (0, 0)),
        pl.BlockSpec((d, n), lambda bi, i: (0, 0), pipeline_mode=pl.Buffered(1)),
    ]
    args = [x, sh, sc, g, w]
    if use_rope:
        in_specs += [pl.BlockSpec((tm, LANES), lambda bi, i: (i, 0))] * 3
        args += list(tables)
    return pl.pallas_call(
        functools.partial(_proj_kernel, chunks=tuple(chunks), n_out=len(outs), use_rope=use_rope),
        grid=(b, s // tm),
        in_specs=in_specs,
        out_specs=[pl.BlockSpec((1, tm, wd), lambda bi, i: (bi, i, 0)) for wd, _ in outs],
        out_shape=[jax.ShapeDtypeStruct((b, s, wd), dt) for wd, dt in outs],
        compiler_params=_params("parallel", "arbitrary"),
        name="norm_proj",
    )(*args)


def _outproj_kernel(y_ref, w_ref, x_ref, g_ref, o_ref):
    y = jnp.dot(y_ref[0], w_ref[...], preferred_element_type=F32)
    o_ref[0] = x_ref[0] + g_ref[0] * y


def _outproj(y, w, x, gate, tm=512):
    b, s, d = x.shape
    k = y.shape[-1]
    tm = min(tm, s)
    return pl.pallas_call(
        _outproj_kernel,
        grid=(b, s // tm),
        in_specs=[
            pl.BlockSpec((1, tm, k), lambda bi, i: (bi, i, 0)),
            pl.BlockSpec((k, d), lambda bi, i: (0, 0), pipeline_mode=pl.Buffered(1)),
            pl.BlockSpec((1, tm, d), lambda bi, i: (bi, i, 0)),
            pl.BlockSpec((1, 1, d), lambda bi, i: (bi, 0, 0)),
        ],
        out_specs=pl.BlockSpec((1, tm, d), lambda bi, i: (bi, i, 0)),
        out_shape=jax.ShapeDtypeStruct((b, s, d), F32),
        compiler_params=_params("parallel", "arbitrary"),
        name="out_proj",
    )(y, w, x, gate)


def _dil_kernel(q_ref, kc_ref, kp_ref, vc_ref, vp_ref, o_ref, lse_ref, kbuf, vbuf, *, tq, back, heads):
    n = pl.program_id(2)
    qb = ATTN_BLOCK
    kbuf[0:qb] = kp_ref[0]
    kbuf[qb:] = kc_ref[0]
    vbuf[0:qb] = vp_ref[0]
    vbuf[qb:] = vc_ref[0]
    qi = lax.broadcasted_iota(I32, (qb, 2 * qb), 0)
    kj = lax.broadcasted_iota(I32, (qb, 2 * qb), 1)
    rel = kj - qi
    band = (rel >= qb - back) & (rel <= qb)
    lane = lax.broadcasted_iota(I32, (qb, LANES), 1)
    for j in range(tq // qb):
        first_key = n * tq + (j - 1) * qb
        valid = band & (kj + first_key >= 0)
        q = q_ref[0, j * qb:(j + 1) * qb, :]
        kk = kbuf[j * qb:(j + 2) * qb, :]
        vv = vbuf[j * qb:(j + 2) * qb, :]
        lse_tile = jnp.zeros((qb, LANES), F32)
        for h in range(heads):
            hs = slice(h * HEAD_DIM, (h + 1) * HEAD_DIM)
            s = lax.dot_general(q[:, hs], kk[:, hs], _NT, preferred_element_type=F32)
            s = jnp.where(valid, s, NEG)
            m = jnp.max(s, axis=-1, keepdims=True)
            p = jnp.exp(s - m)
            l = jnp.sum(p, axis=-1, keepdims=True)
            o = jnp.dot(p.astype(BF16), vv[:, hs], preferred_element_type=F32) / l
            o_ref[0, j * qb:(j + 1) * qb, hs] = o.astype(o_ref.dtype)
            lse_tile = jnp.where(lane == h, m + jnp.log(l), lse_tile)
        lse_ref[0, j * qb:(j + 1) * qb, :] = lse_tile


def _dil_group(proj, g, window, dilation):
    b, s, c = proj.shape
    back = window // dilation
    assert back <= ATTN_BLOCK and s % (dilation * ATTN_BLOCK) == 0
    heads = c // (len(DIL_CONFIGS) * 3 * HEAD_DIM)
    width = heads * HEAD_DIM
    ncol = c // width
    ln = s // dilation
    tq = min(512, ln)
    view = proj.reshape(b, ln, dilation * c)
    qcol = g * 3
    sub = tq // ATTN_BLOCK

    def cur(off):
        return pl.BlockSpec((1, tq, width), lambda bi, r, n: (bi, n, r * ncol + qcol + off))

    def prev(off):
        return pl.BlockSpec((1, ATTN_BLOCK, width),
                            lambda bi, r, n: (bi, jnp.maximum(n * sub - 1, 0), r * ncol + qcol + off))

    o, lse = pl.pallas_call(
        functools.partial(_dil_kernel, tq=tq, back=back, heads=heads),
        grid=(b, dilation, ln // tq),
        in_specs=[cur(0), cur(1), prev(1), cur(2), prev(2)],
        out_specs=[
            pl.BlockSpec((1, tq, width), lambda bi, r, n: (bi, n, r)),
            pl.BlockSpec((1, tq, LANES), lambda bi, r, n: (bi, n, r)),
        ],
        out_shape=[
            jax.ShapeDtypeStruct((b, ln, dilation * width), BF16),
            jax.ShapeDtypeStruct((b, ln, dilation * LANES), F32),
        ],
        scratch_shapes=[pltpu.VMEM((tq + ATTN_BLOCK, width), BF16)] * 2,
        compiler_params=_params("parallel", "parallel", "arbitrary"),
        name=f"dil_attn_g{g}",
    )(view, view, view, view, view)
    return o.reshape(b, s, width), lse.reshape(b, s, LANES)


def _dil_out_kernel(o0_ref, o1_ref, o2_ref, l0_ref, l1_ref, l2_ref, w_ref, x_ref, g_ref, out_ref):
    ls = [l0_ref[0], l1_ref[0], l2_ref[0]]
    mx = jnp.maximum(jnp.maximum(ls[0], ls[1]), ls[2])
    es = [jnp.exp(v - mx) for v in ls]
    inv = 1.0 / (es[0] + es[1] + es[2])
    width = o0_ref.shape[-1]
    expand = (lax.broadcasted_iota(I32, (LANES, width), 0)
              == lax.broadcasted_iota(I32, (LANES, width), 1) // HEAD_DIM).astype(BF16)
    o = jnp.zeros(o0_ref.shape[1:], F32)
    for e, o_ref in zip(es, (o0_ref, o1_ref, o2_ref)):
        alpha = e * inv
        hi = alpha.astype(BF16)
        lo = (alpha - hi.astype(F32)).astype(BF16)
        a_full = (jnp.dot(hi, expand, preferred_element_type=F32)
                  + jnp.dot(lo, expand, preferred_element_type=F32))
        o = o + a_full * o_ref[0].astype(F32)
    y = jnp.dot(o.astype(BF16), w_ref[...], preferred_element_type=F32)
    out_ref[0] = x_ref[0] + g_ref[0] * y


def _dil_out(os_, lses, w, x, gate, tm=512):
    b, s, d = x.shape
    width = os_[0].shape[-1]
    tm = min(tm, s)
    tok = lambda wd: pl.BlockSpec((1, tm, wd), lambda bi, i: (bi, i, 0))
    return pl.pallas_call(
        _dil_out_kernel,
        grid=(b, s // tm),
        in_specs=[tok(width)] * 3 + [tok(LANES)] * 3 + [
            pl.BlockSpec((width, d), lambda bi, i: (0, 0)),
            tok(d),
            pl.BlockSpec((1, 1, d), lambda bi, i: (bi, 0, 0)),
        ],
        out_specs=tok(d),
        out_shape=jax.ShapeDtypeStruct((b, s, d), F32),
        compiler_params=_params("parallel", "arbitrary"),
        name="dil_out",
    )(*os_, *lses, w, x, gate)


def _dilated_layer(x, sh, sc, gate, g_norm, w_in, w_out, tables):
    n = w_in.shape[1]
    chunks = []
    for c in range(n // COL_CHUNK):
        kind = c % 3
        scale = (HEAD_DIM ** -0.5, 1.0, None)[kind]
        chunks.append((0, c * COL_CHUNK, scale))
    (proj,) = _proj(x, sh, sc, g_norm, w_in.astype(BF16), chunks, [(n, BF16)], tables)
    os_, lses = [], []
    for g, (window, dilation) in enumerate(DIL_CONFIGS):
        o, lse = _dil_group(proj, g, window, dilation)
        os_.append(o)
        lses.append(lse)
    return _dil_out(os_, lses, w_out.astype(BF16), x, gate)


def _diff_kernel(q_ref, k_ref, v_ref, lq1_ref, lk1_ref, lq2_ref, lk2_ref, hg_ref, o_ref,
                 m_sc, l_sc, acc_sc, *, t, lam_init):
    qi = pl.program_id(2)
    ki = pl.program_id(3)

    @pl.when(ki == 0)
    def _():
        m_sc[...] = jnp.full(m_sc.shape, NEG, F32)
        l_sc[...] = jnp.zeros(l_sc.shape, F32)
        acc_sc[...] = jnp.zeros(acc_sc.shape, F32)

    def step(masked):
        q = q_ref[0]
        k = k_ref[0]
        v = v_ref[0]
        if masked:
            causal = (lax.broadcasted_iota(I32, (t, t), 1) <= lax.broadcasted_iota(I32, (t, t), 0))
        for mi in range(2):
            hs = slice(mi * HEAD_DIM, (mi + 1) * HEAD_DIM)
            s = lax.dot_general(q[:, hs], k[:, hs], _NT, preferred_element_type=F32)
            if masked:
                s = jnp.where(causal, s, NEG)
            m_prev = m_sc[mi]
            m_new = jnp.maximum(m_prev, jnp.max(s, axis=-1, keepdims=True))
            alpha = jnp.exp(m_prev - m_new)
            p = jnp.exp(s - m_new)
            l_sc[mi] = alpha * l_sc[mi] + jnp.sum(p, axis=-1, keepdims=True)
            acc_sc[mi] = alpha * acc_sc[mi] + jnp.dot(p.astype(BF16), v, preferred_element_type=F32)
            m_sc[mi] = m_new

    @pl.when(ki < qi)
    def _():
        step(False)

    @pl.when(ki == qi)
    def _():
        step(True)
        lam = (jnp.exp(jnp.sum(lq1_ref[...] * lk1_ref[...], axis=-1, keepdims=True))
               - jnp.exp(jnp.sum(lq2_ref[...] * lk2_ref[...], axis=-1, keepdims=True)) + lam_init)
        o = acc_sc[0] / l_sc[0] - lam * (acc_sc[1] / l_sc[1])
        ms = jnp.mean(o * o, axis=-1, keepdims=True)
        o = o * lax.rsqrt(ms + DIFF_NORM_EPS) * hg_ref[...] * (1.0 - lam_init)
        o_ref[0] = o.astype(o_ref.dtype)


def _diff_attention(qkv, lam_q1, lam_k1, lam_q2, lam_k2, head_g, lam_init, t=512):
    b, s, c3 = qkv.shape
    d = c3 // 3
    vw = 2 * HEAD_DIM
    heads = d // vw
    t = min(t, s)
    nq = s // t
    vec = lambda n: pl.BlockSpec((1, n), lambda bi, h, qi, ki: (0, 0))
    return pl.pallas_call(
        functools.partial(_diff_kernel, t=t, lam_init=lam_init),
        grid=(b, heads, nq, nq),
        in_specs=[
            pl.BlockSpec((1, t, vw), lambda bi, h, qi, ki: (bi, qi, h)),
            pl.BlockSpec((1, t, vw), lambda bi, h, qi, ki: (bi, jnp.minimum(ki, qi), heads + h)),
            pl.BlockSpec((1, t, vw), lambda bi, h, qi, ki: (bi, jnp.minimum(ki, qi), 2 * heads + h)),
            vec(HEAD_DIM), vec(HEAD_DIM), vec(HEAD_DIM), vec(HEAD_DIM), vec(vw),
        ],
        out_specs=pl.BlockSpec((1, t, vw), lambda bi, h, qi, ki: (bi, qi, h)),
        out_shape=jax.ShapeDtypeStruct((b, s, d), BF16),
        scratch_shapes=[
            pltpu.VMEM((2, t, 1), F32),
            pltpu.VMEM((2, t, 1), F32),
            pltpu.VMEM((2, t, vw), F32),
        ],
        compiler_params=_params("parallel", "parallel", "parallel", "arbitrary"),
        name="diff_attn",
    )(qkv, qkv, qkv, lam_q1.reshape(1, -1), lam_k1.reshape(1, -1), lam_q2.reshape(1, -1),
      lam_k2.reshape(1, -1), head_g.reshape(1, -1))


def _diff_layer(x, sh, sc, gate, g_norm, w_in, lam_q1, lam_k1, lam_q2, lam_k2, head_g, w_out,
                tables, layer_idx):
    n = w_in.shape[1]
    third = n // 3 // COL_CHUNK
    chunks = []
    for c in range(n // COL_CHUNK):
        scale = (HEAD_DIM ** -0.5, 1.0, None)[c // third]
        chunks.append((0, c * COL_CHUNK, scale))
    (qkv,) = _proj(x, sh, sc, g_norm, w_in.astype(BF16), chunks, [(n, BF16)], tables)
    lam_init = 0.8 - 0.6 * math.exp(-0.3 * layer_idx)
    o = _diff_attention(qkv, lam_q1, lam_k1, lam_q2, lam_k2, head_g, lam_init)
    return _outproj(o, w_out.astype(BF16), x, gate)


def _causal_conv_silu(cur_ref, tail_ref, w_ref, b_ref):
    cur = cur_ref[0].astype(F32)
    rows = cur.shape[0]
    tail = tail_ref[...]
    w = w_ref[...]
    row8 = lax.broadcasted_iota(I32, (SUBLANES, cur.shape[1]), 0)
    acc = cur * w[SSM_CONV - 1:SSM_CONV] + b_ref[...]
    for k in range(1, SSM_CONV):
        sh = pltpu.roll(cur, k, 0)
        top = jnp.where(row8 < k, pltpu.roll(tail, k, 0), sh[0:SUBLANES])
        shifted = jnp.concatenate([top, sh[SUBLANES:]], axis=0)
        acc = acc + shifted * w[SSM_CONV - 1 - k:SSM_CONV - k]
    tail_ref[...] = cur[rows - SUBLANES:rows]
    return _silu(acc)


def _ssd_kernel(z_ref, x_ref, b_ref, c_ref, dt_ref, wx_ref, wb_ref, wc_ref, bx_ref, bb_ref, bc_ref,
                dtb_ref, alog_ref, dskip_ref, ng_ref, y_ref,
                state_sc, tx_sc, tb_sc, tc_sc, ybuf_sc, wdx_sc, *, hpg):
    ci = pl.program_id(2)
    L = x_ref.shape[1]
    P = SSM_HEAD_DIM

    @pl.when(ci == 0)
    def _():
        state_sc[...] = jnp.zeros(state_sc.shape, F32)
        tx_sc[...] = jnp.zeros(tx_sc.shape, F32)
        tb_sc[...] = jnp.zeros(tb_sc.shape, F32)
        tc_sc[...] = jnp.zeros(tc_sc.shape, F32)

    xs = _causal_conv_silu(x_ref, tx_sc, wx_ref, bx_ref)
    bm = _causal_conv_silu(b_ref, tb_sc, wb_ref, bb_ref).astype(BF16)
    cm = _causal_conv_silu(c_ref, tc_sc, wc_ref, bc_ref).astype(BF16)

    raw = dt_ref[0] + dtb_ref[...]
    dt = jnp.maximum(raw, 0.0) + jnp.log(1.0 + jnp.exp(-jnp.abs(raw)))
    a = -jnp.exp(alog_ref[...])
    da = dt * a
    ti = lax.broadcasted_iota(I32, (L, L), 0)
    si = lax.broadcasted_iota(I32, (L, L), 1)
    causal = ti >= si
    acum = jnp.dot(causal.astype(F32), da, preferred_element_type=F32, precision=HIGHEST)
    acum_t = acum.T
    last = acum[L - 1:L, :]
    eacum = jnp.exp(acum)
    w_end = jnp.exp(last - acum)
    elast = jnp.exp(last)

    cb = lax.dot_general(cm, bm, _NT, preferred_element_type=F32)
    y_inter = lax.dot_general(cm, state_sc[...].astype(BF16), _NT, preferred_element_type=F32)

    for k in range(hpg):
        hs = slice(k * P, (k + 1) * P)
        seg = acum[:, k:k + 1] - acum_t[k:k + 1, :]
        decay = jnp.exp(jnp.where(causal, seg, NEG))
        mk = (cb * decay).astype(BF16)
        dx = dt[:, k:k + 1] * xs[:, hs]
        y = jnp.dot(mk, dx.astype(BF16), preferred_element_type=F32)
        ybuf_sc[:, hs] = y + y_inter[:, hs] * eacum[:, k:k + 1]
        wdx_sc[:, hs] = (w_end[:, k:k + 1] * dx).astype(BF16)
        state_sc[hs, :] = state_sc[hs, :] * elast[:, k:k + 1]

    state_sc[...] += lax.dot_general(wdx_sc[...], bm, _TN, preferred_element_type=F32)

    y = ybuf_sc[...] + dskip_ref[...] * xs
    y = y * _silu(z_ref[0].astype(F32))
    ms = jnp.mean(y * y, axis=-1, keepdims=True)
    y_ref[0] = (y * lax.rsqrt(ms + NORM_EPS) * ng_ref[...]).astype(y_ref.dtype)


def _ssd_scan(main, dt, conv_w, conv_b, dt_bias, a_log, d_skip, norm_g):
    b, s, _ = main.shape
    G, N, P, L = SSM_GROUPS, SSM_STATE, SSM_HEAD_DIM, SSM_CHUNK
    di = norm_g.shape[-1]
    gw = di // G
    hpg = gw // P
    assert s % L == 0
    xoff = di // gw
    boff = 2 * di // N
    coff = boff + G
    seq = lambda wd, off: pl.BlockSpec((1, L, wd), lambda bi, g, c: (bi, c, off + g))
    par = lambda r, wd, off: pl.BlockSpec((r, wd), lambda bi, g, c: (0, off + g))
    return pl.pallas_call(
        functools.partial(_ssd_kernel, hpg=hpg),
        grid=(b, G, s // L),
        in_specs=[
            seq(gw, 0), seq(gw, xoff), seq(N, boff), seq(N, coff), seq(LANES, 0),
            par(SSM_CONV, gw, 0), par(SSM_CONV, N, di // N), par(SSM_CONV, N, di // N + G),
            par(1, gw, 0), par(1, N, di // N), par(1, N, di // N + G),
            par(1, LANES, 0), par(1, LANES, 0), par(1, gw, 0), par(1, gw, 0),
        ],
        out_specs=seq(gw, 0),
        out_shape=jax.ShapeDtypeStruct((b, s, di), BF16),
        scratch_shapes=[
            pltpu.VMEM((gw, N), F32),
            pltpu.VMEM((SUBLANES, gw), F32),
            pltpu.VMEM((SUBLANES, N), F32),
            pltpu.VMEM((SUBLANES, N), F32),
            pltpu.VMEM((L, gw), F32),
            pltpu.VMEM((L, gw), BF16),
        ],
        compiler_params=_params("parallel", "parallel", "arbitrary"),
        name="ssd_scan",
    )(main, main, main, main, dt, conv_w, conv_w, conv_w, conv_b, conv_b, conv_b,
      dt_bias, a_log, d_skip, norm_g)


def _pad_heads(v, groups):
    hpg = v.shape[0] // groups
    return jnp.pad(v.reshape(groups, hpg), ((0, 0), (0, LANES - hpg))).reshape(1, groups * LANES)


def _ssd_layer(x, sh, sc, gate, g_norm, w_in, conv_w, conv_b, dt_bias, a_log, d_skip, norm_g, w_out):
    G = SSM_GROUPS
    di = norm_g.shape[0]
    heads = dt_bias.shape[0]
    hpg = heads // G
    n_main = 2 * di + 2 * G * SSM_STATE
    d = w_in.shape[0]
    w_dt = w_in[:, n_main:].reshape(d, G, hpg)
    w_dt = jnp.pad(w_dt, ((0, 0), (0, 0), (0, LANES - hpg))).reshape(d, G * LANES)
    w = jnp.concatenate([w_in[:, :n_main], w_dt], axis=1).astype(BF16)
    chunks = [(0, c * COL_CHUNK, None) for c in range(n_main // COL_CHUNK)] + [(1, 0, None)]
    main, dt = _proj(x, sh, sc, g_norm, w, chunks, [(n_main, BF16), (G * LANES, F32)], None)
    y = _ssd_scan(main, dt, conv_w, conv_b.reshape(1, -1), _pad_heads(dt_bias, G), _pad_heads(a_log, G),
                  jnp.repeat(d_skip, SSM_HEAD_DIM).reshape(1, -1), norm_g.reshape(1, -1))
    return _outproj(y, w_out.astype(BF16), x, gate)


def _moe_pre_kernel(x_ref, sh_ref, sc_ref, g_ref, rwt_ref, rb_ref, h_ref, e_ref, w_ref, cnt_ref):
    first = (pl.program_id(0) == 0) & (pl.program_id(1) == 0)

    @pl.when(first)
    def _():
        cnt_ref[...] = jnp.zeros(cnt_ref.shape, F32)

    h = _modulated_norm(x_ref[0], g_ref[...], sc_ref[0], sh_ref[0])
    h_ref[0] = h
    logits = lax.dot_general(rwt_ref[...], h, _NT, preferred_element_type=F32, precision=HIGHEST)
    scores = jax.nn.sigmoid(logits)
    biased = scores + rb_ref[...]
    row = lambda a, e: a[e:e + 1, :]
    epg = EXPERTS_PER_GROUP

    best, g_sel = None, None
    for g in range(N_EXPERT_GROUPS):
        v = [row(biased, g * epg + j) for j in range(epg)]
        gs = None
        for i in range(epg):
            for j in range(i + 1, epg):
                pair = v[i] + v[j]
                gs = pair if gs is None else jnp.maximum(gs, pair)
        if best is None:
            best, g_sel = gs, jnp.zeros(gs.shape, I32)
        else:
            better = gs > best
            best = jnp.where(better, gs, best)
            g_sel = jnp.where(better, g, g_sel)

    def pick(a, j):
        out = row(a, j)
        for g in range(1, N_EXPERT_GROUPS):
            out = jnp.where(g_sel == g, row(a, g * epg + j), out)
        return out

    vb = [pick(biased, j) for j in range(epg)]
    vs = [pick(scores, j) for j in range(epg)]

    def argmax_first(vals, exclude):
        bv, bi, bs = None, None, None
        for j in range(epg):
            cand = vals[j] if exclude is None else jnp.where(exclude == j, -jnp.inf, vals[j])
            if bv is None:
                bv, bi, bs = cand, jnp.zeros(cand.shape, I32), vs[0]
            else:
                better = cand > bv
                bv = jnp.where(better, cand, bv)
                bi = jnp.where(better, j, bi)
                bs = jnp.where(better, vs[j], bs)
        return bi, bs

    i1, s1 = argmax_first(vb, None)
    i2, s2 = argmax_first(vb, i1)
    tot = s1 + s2
    e_ref[...] = jnp.concatenate([g_sel * epg + i1, g_sel * epg + i2], axis=0)
    w_ref[...] = jnp.concatenate([s1 / tot, s2 / tot], axis=0)
    eid = lax.broadcasted_iota(I32, logits.shape, 0)
    onehot = (eid == g_sel * epg + i1).astype(F32) + (eid == g_sel * epg + i2).astype(F32)
    cnt_ref[...] += jnp.sum(onehot, axis=1, keepdims=True)


def _moe_pre(x, sh, sc, g_norm, router_w, router_bias, tm=512):
    b, s, d = x.shape
    tm = min(tm, s)
    nt = s // tm
    t_all = b * s
    e = router_w.shape[1]
    return pl.pallas_call(
        _moe_pre_kernel,
        grid=(b, nt),
        in_specs=[
            pl.BlockSpec((1, tm, d), lambda bi, i: (bi, i, 0)),
            pl.BlockSpec((1, 1, d), lambda bi, i: (bi, 0, 0)),
            pl.BlockSpec((1, 1, d), lambda bi, i: (bi, 0, 0)),
            pl.BlockSpec((1, d), lambda bi, i: (0, 0)),
            pl.BlockSpec((e, d), lambda bi, i: (0, 0)),
            pl.BlockSpec((e, 1), lambda bi, i: (0, 0)),
        ],
        out_specs=[
            pl.BlockSpec((1, tm, d), lambda bi, i: (bi, i, 0)),
            pl.BlockSpec((2, tm), lambda bi, i: (0, bi * nt + i)),
            pl.BlockSpec((2, tm), lambda bi, i: (0, bi * nt + i)),
            pl.BlockSpec((e, LANES), lambda bi, i: (0, 0)),
        ],
        out_shape=[
            jax.ShapeDtypeStruct((b, s, d), F32),
            jax.ShapeDtypeStruct((2, t_all), I32),
            jax.ShapeDtypeStruct((2, t_all), F32),
            jax.ShapeDtypeStruct((e, LANES), F32),
        ],
        compiler_params=_params("arbitrary", "arbitrary"),
        name="moe_pre",
    )(x, sh, sc, g_norm, router_w.T, router_bias.reshape(e, 1))


def _moe_rank_kernel(e_ref, cnt_ref, dest_ref, meta_ref, pstart_sc, run_sc, *, sub, n_sub, nbp):
    ne = cnt_ref.shape[0]

    @pl.when(pl.program_id(0) == 0)
    def _():
        cnt = cnt_ref[...]
        padded = jnp.ceil(cnt / MOE_BLOCK) * MOE_BLOCK
        blk = lax.broadcasted_iota(I32, (1, nbp), 1).astype(F32) * MOE_BLOCK
        running = jnp.zeros((1, LANES), F32)
        block_e = jnp.zeros((1, nbp), F32)
        for e in range(ne):
            pstart_sc[e:e + 1, :] = running
            running = running + padded[e:e + 1, :]
            block_e = block_e + (running[:, 0:1] <= blk).astype(F32)
        run_sc[...] = jnp.zeros(run_sc.shape, F32)
        n_used = running[:, 0:1] / MOE_BLOCK
        meta_ref[...] = jnp.concatenate(
            [jnp.minimum(block_e, ne - 1.0), jnp.broadcast_to(n_used, (1, nbp))], axis=0).astype(I32)

    upper = (lax.broadcasted_iota(I32, (sub, sub), 0) < lax.broadcasted_iota(I32, (sub, sub), 1)).astype(BF16)
    eid = lax.broadcasted_iota(I32, (ne, sub), 0)
    for j in range(n_sub):
        e = e_ref[:, j * sub:(j + 1) * sub]
        oh0 = (eid == e[0:1, :]).astype(F32)
        oh1 = (eid == e[1:2, :]).astype(F32)
        both = jnp.concatenate([oh0, oh1], axis=0).astype(BF16)
        rank = jnp.dot(both, upper, preferred_element_type=F32)
        c0 = jnp.sum(oh0, axis=1, keepdims=True)
        c1 = jnp.sum(oh1, axis=1, keepdims=True)
        base0 = pstart_sc[:, 0:1] + run_sc[:, 0:1]
        base1 = base0 + c0
        d0 = jnp.sum(oh0 * (rank[0:ne] + base0), axis=0, keepdims=True)
        d1 = jnp.sum(oh1 * (rank[ne:2 * ne] + base1), axis=0, keepdims=True)
        dest_ref[:, j * sub:(j + 1) * sub] = jnp.concatenate([d0, d1], axis=0).astype(I32)
        run_sc[...] += c0 + c1


def _moe_rank(e, cnt, n_blocks):
    t_all = e.shape[1]
    sub = 256
    tr = min(2048, t_all)
    nbp = -(-n_blocks // LANES) * LANES
    ne = cnt.shape[0]
    return pl.pallas_call(
        functools.partial(_moe_rank_kernel, sub=sub, n_sub=tr // sub, nbp=nbp),
        grid=(t_all // tr,),
        in_specs=[
            pl.BlockSpec((2, tr), lambda i: (0, i)),
            pl.BlockSpec((ne, LANES), lambda i: (0, 0)),
        ],
        out_specs=[
            pl.BlockSpec((2, tr), lambda i: (0, i)),
            pl.BlockSpec((2, nbp), lambda i: (0, 0)),
        ],
        out_shape=[
            jax.ShapeDtypeStruct((2, t_all), I32),
            jax.ShapeDtypeStruct((2, nbp), I32),
        ],
        scratch_shapes=[pltpu.VMEM((ne, LANES), F32), pltpu.VMEM((ne, LANES), F32)],
        compiler_params=_params("arbitrary"),
        name="moe_rank",
    )(e, cnt)


def _row_copy(src_ref, s, dst_ref, d, sem):
    return pltpu.make_async_copy(src_ref.at[pl.ds(s, 1)], dst_ref.at[pl.ds(d, 1)], sem)


def _moe_dispatch_kernel(dest_hbm, h_ref, xs_in, xs_out, idx_smem, idx_sem, sem, *, tm):
    del xs_in
    i = pl.program_id(0)
    cp = pltpu.make_async_copy(dest_hbm.at[i], idx_smem, idx_sem)
    cp.start()
    cp.wait()

    def issue(t, carry):
        _row_copy(h_ref, t, xs_out, idx_smem[0, t], sem).start()
        _row_copy(h_ref, t, xs_out, idx_smem[1, t], sem).start()
        return carry

    lax.fori_loop(0, tm, issue, 0)

    def drain(t, carry):
        _row_copy(h_ref, 0, xs_out, 0, sem).wait()
        return carry

    lax.fori_loop(0, 2 * tm, drain, 0)


def _moe_dispatch(dest_tiles, h, rows, tm):
    t_all, d = h.shape
    zeros = jnp.zeros((rows, d), F32)
    return pl.pallas_call(
        functools.partial(_moe_dispatch_kernel, tm=tm),
        grid=(t_all // tm,),
        in_specs=[
            pl.BlockSpec(memory_space=pl.ANY),
            pl.BlockSpec((tm, d), lambda i: (i, 0)),
            pl.BlockSpec(memory_space=pl.ANY),
        ],
        out_specs=pl.BlockSpec(memory_space=pl.ANY),
        out_shape=jax.ShapeDtypeStruct((rows, d), F32),
        scratch_shapes=[pltpu.SMEM((2, tm), I32), pltpu.SemaphoreType.DMA(()), pltpu.SemaphoreType.DMA(())],
        input_output_aliases={2: 0},
        compiler_params=_params("arbitrary"),
        name="moe_dispatch",
    )(dest_tiles, h, zeros)


def _moe_expert_kernel(meta_ref, x_ref, wg_ref, wu_ref, wd_ref, y_ref):
    i = pl.program_id(0)

    @pl.when(i < meta_ref[1, 0])
    def _():
        x = x_ref[...].astype(BF16)
        gate = jnp.dot(x, wg_ref[0], preferred_element_type=F32)
        up = jnp.dot(x, wu_ref[0], preferred_element_type=F32)
        hidden = (_silu(gate) * up).astype(BF16)
        y_ref[...] = jnp.dot(hidden, wd_ref[0], preferred_element_type=F32)

    @pl.when(i >= meta_ref[1, 0])
    def _():
        y_ref[...] = jnp.zeros(y_ref.shape, F32)


def _moe_experts(meta, xs, w_gate, w_up, w_down, n_blocks):
    rows, d = xs.shape
    f = w_gate.shape[-1]
    wspec = lambda a, c: pl.BlockSpec((1, a, c), lambda i, meta: (meta[0, i], 0, 0))
    return pl.pallas_call(
        _moe_expert_kernel,
        grid_spec=pltpu.PrefetchScalarGridSpec(
            num_scalar_prefetch=1,
            grid=(n_blocks,),
            in_specs=[
                pl.BlockSpec((MOE_BLOCK, d), lambda i, meta: (i, 0)),
                wspec(d, f), wspec(d, f), wspec(f, d),
            ],
            out_specs=pl.BlockSpec((MOE_BLOCK, d), lambda i, meta: (i, 0)),
        ),
        out_shape=jax.ShapeDtypeStruct((rows, d), F32),
        compiler_params=_params("arbitrary"),
        name="moe_experts",
    )(meta, xs, w_gate, w_up, w_down)


def _moe_combine_kernel(dest_hbm, y_hbm, w_ref, x_ref, g_ref, o_ref, ybuf, idx_smem, idx_sem, sem,
                        *, tm, nt):
    tile = pl.program_id(0) * nt + pl.program_id(1)
    cp = pltpu.make_async_copy(dest_hbm.at[tile], idx_smem, idx_sem)
    cp.start()
    cp.wait()

    def issue(t, carry):
        _row_copy(y_hbm, idx_smem[0, t], ybuf.at[0], t, sem).start()
        _row_copy(y_hbm, idx_smem[1, t], ybuf.at[1], t, sem).start()
        return carry

    lax.fori_loop(0, tm, issue, 0)

    def drain(t, carry):
        _row_copy(y_hbm, 0, ybuf.at[0], 0, sem).wait()
        return carry

    lax.fori_loop(0, 2 * tm, drain, 0)
    w = w_ref[...]
    moe = w[:, 0:1] * ybuf[0] + w[:, 1:2] * ybuf[1]
    o_ref[0] = x_ref[0] + g_ref[0] * moe


def _moe_combine(dest_tiles, y, w_col, x, gate, tm):
    b, s, d = x.shape
    nt = s // tm
    return pl.pallas_call(
        functools.partial(_moe_combine_kernel, tm=tm, nt=nt),
        grid=(b, nt),
        in_specs=[
            pl.BlockSpec(memory_space=pl.ANY),
            pl.BlockSpec(memory_space=pl.ANY),
            pl.BlockSpec((tm, 2), lambda bi, i: (bi * nt + i, 0)),
            pl.BlockSpec((1, tm, d), lambda bi, i: (bi, i, 0)),
            pl.BlockSpec((1, 1, d), lambda bi, i: (bi, 0, 0)),
        ],
        out_specs=pl.BlockSpec((1, tm, d), lambda bi, i: (bi, i, 0)),
        out_shape=jax.ShapeDtypeStruct((b, s, d), F32),
        scratch_shapes=[
            pltpu.VMEM((2, tm, d), F32),
            pltpu.SMEM((2, tm), I32),
            pltpu.SemaphoreType.DMA(()),
            pltpu.SemaphoreType.DMA(()),
        ],
        compiler_params=_params("arbitrary", "arbitrary"),
        name="moe_combine",
    )(dest_tiles, y, w_col, x, gate)


def _moe_layer(x, sh, sc, gate, g_norm, router_w, router_bias, w_gate, w_up, w_down):
    b, s, d = x.shape
    t_all = b * s
    tm = min(256, s)
    n_blocks = -(-t_all * 2 // MOE_BLOCK) + N_EXPERTS
    rows = n_blocks * MOE_BLOCK
    h, e, w, cnt = _moe_pre(x, sh, sc, g_norm, router_w, router_bias)
    dest, meta = _moe_rank(e, cnt, n_blocks)
    dest_tiles = dest.reshape(2, t_all // tm, tm).transpose(1, 0, 2)
    xs = _moe_dispatch(dest_tiles, h.reshape(t_all, d), rows, tm)
    y = _moe_experts(meta, xs, w_gate.astype(BF16), w_up.astype(BF16), w_down.astype(BF16), n_blocks)
    return _moe_combine(dest_tiles, y, w.T, x, gate, tm)


def _final_norm_kernel(x_ref, g_ref, o_ref):
    x = x_ref[0]
    ms = jnp.mean(x * x, axis=-1, keepdims=True)
    o_ref[0] = x * lax.rsqrt(ms + NORM_EPS) * g_ref[...]


def _final_norm(x, g, tm=1024):
    b, s, d = x.shape
    tm = min(tm, s)
    return pl.pallas_call(
        _final_norm_kernel,
        grid=(b, s // tm),
        in_specs=[
            pl.BlockSpec((1, tm, d), lambda bi, i: (bi, i, 0)),
            pl.BlockSpec((1, d), lambda bi, i: (0, 0)),
        ],
        out_specs=pl.BlockSpec((1, tm, d), lambda bi, i: (bi, i, 0)),
        out_shape=jax.ShapeDtypeStruct((b, s, d), F32),
        compiler_params=_params("parallel", "arbitrary"),
        name="final_norm",
    )(x, g.reshape(1, d))


def kernel(x, c, ada_w, ada_b, norm1_g, norm2_g, router_w, router_bias, moe_w_gate, moe_w_up, moe_w_down, dil_w_in, dil_w_out, diff_w_in, diff_lam_q1, diff_lam_k1, diff_lam_q2, diff_lam_k2, diff_head_norm_g, diff_w_out, ssm_w_in, ssm_conv_w, ssm_conv_b, ssm_dt_bias, ssm_A_log, ssm_D, ssm_norm_g, ssm_w_out, final_norm_g):
    b, s, d = x.shape
    depth = ada_w.shape[0]
    mod = _ada_mod(c, ada_w, ada_b).reshape(depth, b, 6, 1, d)
    tables = _rope_tables(s)
    for i in range(depth):
        sh1, sc1, g1, sh2, sc2, g2 = (mod[i, :, j] for j in range(6))
        n1 = norm1_g[i].reshape(1, d)
        kind, j = i % 3, i // 3
        if kind == 0:
            x = _dilated_layer(x, sh1, sc1, g1, n1, dil_w_in[j], dil_w_out[j], tables)
        elif kind == 1:
            x = _diff_layer(x, sh1, sc1, g1, n1, diff_w_in[j], diff_lam_q1[j], diff_lam_k1[j],
                            diff_lam_q2[j], diff_lam_k2[j], diff_head_norm_g[j], diff_w_out[j], tables, i)
        else:
            x = _ssd_layer(x, sh1, sc1, g1, n1, ssm_w_in[j], ssm_conv_w[j], ssm_conv_b[j], ssm_dt_bias[j],
                           ssm_A_log[j], ssm_D[j], ssm_norm_g[j], ssm_w_out[j])
        x = _moe_layer(x, sh2, sc2, g2, norm2_g[i].reshape(1, d), router_w, router_bias,
                       moe_w_gate[i], moe_w_up[i], moe_w_down[i])
    return _final_norm(x, final_norm_g)
```

```python
import functools
import math

import jax
import jax.numpy as jnp
from jax import lax
from jax.experimental import pallas as pl
from jax.experimental.pallas import tpu as pltpu

F32 = jnp.float32
BF16 = jnp.bfloat16
I32 = jnp.int32
HIGHEST = lax.Precision.HIGHEST

LANES = 128
SUBLANES = 8
VMEM_LIMIT_BYTES = 56 * 1024 * 1024

NORM_EPS = 1e-6
ROPE_THETA = 500000.0
ROPE_FRACTION = 4
HEAD_DIM = 64
ATTN_BLOCK = 128
DIL_CONFIGS = ((128, 1), (512, 4), (2048, 16))
DIFF_NORM_EPS = 1e-5
SSM_HEAD_DIM = 64
SSM_GROUPS = 4
SSM_STATE = 128
SSM_CONV = 4
SSM_CHUNK = 256
N_EXPERTS = 16
N_EXPERT_GROUPS = 4
EXPERTS_PER_GROUP = 4
MOE_BLOCK = 256
COL_CHUNK = 512
NEG = -1e30

_NT = (((1,), (1,)), ((), ()))
_TN = (((0,), (0,)), ((), ()))


def _params(*sem):
    return pltpu.CompilerParams(dimension_semantics=sem, vmem_limit_bytes=VMEM_LIMIT_BYTES)


def _silu(v):
    return v * jax.nn.sigmoid(v)


def _ada_kernel(c_ref, w_ref, b_ref, o_ref):
    cond = _silu(c_ref[...])
    o_ref[0] = jnp.dot(cond, w_ref[0], preferred_element_type=F32, precision=HIGHEST) + b_ref[0]


def _ada_mod(c, ada_w, ada_b):
    depth, d, n = ada_w.shape
    b = c.shape[0]
    tn = 1536
    return pl.pallas_call(
        _ada_kernel,
        grid=(depth, n // tn),
        in_specs=[
            pl.BlockSpec((b, d), lambda i, j: (0, 0)),
            pl.BlockSpec((1, d, tn), lambda i, j: (i, 0, j)),
            pl.BlockSpec((1, 1, tn), lambda i, j: (i, 0, j)),
        ],
        out_specs=pl.BlockSpec((1, b, tn), lambda i, j: (i, 0, j)),
        out_shape=jax.ShapeDtypeStruct((depth, b, n), F32),
        compiler_params=_params("arbitrary", "arbitrary"),
        name="ada_mod",
    )(c, ada_w, ada_b.reshape(depth, 1, n))


def _rope_tables(seq):
    r = HEAD_DIM // ROPE_FRACTION
    half = r // 2
    inv = jnp.power(ROPE_THETA, -jnp.arange(half, dtype=F32) * 2.0 / r)
    ang = jnp.arange(seq, dtype=F32)[:, None] * inv[None, :]
    cos, sin = jnp.cos(ang), jnp.sin(ang)
    ones = jnp.ones((seq, HEAD_DIM - r), F32)
    zeros = jnp.zeros((seq, HEAD_DIM - r), F32)
    zh = jnp.zeros((seq, half), F32)
    cos_t = jnp.concatenate([cos, cos, ones], axis=1)
    sin_a = jnp.concatenate([zh, sin, zeros], axis=1)
    sin_b = jnp.concatenate([-sin, zh, zeros], axis=1)
    rep = LANES // HEAD_DIM
    lane_form = tuple(jnp.tile(t, (1, rep)) for t in (cos_t, sin_a, sin_b))
    return lane_form, tuple(t.T for t in lane_form)


def _modulated_norm(x, g, sc, sh):
    ms = jnp.mean(x * x, axis=-1, keepdims=True)
    return x * lax.rsqrt(ms + NORM_EPS) * g * (1.0 + sc) + sh


def _rope(a, cos, sa, sb, scale, axis):
    half = HEAD_DIM // ROPE_FRACTION // 2
    r = a * cos + pltpu.roll(a, half, axis) * sa + pltpu.roll(a, LANES - half, axis) * sb
    return r if scale == 1.0 else r * scale


def _proj_kernel(*refs, chunks, tplan, n_out, use_rope, has_perm):
    x_ref, sh_ref, sc_ref, g_ref, w_ref = refs[:5]
    pos = 5
    if use_rope:
        cos_ref, sa_ref, sb_ref = refs[pos:pos + 3]
        pos += 3
    if tplan is not None:
        wt_ref, cost_ref, sat_ref, sbt_ref = refs[pos:pos + 4]
        pos += 4
    out_refs = refs[pos:pos + n_out]
    perm_sc = refs[pos + n_out] if has_perm else None
    tm = x_ref.shape[1]
    h = _modulated_norm(x_ref[0], g_ref[...], sc_ref[0], sh_ref[0]).astype(BF16)
    tiles_per_chunk = COL_CHUNK // LANES
    for c, (oi, off, scale, dil) in enumerate(chunks):
        o_ref = out_refs[oi]
        acc = jnp.dot(h, w_ref[:, c * COL_CHUNK:(c + 1) * COL_CHUNK], preferred_element_type=F32)
        tiles = None
        if scale is not None:
            cos, sa, sb = cos_ref[...], sa_ref[...], sb_ref[...]
            tiles = [_rope(acc[:, s * LANES:(s + 1) * LANES], cos, sa, sb, scale, 1)
                     for s in range(tiles_per_chunk)]
        if dil == 1:
            if tiles is None:
                o_ref[0, :, off:off + COL_CHUNK] = acc.astype(o_ref.dtype)
            else:
                for s, tl in enumerate(tiles):
                    o_ref[0, :, off + s * LANES:off + (s + 1) * LANES] = tl.astype(o_ref.dtype)
            continue
        if tiles is None:
            tiles = [acc[:, s * LANES:(s + 1) * LANES] for s in range(tiles_per_chunk)]
        for s, tl in enumerate(tiles):
            perm_sc[s] = tl
        for rho in range(dil):
            for s in range(tiles_per_chunk):
                o_ref[0, rho, :, off + s * LANES:off + (s + 1) * LANES] = (
                    perm_sc[s, pl.ds(rho, tm // dil, stride=dil), :].astype(o_ref.dtype))
    if tplan is not None:
        oi, scale = tplan
        o_ref = out_refs[oi]
        acct = lax.dot_general(wt_ref[...], h, _NT, preferred_element_type=F32)
        cost, sat, sbt = cost_ref[...], sat_ref[...], sbt_ref[...]
        for s in range(acct.shape[0] // LANES):
            rows = slice(s * LANES, (s + 1) * LANES)
            o_ref[0, rows, :] = _rope(acct[rows, :], cost, sat, sbt, scale, 0).astype(o_ref.dtype)


def _proj(x, sh, sc, g, w, chunks, outs, tables, wt=None, tplan=None, tm=512):
    b, s, d = x.shape
    n = w.shape[1]
    tm = min(tm, s)
    use_rope = any(c[2] is not None for c in chunks)
    has_perm = any(c[3] > 1 for c in chunks)
    in_specs = [
        pl.BlockSpec((1, tm, d), lambda bi, i: (bi, i, 0)),
        pl.BlockSpec((1, 1, d), lambda bi, i: (bi, 0, 0)),
        pl.BlockSpec((1, 1, d), lambda bi, i: (bi, 0, 0)),
        pl.BlockSpec((1, d), lambda bi, i: (0, 0)),
        pl.BlockSpec((d, n), lambda bi, i: (0, 0), pipeline_mode=pl.Buffered(1)),
    ]
    args = [x, sh, sc, g, w]
    if use_rope:
        in_specs += [pl.BlockSpec((tm, LANES), lambda bi, i: (i, 0))] * 3
        args += list(tables[0])
    if tplan is not None:
        in_specs.append(pl.BlockSpec(wt.shape, lambda bi, i: (0, 0), pipeline_mode=pl.Buffered(1)))
        in_specs += [pl.BlockSpec((LANES, tm), lambda bi, i: (0, i))] * 3
        args += [wt] + list(tables[1])
    out_specs, out_shape = [], []
    for layout, wd, dt, dil in outs:
        if layout == "tok":
            out_specs.append(pl.BlockSpec((1, tm, wd), lambda bi, i: (bi, i, 0)))
            out_shape.append(jax.ShapeDtypeStruct((b, s, wd), dt))
        elif layout == "res":
            out_specs.append(pl.BlockSpec((1, dil, tm // dil, wd), lambda bi, i: (bi, 0, i, 0)))
            out_shape.append(jax.ShapeDtypeStruct((b, dil, s // dil, wd), dt))
        else:
            out_specs.append(pl.BlockSpec((1, wd, tm), lambda bi, i: (bi, 0, i)))
            out_shape.append(jax.ShapeDtypeStruct((b, wd, s), dt))
    return pl.pallas_call(
        functools.partial(_proj_kernel, chunks=tuple(chunks), tplan=tplan, n_out=len(outs),
                          use_rope=use_rope, has_perm=has_perm),
        grid=(b, s // tm),
        in_specs=in_specs,
        out_specs=out_specs,
        out_shape=out_shape,
        scratch_shapes=[pltpu.VMEM((COL_CHUNK // LANES, tm, LANES), F32)] if has_perm else [],
        compiler_params=_params("parallel", "arbitrary"),
        name="norm_proj",
    )(*args)


def _outproj_kernel(y_ref, w_ref, x_ref, g_ref, o_ref):
    y = jnp.dot(y_ref[0], w_ref[...], preferred_element_type=F32)
    o_ref[0] = x_ref[0] + g_ref[0] * y


def _outproj(y, w, x, gate, tm=512):
    b, s, d = x.shape
    k = y.shape[-1]
    tm = min(tm, s)
    return pl.pallas_call(
        _outproj_kernel,
        grid=(b, s // tm),
        in_specs=[
            pl.BlockSpec((1, tm, k), lambda bi, i: (bi, i, 0)),
            pl.BlockSpec((k, d), lambda bi, i: (0, 0), pipeline_mode=pl.Buffered(1)),
            pl.BlockSpec((1, tm, d), lambda bi, i: (bi, i, 0)),
            pl.BlockSpec((1, 1, d), lambda bi, i: (bi, 0, 0)),
        ],
        out_specs=pl.BlockSpec((1, tm, d), lambda bi, i: (bi, i, 0)),
        out_shape=jax.ShapeDtypeStruct((b, s, d), F32),
        compiler_params=_params("parallel", "arbitrary"),
        name="out_proj",
    )(y, w, x, gate)


def _dil_kernel(q_ref, kc_ref, kp_ref, vc_ref, vp_ref, o_ref, lse_ref, kbuf, vbuf, *, tq, back, heads):
    n = pl.program_id(2)
    qb = ATTN_BLOCK
    kbuf[0:qb] = kp_ref[0, 0]
    kbuf[qb:] = kc_ref[0, 0]
    vbuf[0:qb] = vp_ref[0, 0]
    vbuf[qb:] = vc_ref[0, 0]
    rows = heads * qb
    qi = lax.broadcasted_iota(I32, (rows, 2 * qb), 0) & (qb - 1)
    kj = lax.broadcasted_iota(I32, (rows, 2 * qb), 1)
    rel = kj - qi
    band = (rel >= qb - back) & (rel <= qb)
    lane = lax.broadcasted_iota(I32, (qb, LANES), 1)
    for j in range(tq // qb):
        first_key = n * tq + (j - 1) * qb
        valid = band & (kj + first_key >= 0)
        q = q_ref[0, 0, j * qb:(j + 1) * qb, :]
        kk = kbuf[j * qb:(j + 2) * qb, :]
        vv = vbuf[j * qb:(j + 2) * qb, :]
        s = jnp.concatenate(
            [lax.dot_general(q[:, h * HEAD_DIM:(h + 1) * HEAD_DIM], kk[:, h * HEAD_DIM:(h + 1) * HEAD_DIM],
                             _NT, preferred_element_type=F32) for h in range(heads)], axis=0)
        s = jnp.where(valid, s, NEG)
        m = jnp.max(s, axis=-1, keepdims=True)
        p = jnp.exp(s - m)
        l = jnp.sum(p, axis=-1, keepdims=True)
        inv = 1.0 / l
        lse = m + jnp.log(l)
        pb = p.astype(BF16)
        lse_tile = jnp.zeros((qb, LANES), F32)
        for h in range(heads):
            hs = slice(h * HEAD_DIM, (h + 1) * HEAD_DIM)
            hr = slice(h * qb, (h + 1) * qb)
            o = jnp.dot(pb[hr], vv[:, hs], preferred_element_type=F32) * inv[hr]
            o_ref[0, 0, j * qb:(j + 1) * qb, hs] = o.astype(o_ref.dtype)
            lse_tile = jnp.where(lane == h, lse[hr], lse_tile)
        lse_ref[0, 0, j * qb:(j + 1) * qb, :] = lse_tile


def _dil_group(proj, window, dilation):
    b, dil, ln, c = proj.shape
    back = window // dilation
    assert dil == dilation and back <= ATTN_BLOCK and ln % ATTN_BLOCK == 0
    width = c // 3
    heads = width // HEAD_DIM
    tq = min(512, ln)
    sub = tq // ATTN_BLOCK

    def cur(col):
        return pl.BlockSpec((1, 1, tq, width), lambda bi, r, n: (bi, r, n, col))

    def prev(col):
        return pl.BlockSpec((1, 1, ATTN_BLOCK, width),
                            lambda bi, r, n: (bi, r, jnp.maximum(n * sub - 1, 0), col))

    return pl.pallas_call(
        functools.partial(_dil_kernel, tq=tq, back=back, heads=heads),
        grid=(b, dil, ln // tq),
        in_specs=[cur(0), cur(1), prev(1), cur(2), prev(2)],
        out_specs=[
            pl.BlockSpec((1, 1, tq, width), lambda bi, r, n: (bi, r, n, 0)),
            pl.BlockSpec((1, 1, tq, LANES), lambda bi, r, n: (bi, r, n, 0)),
        ],
        out_shape=[
            jax.ShapeDtypeStruct((b, dil, ln, width), BF16),
            jax.ShapeDtypeStruct((b, dil, ln, LANES), F32),
        ],
        scratch_shapes=[pltpu.VMEM((tq + ATTN_BLOCK, width), BF16)] * 2,
        compiler_params=_params("parallel", "parallel", "arbitrary"),
        name=f"dil_attn_d{dilation}",
    )(proj, proj, proj, proj, proj)


def _dil_out_kernel(o0_ref, o1_ref, o2_ref, l0_ref, l1_ref, l2_ref, w_ref, x_ref, g_ref, out_ref,
                    o_sc, l1_sc, l2_sc):
    tm = x_ref.shape[1]

    def to_token_order(src_ref, dst_sc):
        dil = src_ref.shape[1]
        for rho in range(dil):
            src = src_ref[0, rho].astype(F32)
            for s in range(dst_sc.shape[0]):
                dst_sc[s, pl.ds(rho, tm // dil, stride=dil), :] = src[:, s * LANES:(s + 1) * LANES]
        return jnp.concatenate([dst_sc[s] for s in range(dst_sc.shape[0])], axis=1)

    ls = [l0_ref[0, 0], to_token_order(l1_ref, l1_sc), to_token_order(l2_ref, l2_sc)]
    mx =jnp.maximum(jnp.maximum(ls[0], ls[1]), ls[2])
    es = [jnp.exp(v - mx) for v in ls]
    inv = 1.0 / (es[0] + es[1] + es[2])
    width = o0_ref.shape[-1]
    expand = (lax.broadcasted_iota(I32, (LANES, width), 0)
              == lax.broadcasted_iota(I32, (LANES, width), 1) // HEAD_DIM).astype(BF16)
    o = jnp.zeros((tm, width), F32)
    for gi, (e, o_ref) in enumerate(zip(es, (o0_ref, o1_ref, o2_ref))):
        alpha = e * inv
        hi = alpha.astype(BF16)
        lo = (alpha - hi.astype(F32)).astype(BF16)
        a_full = (jnp.dot(hi, expand, preferred_element_type=F32)
                  + jnp.dot(lo, expand, preferred_element_type=F32))
        og = o_ref[0, 0].astype(F32) if gi == 0 else to_token_order(o_ref, o_sc)
        o = o + a_full * og
    y = jnp.dot(o.astype(BF16), w_ref[...], preferred_element_type=F32)
    out_ref[0] = x_ref[0] + g_ref[0] * y


def _dil_out(os_, lses, w, x, gate, tm=512):
    b, s, d = x.shape
    width = os_[0].shape[-1]
    tm = min(tm, s)
    tok = lambda wd: pl.BlockSpec((1, tm, wd), lambda bi, i: (bi, i, 0))
    res = lambda a: pl.BlockSpec((1, a.shape[1], tm // a.shape[1], a.shape[3]), lambda bi, i: (bi, 0, i, 0))
    return pl.pallas_call(
        _dil_out_kernel,
        grid=(b, s // tm),
        in_specs=[res(a) for a in os_] + [res(a) for a in lses] + [
            pl.BlockSpec((width, d), lambda bi, i: (0, 0)),
            tok(d),
            pl.BlockSpec((1, 1, d), lambda bi, i: (bi, 0, 0)),
        ],
        out_specs=tok(d),
        out_shape=jax.ShapeDtypeStruct((b, s, d), F32),
        scratch_shapes=[pltpu.VMEM((width // LANES, tm, LANES), F32), pltpu.VMEM((1, tm, LANES), F32),
                        pltpu.VMEM((1, tm, LANES), F32)],
        compiler_params=_params("parallel", "arbitrary"),
        name="dil_out",
    )(*os_, *lses, w, x, gate)


def _dilated_layer(x, sh, sc, gate, g_norm, w_in, w_out, tables):
    b, s, _ = x.shape
    n = w_in.shape[1]
    gw = n // len(DIL_CONFIGS)
    chunks, outs = [], []
    for g, (_, dilation) in enumerate(DIL_CONFIGS):
        for kind in range(3):
            scale = (HEAD_DIM ** -0.5, 1.0, None)[kind]
            chunks.append((g, kind * COL_CHUNK, scale, dilation))
        outs.append(("tok" if dilation == 1 else "res", gw, BF16, dilation))
    projs = _proj(x, sh, sc, g_norm, w_in.astype(BF16), chunks, outs, tables)
    os_, lses = [], []
    for proj, (window, dilation) in zip(projs, DIL_CONFIGS):
        o, lse = _dil_group(proj.reshape(b, dilation, s // dilation, gw), window, dilation)
        os_.append(o)
        lses.append(lse)
    return _dil_out(os_, lses, w_out.astype(BF16), x, gate)


def _diff_kernel(qi_tab, ki_tab, q_ref, kt_ref, v_ref, lq1_ref, lk1_ref, lq2_ref, lk2_ref, hg_ref, o_ref,
                 m_sc, acc_sc, *, t, r, lam_init):
    pair = pl.program_id(2)
    qi = qi_tab[pair]
    ki = ki_tab[pair]
    vw = v_ref.shape[-1]

    @pl.when(ki == 0)
    def _():
        m_sc[...] = jnp.full(m_sc.shape, NEG, F32)
        acc_sc[...] = jnp.zeros(acc_sc.shape, F32)

    def step(diagonal, r):
        kt = kt_ref[0]
        v_aug = jnp.concatenate([v_ref[0], jnp.ones((t, LANES), BF16)], axis=1)
        lane = lax.broadcasted_iota(I32, (r, vw), 1)
        work = [(mi, c) for mi in range(2) for c in range(t // r)]

        def scores(mi, c):
            in_half = (lane >= mi * HEAD_DIM) & (lane < (mi + 1) * HEAD_DIM)
            ncols = (c + 1) * r if diagonal else t
            q_c = jnp.where(in_half, q_ref[0, c * r:(c + 1) * r, :], jnp.zeros((r, vw), BF16))
            return jnp.dot(q_c, kt[:, :ncols], preferred_element_type=F32)

        ahead = 3
        pending = [scores(*wk) for wk in work[:ahead]]
        for idx, (mi, c) in enumerate(work):
            s = pending.pop(0)
            if idx + ahead < len(work):
                pending.append(scores(*work[idx + ahead]))
            ncols = s.shape[1]
            if diagonal:
                col = lax.broadcasted_iota(I32, (r, ncols), 1)
                row = lax.broadcasted_iota(I32, (r, ncols), 0) + c * r
                s = jnp.where(col <= row, s, NEG)
            tiles = [s[:, j * LANES:(j + 1) * LANES] for j in range(ncols // LANES)]
            tmax = functools.reduce(jnp.maximum, tiles)
            srows = slice(mi * t + c * r, mi * t + (c + 1) * r)
            m_prev = m_sc[srows, :]
            m_new = jnp.maximum(m_prev, jnp.max(tmax, axis=-1, keepdims=True))
            alpha = jnp.exp2(m_prev - m_new)
            p = jnp.concatenate([jnp.exp2(tl - m_new) for tl in tiles], axis=1).astype(BF16)
            pv = jnp.dot(p, v_aug[:ncols], preferred_element_type=F32)
            acc_sc[srows, :] = jnp.concatenate([alpha, alpha], axis=1) * acc_sc[srows, :] + pv
            m_sc[srows, :] = m_new

    @pl.when(ki < qi)
    def _():
        step(False, 2 * r)

    @pl.when(ki == qi)
    def _():
        step(True, 2 * r)
        lam = (jnp.exp(jnp.sum(lq1_ref[...] * lk1_ref[...], axis=-1, keepdims=True))
               - jnp.exp(jnp.sum(lq2_ref[...] * lk2_ref[...], axis=-1, keepdims=True)) + lam_init)
        o = acc_sc[:t, :vw] / acc_sc[:t, vw:] - lam * (acc_sc[t:, :vw] / acc_sc[t:, vw:])
        ms = jnp.mean(o * o, axis=-1, keepdims=True)
        o = o * lax.rsqrt(ms + DIFF_NORM_EPS) * hg_ref[...] * (1.0 - lam_init)
        o_ref[0] = o.astype(o_ref.dtype)


def _diff_attention(q, kt, v, lam_q1, lam_k1, lam_q2, lam_k2, head_g, lam_init, t=1024, r=128):
    b, s, d = q.shape
    vw = 2 * HEAD_DIM
    assert vw == LANES
    heads = d // vw
    t = min(t, s)
    r = min(r, t // 2)
    nq = s // t
    pairs = [(qi, ki) for qi in range(nq) for ki in range(qi + 1)]
    qi_tab = jnp.asarray([p[0] for p in pairs], I32)
    ki_tab = jnp.asarray([p[1] for p in pairs], I32)
    vec = lambda n: pl.BlockSpec((1, n), lambda bi, h, p, qt, kt_: (0, 0))
    return pl.pallas_call(
        functools.partial(_diff_kernel, t=t, r=r, lam_init=lam_init),
        grid_spec=pltpu.PrefetchScalarGridSpec(
            num_scalar_prefetch=2,
            grid=(b, heads, len(pairs)),
            in_specs=[
                pl.BlockSpec((1, t, vw), lambda bi, h, p, qt, kt_: (bi, qt[p], h)),
                pl.BlockSpec((1, vw, t), lambda bi, h, p, qt, kt_: (bi, h, kt_[p])),
                pl.BlockSpec((1, t, vw), lambda bi, h, p, qt, kt_: (bi, kt_[p], h)),
                vec(HEAD_DIM), vec(HEAD_DIM), vec(HEAD_DIM), vec(HEAD_DIM), vec(vw),
            ],
            out_specs=pl.BlockSpec((1, t, vw), lambda bi, h, p, qt, kt_: (bi, qt[p], h)),
            scratch_shapes=[
                pltpu.VMEM((2 * t, LANES), F32),
                pltpu.VMEM((2 * t, vw + LANES), F32),
            ],
        ),
        out_shape=jax.ShapeDtypeStruct((b, s, d), BF16),
        compiler_params=_params("parallel", "parallel", "arbitrary"),
        name="diff_attn",
    )(qi_tab, ki_tab, q, kt, v, lam_q1.reshape(1, -1), lam_k1.reshape(1, -1), lam_q2.reshape(1, -1),
      lam_k2.reshape(1, -1), head_g.reshape(1, -1))


def _diff_layer(x, sh, sc, gate, g_norm, w_in, lam_q1, lam_k1, lam_q2, lam_k2, head_g, w_out,
                tables, layer_idx):
    d = w_in.shape[1] // 3
    nch = d // COL_CHUNK
    q_scale = HEAD_DIM ** -0.5 * math.log2(math.e)
    chunks = ([(0, c * COL_CHUNK, q_scale, 1) for c in range(nch)]
              + [(1, c * COL_CHUNK, None, 1) for c in range(nch)])
    w_qv = jnp.concatenate([w_in[:, :d], w_in[:, 2 * d:]], axis=1).astype(BF16)
    w_kt = w_in[:, d:2 * d].T.astype(BF16)
    q, v, kt = _proj(x, sh, sc, g_norm, w_qv, chunks,
                     [("tok", d, BF16, 1), ("tok", d, BF16, 1), ("T", d, BF16, 1)], tables,
                     wt=w_kt, tplan=(2, 1.0))
    lam_init = 0.8 - 0.6 * math.exp(-0.3 * layer_idx)
    o = _diff_attention(q, kt, v, lam_q1, lam_k1, lam_q2, lam_k2, head_g, lam_init)
    return _outproj(o, w_out.astype(BF16), x, gate)


def _causal_conv_silu(cur_ref, tail_ref, w_ref, b_ref):
    cur = cur_ref[0].astype(F32)
    rows = cur.shape[0]
    tail = tail_ref[...]
    w = w_ref[...]
    row8 = lax.broadcasted_iota(I32, (SUBLANES, cur.shape[1]), 0)
    acc = cur * w[SSM_CONV - 1:SSM_CONV] + b_ref[...]
    for k in range(1, SSM_CONV):
        sh = pltpu.roll(cur, k, 0)
        top = jnp.where(row8 < k, pltpu.roll(tail, k, 0), sh[0:SUBLANES])
        shifted = jnp.concatenate([top, sh[SUBLANES:]], axis=0)
        acc = acc + shifted * w[SSM_CONV - 1 - k:SSM_CONV - k]
    tail_ref[...] = cur[rows - SUBLANES:rows]
    return _silu(acc)


def _ssd_kernel(z_ref, x_ref, b_ref, c_ref, dt_ref, wx_ref, wb_ref, wc_ref, bx_ref, bb_ref, bc_ref,
                dtb_ref, alog_ref, dskip_ref, ng_ref, y_ref,
                state_sc, tx_sc, tb_sc, tc_sc, ybuf_sc, wdx_sc, *, hpg):
    ci = pl.program_id(2)
    L = x_ref.shape[1]
    P = SSM_HEAD_DIM

    @pl.when(ci == 0)
    def _():
        state_sc[...] = jnp.zeros(state_sc.shape, F32)
        tx_sc[...] = jnp.zeros(tx_sc.shape, F32)
        tb_sc[...] = jnp.zeros(tb_sc.shape, F32)
        tc_sc[...] = jnp.zeros(tc_sc.shape, F32)

    xs = _causal_conv_silu(x_ref, tx_sc, wx_ref, bx_ref)
    bm = _causal_conv_silu(b_ref, tb_sc, wb_ref, bb_ref).astype(BF16)
    cm = _causal_conv_silu(c_ref, tc_sc, wc_ref, bc_ref).astype(BF16)

    raw = dt_ref[0] + dtb_ref[...]
    dt = jnp.maximum(raw, 0.0) + jnp.log(1.0 + jnp.exp(-jnp.abs(raw)))
    a = -jnp.exp(alog_ref[...])
    da = dt * a
    ti = lax.broadcasted_iota(I32, (L, L), 0)
    si = lax.broadcasted_iota(I32, (L, L), 1)
    causal = ti >= si
    acum = jnp.dot(causal.astype(F32), da, preferred_element_type=F32, precision=HIGHEST)
    acum_t = acum.T
    last = acum[L - 1:L, :]
    eacum = jnp.exp(acum)
    w_end = jnp.exp(last - acum)
    elast = jnp.exp(last)

    cb = lax.dot_general(cm, bm, _NT, preferred_element_type=F32)
    y_inter = lax.dot_general(cm, state_sc[...].astype(BF16), _NT, preferred_element_type=F32)

    for k in range(hpg):
        hs = slice(k * P, (k + 1) * P)
        seg = acum[:, k:k + 1] - acum_t[k:k + 1, :]
        decay = jnp.exp(jnp.where(causal, seg, NEG))
        mk = (cb * decay).astype(BF16)
        dx = dt[:, k:k + 1] * xs[:, hs]
        y = jnp.dot(mk, dx.astype(BF16), preferred_element_type=F32)
        ybuf_sc[:, hs] = y + y_inter[:, hs] * eacum[:, k:k + 1]
        wdx_sc[:, hs] = (w_end[:, k:k + 1] * dx).astype(BF16)
        state_sc[hs, :] = state_sc[hs, :] * elast[:, k:k + 1]

    state_sc[...] += lax.dot_general(wdx_sc[...], bm, _TN, preferred_element_type=F32)

    y = ybuf_sc[...] + dskip_ref[...] * xs
    y = y * _silu(z_ref[0].astype(F32))
    ms = jnp.mean(y * y, axis=-1, keepdims=True)
    y_ref[0] = (y * lax.rsqrt(ms + NORM_EPS) * ng_ref[...]).astype(y_ref.dtype)


def _ssd_scan(main, dt, conv_w, conv_b, dt_bias, a_log, d_skip, norm_g):
    b, s, _ = main.shape
    G, N, P, L = SSM_GROUPS, SSM_STATE, SSM_HEAD_DIM, SSM_CHUNK
    di = norm_g.shape[-1]
    gw = di // G
    hpg = gw // P
    assert s % L == 0
    xoff = di // gw
    boff = 2 * di // N
    coff = boff + G
    seq = lambda wd, off: pl.BlockSpec((1, L, wd), lambda bi, g, c: (bi, c, off + g))
    par = lambda r, wd, off: pl.BlockSpec((r, wd), lambda bi, g, c: (0, off + g))
    return pl.pallas_call(
        functools.partial(_ssd_kernel, hpg=hpg),
        grid=(b, G, s // L),
        in_specs=[
            seq(gw, 0), seq(gw, xoff), seq(N, boff), seq(N, coff), seq(LANES, 0),
            par(SSM_CONV, gw, 0), par(SSM_CONV, N, di // N), par(SSM_CONV, N, di // N + G),
            par(1, gw, 0), par(1, N, di // N), par(1, N, di // N + G),
            par(1, LANES, 0), par(1, LANES, 0), par(1, gw, 0), par(1, gw, 0),
        ],
        out_specs=seq(gw, 0),
        out_shape=jax.ShapeDtypeStruct((b, s, di), BF16),
        scratch_shapes=[
            pltpu.VMEM((gw, N), F32),
            pltpu.VMEM((SUBLANES, gw), F32),
            pltpu.VMEM((SUBLANES, N), F32),
            pltpu.VMEM((SUBLANES, N), F32),
            pltpu.VMEM((L, gw), F32),
            pltpu.VMEM((L, gw), BF16),
        ],
        compiler_params=_params("parallel", "parallel", "arbitrary"),
        name="ssd_scan",
    )(main, main, main, main, dt, conv_w, conv_w, conv_w, conv_b, conv_b, conv_b,
      dt_bias, a_log, d_skip, norm_g)


def _pad_heads(v, groups):
    hpg = v.shape[0] // groups
    return jnp.pad(v.reshape(groups, hpg), ((0, 0), (0, LANES - hpg))).reshape(1, groups * LANES)


def _ssd_layer(x, sh, sc, gate, g_norm, w_in, conv_w, conv_b, dt_bias, a_log, d_skip, norm_g, w_out):
    G = SSM_GROUPS
    di = norm_g.shape[0]
    heads = dt_bias.shape[0]
    hpg = heads // G
    n_main = 2 * di + 2 * G * SSM_STATE
    d = w_in.shape[0]
    w_dt = w_in[:, n_main:].reshape(d, G, hpg)
    w_dt = jnp.pad(w_dt, ((0, 0), (0, 0), (0, LANES - hpg))).reshape(d, G * LANES)
    w = jnp.concatenate([w_in[:, :n_main], w_dt], axis=1).astype(BF16)
    chunks = [(0, c * COL_CHUNK, None, 1) for c in range(n_main // COL_CHUNK)] + [(1, 0, None, 1)]
    main, dt = _proj(x, sh, sc, g_norm, w, chunks, [("tok", n_main, BF16, 1), ("tok", G * LANES, F32, 1)], None)
    y = _ssd_scan(main, dt, conv_w, conv_b.reshape(1, -1), _pad_heads(dt_bias, G), _pad_heads(a_log, G),
                  jnp.repeat(d_skip, SSM_HEAD_DIM).reshape(1, -1), norm_g.reshape(1, -1))
    return _outproj(y, w_out.astype(BF16), x, gate)


def _moe_pre_kernel(x_ref, sh_ref, sc_ref, g_ref, rwt_ref, rb_ref, h_ref, e_ref, w_ref, cnt_ref):
    first = (pl.program_id(0) == 0) & (pl.program_id(1) == 0)

    @pl.when(first)
    def _():
        cnt_ref[...] = jnp.zeros(cnt_ref.shape, F32)

    tm, d = x_ref.shape[1], x_ref.shape[2]
    pieces = d // LANES
    h = _modulated_norm(x_ref[0], g_ref[...], sc_ref[0], sh_ref[0])
    for s in range(pieces):
        h_ref[pl.ds(s, tm, stride=pieces), :] = h[:, s * LANES:(s + 1) * LANES]
    logits = lax.dot_general(rwt_ref[...], h, _NT, preferred_element_type=F32, precision=HIGHEST)
    scores = jax.nn.sigmoid(logits)
    biased = scores + rb_ref[...]
    row = lambda a, e: a[e:e + 1, :]
    epg = EXPERTS_PER_GROUP

    best, g_sel = None, None
    for g in range(N_EXPERT_GROUPS):
        v = [row(biased, g * epg + j) for j in range(epg)]
        gs = None
        for i in range(epg):
            for j in range(i + 1, epg):
                pair = v[i] + v[j]
                gs = pair if gs is None else jnp.maximum(gs, pair)
        if best is None:
            best, g_sel = gs, jnp.zeros(gs.shape, I32)
        else:
            better = gs > best
            best = jnp.where(better, gs, best)
            g_sel = jnp.where(better, g, g_sel)

    def pick(a, j):
        out = row(a, j)
        for g in range(1, N_EXPERT_GROUPS):
            out = jnp.where(g_sel == g, row(a, g * epg + j), out)
        return out

    vb = [pick(biased, j) for j in range(epg)]
    vs = [pick(scores, j) for j in range(epg)]

    def argmax_first(vals, exclude):
        bv, bi, bs = None, None, None
        for j in range(epg):
            cand = vals[j] if exclude is None else jnp.where(exclude == j, -jnp.inf, vals[j])
            if bv is None:
                bv, bi, bs = cand, jnp.zeros(cand.shape, I32), vs[0]
            else:
                better = cand > bv
                bv = jnp.where(better, cand, bv)
                bi = jnp.where(better, j, bi)
                bs = jnp.where(better, vs[j], bs)
        return bi, bs

    i1, s1 = argmax_first(vb, None)
    i2, s2 = argmax_first(vb, i1)
    tot = s1 + s2
    e_ref[...] = jnp.concatenate([g_sel * epg + i1, g_sel * epg + i2], axis=0)
    w_ref[...] = jnp.concatenate([s1 / tot, s2 / tot], axis=0)
    eid = lax.broadcasted_iota(I32, logits.shape, 0)
    onehot = (eid == g_sel * epg + i1).astype(F32) + (eid == g_sel * epg + i2).astype(F32)
    cnt_ref[...] += jnp.sum(onehot, axis=1, keepdims=True)


def _moe_pre(x, sh, sc, g_norm, router_w, router_bias, tm=512):
    b, s, d = x.shape
    tm = min(tm, s)
    nt = s // tm
    t_all = b * s
    e = router_w.shape[1]
    pieces = d // LANES
    return pl.pallas_call(
        _moe_pre_kernel,
        grid=(b, nt),
        in_specs=[
            pl.BlockSpec((1, tm, d), lambda bi, i: (bi, i, 0)),
            pl.BlockSpec((1, 1, d), lambda bi, i: (bi, 0, 0)),
            pl.BlockSpec((1, 1, d), lambda bi, i: (bi, 0, 0)),
            pl.BlockSpec((1, d), lambda bi, i: (0, 0)),
            pl.BlockSpec((e, d), lambda bi, i: (0, 0)),
            pl.BlockSpec((e, 1), lambda bi, i: (0, 0)),
        ],
        out_specs=[
            pl.BlockSpec((tm * pieces, LANES), lambda bi, i: (bi * nt + i, 0)),
            pl.BlockSpec((2, tm), lambda bi, i: (0, bi * nt + i)),
            pl.BlockSpec((2, tm), lambda bi, i: (0, bi * nt + i)),
            pl.BlockSpec((e, LANES), lambda bi, i: (0, 0)),
        ],
        out_shape=[
            jax.ShapeDtypeStruct((t_all * pieces, LANES), F32),
            jax.ShapeDtypeStruct((2, t_all), I32),
            jax.ShapeDtypeStruct((2, t_all), F32),
            jax.ShapeDtypeStruct((e, LANES), F32),
        ],
        compiler_params=_params("arbitrary", "arbitrary"),
        name="moe_pre",
    )(x, sh, sc, g_norm, router_w.T, router_bias.reshape(e, 1))


def _moe_rank_kernel(e_ref, cnt_ref, dest_ref, meta_ref, pstart_sc, run_sc, *, sub, n_sub, nbp):
    ne = cnt_ref.shape[0]

    @pl.when(pl.program_id(0) == 0)
    def _():
        cnt = cnt_ref[...]
        padded = jnp.ceil(cnt / MOE_BLOCK) * MOE_BLOCK
        blk = lax.broadcasted_iota(I32, (1, nbp), 1).astype(F32) * MOE_BLOCK
        running = jnp.zeros((1, LANES), F32)
        block_e = jnp.zeros((1, nbp), F32)
        for e in range(ne):
            pstart_sc[e:e + 1, :] = running
            running = running + padded[e:e + 1, :]
            block_e = block_e + (running[:, 0:1] <= blk).astype(F32)
        run_sc[...] = jnp.zeros(run_sc.shape, F32)
        n_used = running[:, 0:1] / MOE_BLOCK
        meta_ref[...] = jnp.concatenate(
            [jnp.minimum(block_e, ne - 1.0), jnp.broadcast_to(n_used, (1, nbp))], axis=0).astype(I32)

    upper = (lax.broadcasted_iota(I32, (sub, sub), 0) < lax.broadcasted_iota(I32, (sub, sub), 1)).astype(BF16)
    eid = lax.broadcasted_iota(I32, (ne, sub), 0)
    for j in range(n_sub):
        e = e_ref[:, j * sub:(j + 1) * sub]
        oh0 = (eid == e[0:1, :]).astype(F32)
        oh1 = (eid == e[1:2, :]).astype(F32)
        both = jnp.concatenate([oh0, oh1], axis=0).astype(BF16)
        rank = jnp.dot(both, upper, preferred_element_type=F32)
        c0 = jnp.sum(oh0, axis=1, keepdims=True)
        c1 = jnp.sum(oh1, axis=1, keepdims=True)
        base0 = pstart_sc[:, 0:1] + run_sc[:, 0:1]
        base1 = base0 + c0
        d0 = jnp.sum(oh0 * (rank[0:ne] + base0), axis=0, keepdims=True)
        d1 = jnp.sum(oh1 * (rank[ne:2 * ne] + base1), axis=0, keepdims=True)
        dest_ref[:, j * sub:(j + 1) * sub] = jnp.concatenate([d0, d1], axis=0).astype(I32)
        run_sc[...] += c0 + c1


def _moe_rank(e, cnt, n_blocks):
    t_all = e.shape[1]
    sub = 256
    tr = min(2048, t_all)
    nbp = -(-n_blocks // LANES) * LANES
    ne = cnt.shape[0]
    return pl.pallas_call(
        functools.partial(_moe_rank_kernel, sub=sub, n_sub=tr // sub, nbp=nbp),
        grid=(t_all // tr,),
        in_specs=[
            pl.BlockSpec((2, tr), lambda i: (0, i)),
            pl.BlockSpec((ne, LANES), lambda i: (0, 0)),
        ],
        out_specs=[
            pl.BlockSpec((2, tr), lambda i: (0, i)),
            pl.BlockSpec((2, nbp), lambda i: (0, 0)),
        ],
        out_shape=[
            jax.ShapeDtypeStruct((2, t_all), I32),
            jax.ShapeDtypeStruct((2, nbp), I32),
        ],
        scratch_shapes=[pltpu.VMEM((ne, LANES), F32), pltpu.VMEM((ne, LANES), F32)],
        compiler_params=_params("arbitrary"),
        name="moe_rank",
    )(e, cnt)


def _token_copy(src_ref, s, dst_ref, d, sem, pieces):
    return pltpu.make_async_copy(src_ref.at[pl.ds(pl.multiple_of(s * pieces, pieces), pieces)],
                                 dst_ref.at[pl.ds(pl.multiple_of(d * pieces, pieces), pieces)], sem)


def _moe_dispatch_kernel(dest_hbm, h_ref, xs_in, xs_out, idx_smem, idx_sem, sem, *, tm, pieces):
    del xs_in
    i = pl.program_id(0)
    cp = pltpu.make_async_copy(dest_hbm.at[i], idx_smem, idx_sem)
    cp.start()
    cp.wait()

    def issue(t, carry):
        _token_copy(h_ref, t, xs_out, idx_smem[0, t], sem, pieces).start()
        _token_copy(h_ref, t, xs_out, idx_smem[1, t], sem, pieces).start()
        return carry

    lax.fori_loop(0, tm, issue, 0, unroll=8)

    def drain(t, carry):
        _token_copy(h_ref, 0, xs_out, 0, sem, pieces).wait()
        return carry

    lax.fori_loop(0, 2 * tm, drain, 0, unroll=8)


def _moe_dispatch(dest_tiles, h, rows, tm, pieces):
    t_all = h.shape[0] // pieces
    zeros = jnp.zeros((rows * pieces, LANES), F32)
    return pl.pallas_call(
        functools.partial(_moe_dispatch_kernel, tm=tm, pieces=pieces),
        grid=(t_all // tm,),
        in_specs=[
            pl.BlockSpec(memory_space=pl.ANY),
            pl.BlockSpec((tm * pieces, LANES), lambda i: (i, 0)),
            pl.BlockSpec(memory_space=pl.ANY),
        ],
        out_specs=pl.BlockSpec(memory_space=pl.ANY),
        out_shape=jax.ShapeDtypeStruct((rows * pieces, LANES), F32),
        scratch_shapes=[pltpu.SMEM((2, tm), I32), pltpu.SemaphoreType.DMA(()), pltpu.SemaphoreType.DMA(())],
        input_output_aliases={2: 0},
        compiler_params=_params("arbitrary"),
        name="moe_dispatch",
    )(dest_tiles, h, zeros)


def _moe_expert_kernel(meta_ref, x_ref, wg_ref, wu_ref, wd_ref, y_ref, xb_sc, *, pieces):
    i = pl.program_id(0)
    rows = xb_sc.shape[0]

    @pl.when(i < meta_ref[1, 0])
    def _():
        for s in range(pieces):
            xb_sc[:, s * LANES:(s + 1) * LANES] = x_ref[pl.ds(s, rows, stride=pieces), :].astype(BF16)
        x = xb_sc[...]
        gate = jnp.dot(x, wg_ref[0], preferred_element_type=F32)
        up = jnp.dot(x, wu_ref[0], preferred_element_type=F32)
        hidden = (_silu(gate) * up).astype(BF16)
        y = jnp.dot(hidden, wd_ref[0], preferred_element_type=F32)
        for s in range(pieces):
            y_ref[pl.ds(s, rows, stride=pieces), :] = y[:, s * LANES:(s + 1) * LANES]

    @pl.when(i >= meta_ref[1, 0])
    def _():
        y_ref[...] = jnp.zeros(y_ref.shape, F32)


def _moe_experts(meta, xs, w_gate, w_up, w_down, n_blocks, pieces):
    d, f = w_gate.shape[1], w_gate.shape[2]
    wspec = lambda a, c: pl.BlockSpec((1, a, c), lambda i, meta: (meta[0, i], 0, 0))
    tile = pl.BlockSpec((MOE_BLOCK * pieces, LANES), lambda i, meta: (i, 0))
    return pl.pallas_call(
        functools.partial(_moe_expert_kernel, pieces=pieces),
        grid_spec=pltpu.PrefetchScalarGridSpec(
            num_scalar_prefetch=1,
            grid=(n_blocks,),
            in_specs=[tile, wspec(d, f), wspec(d, f), wspec(f, d)],
            out_specs=tile,
            scratch_shapes=[pltpu.VMEM((MOE_BLOCK, d), BF16)],
        ),
        out_shape=jax.ShapeDtypeStruct(xs.shape, F32),
        compiler_params=_params("arbitrary"),
        name="moe_experts",
    )(meta, xs, w_gate, w_up, w_down)


def _moe_combine_kernel(dest_hbm, y_hbm, w_ref, x_ref, g_ref, o_ref, y0_sc, y1_sc, idx_smem, idx_sem, sem,
                        *, tm, nt, pieces):
    tile = pl.program_id(0) * nt + pl.program_id(1)
    cp = pltpu.make_async_copy(dest_hbm.at[tile], idx_smem, idx_sem)
    cp.start()
    cp.wait()

    def issue(t, carry):
        _token_copy(y_hbm, idx_smem[0, t], y0_sc, t, sem, pieces).start()
        _token_copy(y_hbm, idx_smem[1, t], y1_sc, t, sem, pieces).start()
        return carry

    lax.fori_loop(0, tm, issue, 0, unroll=8)

    def drain(t, carry):
        _token_copy(y_hbm, 0, y0_sc, 0, sem, pieces).wait()
        return carry

    lax.fori_loop(0, 2 * tm, drain, 0, unroll=8)
    w = w_ref[...]
    w0, w1 = w[:, 0:1], w[:, 1:2]
    for s in range(pieces):
        cols = slice(s * LANES, (s + 1) * LANES)
        moe = w0 * y0_sc[pl.ds(s, tm, stride=pieces), :] + w1 * y1_sc[pl.ds(s, tm, stride=pieces), :]
        o_ref[0, :, cols] = x_ref[0, :, cols] + g_ref[0, :, cols] * moe


def _moe_combine(dest_tiles, y, w_col, x, gate, tm, pieces):
    b, s, d = x.shape
    nt = s // tm
    return pl.pallas_call(
        functools.partial(_moe_combine_kernel, tm=tm, nt=nt, pieces=pieces),
        grid=(b, nt),
        in_specs=[
            pl.BlockSpec(memory_space=pl.ANY),
            pl.BlockSpec(memory_space=pl.ANY),
            pl.BlockSpec((tm, 2), lambda bi, i: (bi * nt + i, 0)),
            pl.BlockSpec((1, tm, d), lambda bi, i: (bi, i, 0)),
            pl.BlockSpec((1, 1, d), lambda bi, i: (bi, 0, 0)),
        ],
        out_specs=pl.BlockSpec((1, tm, d), lambda bi, i: (bi, i, 0)),
        out_shape=jax.ShapeDtypeStruct((b, s, d), F32),
        scratch_shapes=[
            pltpu.VMEM((tm * pieces, LANES), F32),
            pltpu.VMEM((tm * pieces, LANES), F32),
            pltpu.SMEM((2, tm), I32),
            pltpu.SemaphoreType.DMA(()),
            pltpu.SemaphoreType.DMA(()),
        ],
        compiler_params=_params("arbitrary", "arbitrary"),
        name="moe_combine",
    )(dest_tiles, y, w_col, x, gate)


def _moe_layer(x, sh, sc, gate, g_norm, router_w, router_bias, w_gate, w_up, w_down):
    b, s, d = x.shape
    t_all = b * s
    pieces = d // LANES
    tm = min(512, s)
    n_blocks = -(-t_all * 2 // MOE_BLOCK) + N_EXPERTS
    rows = n_blocks * MOE_BLOCK
    h, e, w, cnt = _moe_pre(x, sh, sc, g_norm, router_w, router_bias)
    dest, meta = _moe_rank(e, cnt, n_blocks)
    dest_tiles = dest.reshape(2, t_all // tm, tm).transpose(1, 0, 2)
    xs = _moe_dispatch(dest_tiles, h, rows, tm, pieces)
    y = _moe_experts(meta, xs, w_gate.astype(BF16), w_up.astype(BF16), w_down.astype(BF16), n_blocks, pieces)
    return _moe_combine(dest_tiles, y, w.T, x, gate, tm, pieces)


def _final_norm_kernel(x_ref, g_ref, o_ref):
    x = x_ref[0]
    ms = jnp.mean(x * x, axis=-1, keepdims=True)
    o_ref[0] = x * lax.rsqrt(ms + NORM_EPS) * g_ref[...]


def _final_norm(x, g, tm=1024):
    b, s, d = x.shape
    tm = min(tm, s)
    return pl.pallas_call(
        _final_norm_kernel,
        grid=(b, s // tm),
        in_specs=[
            pl.BlockSpec((1, tm, d), lambda bi, i: (bi, i, 0)),
            pl.BlockSpec((1, d), lambda bi, i: (0, 0)),
        ],
        out_specs=pl.BlockSpec((1, tm, d), lambda bi, i: (bi, i, 0)),
        out_shape=jax.ShapeDtypeStruct((b, s, d), F32),
        compiler_params=_params("parallel", "arbitrary"),
        name="final_norm",
    )(x, g.reshape(1, d))


def kernel(x, c, ada_w, ada_b, norm1_g, norm2_g, router_w, router_bias, moe_w_gate, moe_w_up, moe_w_down, dil_w_in, dil_w_out, diff_w_in, diff_lam_q1, diff_lam_k1, diff_lam_q2, diff_lam_k2, diff_head_norm_g, diff_w_out, ssm_w_in, ssm_conv_w, ssm_conv_b, ssm_dt_bias, ssm_A_log, ssm_D, ssm_norm_g, ssm_w_out, final_norm_g):
    b, s, d = x.shape
    depth = ada_w.shape[0]
    mod = _ada_mod(c, ada_w, ada_b).reshape(depth, b, 6, 1, d)
    tables = _rope_tables(s)
    for i in range(depth):
        sh1, sc1, g1, sh2, sc2, g2 = (mod[i, :, j] for j in range(6))
        n1 = norm1_g[i].reshape(1, d)
        kind, j = i % 3, i // 3
        if kind == 0:
            x = _dilated_layer(x, sh1, sc1, g1, n1, dil_w_in[j], dil_w_out[j], tables)
        elif kind == 1:
            x = _diff_layer(x, sh1, sc1, g1, n1, diff_w_in[j], diff_lam_q1[j], diff_lam_k1[j],
                            diff_lam_q2[j], diff_lam_k2[j], diff_head_norm_g[j], diff_w_out[j], tables, i)
        else:
            x = _ssd_layer(x, sh1, sc1, g1, n1, ssm_w_in[j], ssm_conv_w[j], ssm_conv_b[j], ssm_dt_bias[j],
                           ssm_A_log[j], ssm_D[j], ssm_norm_g[j], ssm_w_out[j])
        x = _moe_layer(x, sh2, sc2, g2, norm2_g[i].reshape(1, d), router_w, router_bias,
                       moe_w_gate[i], moe_w_up[i], moe_w_down[i])
    return _final_norm(x, final_norm_g)
```

```python
import functools
import math

import jax
import jax.numpy as jnp
from jax import lax
from jax.experimental import pallas as pl
from jax.experimental.pallas import tpu as pltpu

F32 = jnp.float32
BF16 = jnp.bfloat16
I32 = jnp.int32
HIGHEST = lax.Precision.HIGHEST

LANES = 128
SUBLANES = 8
VMEM_LIMIT_BYTES = 56 * 1024 * 1024

NORM_EPS = 1e-6
ROPE_THETA = 500000.0
ROPE_FRACTION = 4
HEAD_DIM = 64
ATTN_BLOCK = 128
DIL_CONFIGS = ((128, 1), (512, 4), (2048, 16))
DIFF_NORM_EPS = 1e-5
SSM_HEAD_DIM = 64
SSM_GROUPS = 4
SSM_STATE = 128
SSM_CONV = 4
SSM_CHUNK = 256
N_EXPERTS = 16
N_EXPERT_GROUPS = 4
EXPERTS_PER_GROUP = 4
PAIRS_PER_GROUP = 6
N_CLASSES = N_EXPERT_GROUPS * PAIRS_PER_GROUP
MOE_BLOCK = 256
COL_CHUNK = 512
NEG = -1e30

_NT = (((1,), (1,)), ((), ()))
_TN = (((0,), (0,)), ((), ()))


def _params(*sem):
    return pltpu.CompilerParams(dimension_semantics=sem, vmem_limit_bytes=VMEM_LIMIT_BYTES)


def _silu(v):
    return v * jax.nn.sigmoid(v)


def _ada_kernel(c_ref, w_ref, b_ref, o_ref):
    cond = _silu(c_ref[...])
    o_ref[0] = jnp.dot(cond, w_ref[0], preferred_element_type=F32, precision=HIGHEST) + b_ref[0]


def _ada_mod(c, ada_w, ada_b):
    depth, d, n = ada_w.shape
    b = c.shape[0]
    tn = 1536
    return pl.pallas_call(
        _ada_kernel,
        grid=(depth, n // tn),
        in_specs=[
            pl.BlockSpec((b, d), lambda i, j: (0, 0)),
            pl.BlockSpec((1, d, tn), lambda i, j: (i, 0, j)),
            pl.BlockSpec((1, 1, tn), lambda i, j: (i, 0, j)),
        ],
        out_specs=pl.BlockSpec((1, b, tn), lambda i, j: (i, 0, j)),
        out_shape=jax.ShapeDtypeStruct((depth, b, n), F32),
        compiler_params=_params("arbitrary", "arbitrary"),
        name="ada_mod",
    )(c, ada_w, ada_b.reshape(depth, 1, n))


def _rope_tables(seq):
    r = HEAD_DIM // ROPE_FRACTION
    half = r // 2
    inv = jnp.power(ROPE_THETA, -jnp.arange(half, dtype=F32) * 2.0 / r)
    ang = jnp.arange(seq, dtype=F32)[:, None] * inv[None, :]
    cos, sin = jnp.cos(ang), jnp.sin(ang)
    ones = jnp.ones((seq, HEAD_DIM - r), F32)
    zeros = jnp.zeros((seq, HEAD_DIM - r), F32)
    zh = jnp.zeros((seq, half), F32)
    cos_t = jnp.concatenate([cos, cos, ones], axis=1)
    sin_a = jnp.concatenate([zh, sin, zeros], axis=1)
    sin_b = jnp.concatenate([-sin, zh, zeros], axis=1)
    rep = LANES // HEAD_DIM
    lane_form = tuple(jnp.tile(t, (1, rep)) for t in (cos_t, sin_a, sin_b))
    return lane_form, tuple(t.T for t in lane_form)


def _modulated_norm(x, g, sc, sh):
    ms = jnp.mean(x * x, axis=-1, keepdims=True)
    return x * lax.rsqrt(ms + NORM_EPS) * g * (1.0 + sc) + sh


def _rope(a, cos, sa, sb, scale, axis):
    half = HEAD_DIM // ROPE_FRACTION // 2
    r = a * cos + pltpu.roll(a, half, axis) * sa + pltpu.roll(a, LANES - half, axis) * sb
    return r if scale == 1.0 else r * scale


def _proj_kernel(*refs, chunks, tplan, n_out, use_rope, has_perm):
    x_ref, sh_ref, sc_ref, g_ref, w_ref = refs[:5]
    pos = 5
    if use_rope:
        cos_ref, sa_ref, sb_ref = refs[pos:pos + 3]
        pos += 3
    if tplan is not None:
        wt_ref, cost_ref, sat_ref, sbt_ref = refs[pos:pos + 4]
        pos += 4
    out_refs = refs[pos:pos + n_out]
    perm_sc = refs[pos + n_out] if has_perm else None
    tm = x_ref.shape[1]
    h = _modulated_norm(x_ref[0], g_ref[...], sc_ref[0], sh_ref[0]).astype(BF16)
    tiles_per_chunk = COL_CHUNK // LANES
    for c, (oi, off, scale, dil) in enumerate(chunks):
        o_ref = out_refs[oi]
        acc = jnp.dot(h, w_ref[:, c * COL_CHUNK:(c + 1) * COL_CHUNK], preferred_element_type=F32)
        tiles = None
        if scale is not None:
            cos, sa, sb = cos_ref[...], sa_ref[...], sb_ref[...]
            tiles = [_rope(acc[:, s * LANES:(s + 1) * LANES], cos, sa, sb, scale, 1)
                     for s in range(tiles_per_chunk)]
        if dil == 1:
            if tiles is None:
                o_ref[0, :, off:off + COL_CHUNK] = acc.astype(o_ref.dtype)
            else:
                for s, tl in enumerate(tiles):
                    o_ref[0, :, off + s * LANES:off + (s + 1) * LANES] = tl.astype(o_ref.dtype)
            continue
        if tiles is None:
            tiles = [acc[:, s * LANES:(s + 1) * LANES] for s in range(tiles_per_chunk)]
        for s, tl in enumerate(tiles):
            perm_sc[s] = tl
        for rho in range(dil):
            for s in range(tiles_per_chunk):
                o_ref[0, rho, :, off + s * LANES:off + (s + 1) * LANES] = (
                    perm_sc[s, pl.ds(rho, tm // dil, stride=dil), :].astype(o_ref.dtype))
    if tplan is not None:
        oi, scale = tplan
        o_ref = out_refs[oi]
        acct = lax.dot_general(wt_ref[...], h, _NT, preferred_element_type=F32)
        cost, sat, sbt = cost_ref[...], sat_ref[...], sbt_ref[...]
        for s in range(acct.shape[0] // LANES):
            rows = slice(s * LANES, (s + 1) * LANES)
            o_ref[0, rows, :] = _rope(acct[rows, :], cost, sat, sbt, scale, 0).astype(o_ref.dtype)


def _proj(x, sh, sc, g, w, chunks, outs, tables, wt=None, tplan=None, tm=512):
    b, s, d = x.shape
    n = w.shape[1]
    tm = min(tm, s)
    use_rope = any(c[2] is not None for c in chunks)
    has_perm = any(c[3] > 1 for c in chunks)
    in_specs = [
        pl.BlockSpec((1, tm, d), lambda bi, i: (bi, i, 0)),
        pl.BlockSpec((1, 1, d), lambda bi, i: (bi, 0, 0)),
        pl.BlockSpec((1, 1, d), lambda bi, i: (bi, 0, 0)),
        pl.BlockSpec((1, d), lambda bi, i: (0, 0)),
        pl.BlockSpec((d, n), lambda bi, i: (0, 0), pipeline_mode=pl.Buffered(1)),
    ]
    args = [x, sh, sc, g, w]
    if use_rope:
        in_specs += [pl.BlockSpec((tm, LANES), lambda bi, i: (i, 0))] * 3
        args += list(tables[0])
    if tplan is not None:
        in_specs.append(pl.BlockSpec(wt.shape, lambda bi, i: (0, 0), pipeline_mode=pl.Buffered(1)))
        in_specs += [pl.BlockSpec((LANES, tm), lambda bi, i: (0, i))] * 3
        args += [wt] + list(tables[1])
    out_specs, out_shape = [], []
    for layout, wd, dt, dil in outs:
        if layout == "tok":
            out_specs.append(pl.BlockSpec((1, tm, wd), lambda bi, i: (bi, i, 0)))
            out_shape.append(jax.ShapeDtypeStruct((b, s, wd), dt))
        elif layout == "res":
            out_specs.append(pl.BlockSpec((1, dil, tm // dil, wd), lambda bi, i: (bi, 0, i, 0)))
            out_shape.append(jax.ShapeDtypeStruct((b, dil, s // dil, wd), dt))
        else:
            out_specs.append(pl.BlockSpec((1, wd, tm), lambda bi, i: (bi, 0, i)))
            out_shape.append(jax.ShapeDtypeStruct((b, wd, s), dt))
    return pl.pallas_call(
        functools.partial(_proj_kernel, chunks=tuple(chunks), tplan=tplan, n_out=len(outs),
                          use_rope=use_rope, has_perm=has_perm),
        grid=(b, s // tm),
        in_specs=in_specs,
        out_specs=out_specs,
        out_shape=out_shape,
        scratch_shapes=[pltpu.VMEM((COL_CHUNK // LANES, tm, LANES), F32)] if has_perm else [],
        compiler_params=_params("parallel", "arbitrary"),
        name="norm_proj",
    )(*args)


def _outproj_kernel(y_ref, w_ref, x_ref, g_ref, o_ref):
    y = jnp.dot(y_ref[0], w_ref[...], preferred_element_type=F32)
    o_ref[0] = x_ref[0] + g_ref[0] * y


def _outproj(y, w, x, gate, tm=512):
    b, s, d = x.shape
    k = y.shape[-1]
    tm = min(tm, s)
    return pl.pallas_call(
        _outproj_kernel,
        grid=(b, s // tm),
        in_specs=[
            pl.BlockSpec((1, tm, k), lambda bi, i: (bi, i, 0)),
            pl.BlockSpec((k, d), lambda bi, i: (0, 0), pipeline_mode=pl.Buffered(1)),
            pl.BlockSpec((1, tm, d), lambda bi, i: (bi, i, 0)),
            pl.BlockSpec((1, 1, d), lambda bi, i: (bi, 0, 0)),
        ],
        out_specs=pl.BlockSpec((1, tm, d), lambda bi, i: (bi, i, 0)),
        out_shape=jax.ShapeDtypeStruct((b, s, d), F32),
        compiler_params=_params("parallel", "arbitrary"),
        name="out_proj",
    )(y, w, x, gate)


def _dil_kernel(q_ref, kc_ref, kp_ref, vc_ref, vp_ref, o_ref, lse_ref, kbuf, vbuf, *, tq, back, heads):
    n = pl.program_id(2)
    qb = ATTN_BLOCK
    kbuf[0:qb] = kp_ref[0, 0]
    kbuf[qb:] = kc_ref[0, 0]
    vbuf[0:qb] = vp_ref[0, 0]
    vbuf[qb:] = vc_ref[0, 0]
    rows = heads * qb
    qi = lax.broadcasted_iota(I32, (rows, 2 * qb), 0) & (qb - 1)
    kj = lax.broadcasted_iota(I32, (rows, 2 * qb), 1)
    rel = kj - qi
    band = (rel >= qb - back) & (rel <= qb)
    lane = lax.broadcasted_iota(I32, (qb, LANES), 1)
    for j in range(tq // qb):
        first_key = n * tq + (j - 1) * qb
        valid = band & (kj + first_key >= 0)
        q = q_ref[0, 0, j * qb:(j + 1) * qb, :]
        kk = kbuf[j * qb:(j + 2) * qb, :]
        vv = vbuf[j * qb:(j + 2) * qb, :]
        s = jnp.concatenate(
            [lax.dot_general(q[:, h * HEAD_DIM:(h + 1) * HEAD_DIM], kk[:, h * HEAD_DIM:(h + 1) * HEAD_DIM],
                             _NT, preferred_element_type=F32) for h in range(heads)], axis=0)
        s = jnp.where(valid, s, NEG)
        m = jnp.max(s, axis=-1, keepdims=True)
        p = jnp.exp(s - m)
        l = jnp.sum(p, axis=-1, keepdims=True)
        inv = 1.0 / l
        lse = m + jnp.log(l)
        pb = p.astype(BF16)
        lse_tile = jnp.zeros((qb, LANES), F32)
        for h in range(heads):
            hs = slice(h * HEAD_DIM, (h + 1) * HEAD_DIM)
            hr = slice(h * qb, (h + 1) * qb)
            o = jnp.dot(pb[hr], vv[:, hs], preferred_element_type=F32) * inv[hr]
            o_ref[0, 0, j * qb:(j + 1) * qb, hs] = o.astype(o_ref.dtype)
            lse_tile = jnp.where(lane == h, lse[hr], lse_tile)
        lse_ref[0, 0, j * qb:(j + 1) * qb, :] = lse_tile


def _dil_group(proj, window, dilation):
    b, dil, ln, c = proj.shape
    back = window // dilation
    assert dil == dilation and back <= ATTN_BLOCK and ln % ATTN_BLOCK == 0
    width = c // 3
    heads = width // HEAD_DIM
    tq = min(512, ln)
    sub = tq // ATTN_BLOCK

    def cur(col):
        return pl.BlockSpec((1, 1, tq, width), lambda bi, r, n: (bi, r, n, col))

    def prev(col):
        return pl.BlockSpec((1, 1, ATTN_BLOCK, width),
                            lambda bi, r, n: (bi, r, jnp.maximum(n * sub - 1, 0), col))

    return pl.pallas_call(
        functools.partial(_dil_kernel, tq=tq, back=back, heads=heads),
        grid=(b, dil, ln // tq),
        in_specs=[cur(0), cur(1), prev(1), cur(2), prev(2)],
        out_specs=[
            pl.BlockSpec((1, 1, tq, width), lambda bi, r, n: (bi, r, n, 0)),
            pl.BlockSpec((1, 1, tq, LANES), lambda bi, r, n: (bi, r, n, 0)),
        ],
        out_shape=[
            jax.ShapeDtypeStruct((b, dil, ln, width), BF16),
            jax.ShapeDtypeStruct((b, dil, ln, LANES), F32),
        ],
        scratch_shapes=[pltpu.VMEM((tq + ATTN_BLOCK, width), BF16)] * 2,
        compiler_params=_params("parallel", "parallel", "arbitrary"),
        name=f"dil_attn_d{dilation}",
    )(proj, proj, proj, proj, proj)


def _dil_out_kernel(o0_ref, o1_ref, o2_ref, l0_ref, l1_ref, l2_ref, w_ref, x_ref, g_ref, out_ref,
                    o_sc, l1_sc, l2_sc):
    tm = x_ref.shape[1]

    def to_token_order(src_ref, dst_sc):
        dil = src_ref.shape[1]
        for rho in range(dil):
            src = src_ref[0, rho].astype(F32)
            for s in range(dst_sc.shape[0]):
                dst_sc[s, pl.ds(rho, tm // dil, stride=dil), :] = src[:, s * LANES:(s + 1) * LANES]
        return jnp.concatenate([dst_sc[s] for s in range(dst_sc.shape[0])], axis=1)

    ls = [l0_ref[0, 0], to_token_order(l1_ref, l1_sc), to_token_order(l2_ref, l2_sc)]
    mx =jnp.maximum(jnp.maximum(ls[0], ls[1]), ls[2])
    es = [jnp.exp(v - mx) for v in ls]
    inv = 1.0 / (es[0] + es[1] + es[2])
    width = o0_ref.shape[-1]
    expand = (lax.broadcasted_iota(I32, (LANES, width), 0)
              == lax.broadcasted_iota(I32, (LANES, width), 1) // HEAD_DIM).astype(BF16)
    o = jnp.zeros((tm, width), F32)
    for gi, (e, o_ref) in enumerate(zip(es, (o0_ref, o1_ref, o2_ref))):
        alpha = e * inv
        hi = alpha.astype(BF16)
        lo = (alpha - hi.astype(F32)).astype(BF16)
        a_full = (jnp.dot(hi, expand, preferred_element_type=F32)
                  + jnp.dot(lo, expand, preferred_element_type=F32))
        og = o_ref[0, 0].astype(F32) if gi == 0 else to_token_order(o_ref, o_sc)
        o = o + a_full * og
    y = jnp.dot(o.astype(BF16), w_ref[...], preferred_element_type=F32)
    out_ref[0] = x_ref[0] + g_ref[0] * y


def _dil_out(os_, lses, w, x, gate, tm=512):
    b, s, d = x.shape
    width = os_[0].shape[-1]
    tm = min(tm, s)
    tok = lambda wd: pl.BlockSpec((1, tm, wd), lambda bi, i: (bi, i, 0))
    res = lambda a: pl.BlockSpec((1, a.shape[1], tm // a.shape[1], a.shape[3]), lambda bi, i: (bi, 0, i, 0))
    return pl.pallas_call(
        _dil_out_kernel,
        grid=(b, s // tm),
        in_specs=[res(a) for a in os_] + [res(a) for a in lses] + [
            pl.BlockSpec((width, d), lambda bi, i: (0, 0)),
            tok(d),
            pl.BlockSpec((1, 1, d), lambda bi, i: (bi, 0, 0)),
        ],
        out_specs=tok(d),
        out_shape=jax.ShapeDtypeStruct((b, s, d), F32),
        scratch_shapes=[pltpu.VMEM((width // LANES, tm, LANES), F32), pltpu.VMEM((1, tm, LANES), F32),
                        pltpu.VMEM((1, tm, LANES), F32)],
        compiler_params=_params("parallel", "arbitrary"),
        name="dil_out",
    )(*os_, *lses, w, x, gate)


def _dilated_layer(x, sh, sc, gate, g_norm, w_in, w_out, tables):
    b, s, _ = x.shape
    n = w_in.shape[1]
    gw = n // len(DIL_CONFIGS)
    chunks, outs = [], []
    for g, (_, dilation) in enumerate(DIL_CONFIGS):
        for kind in range(3):
            scale = (HEAD_DIM ** -0.5, 1.0, None)[kind]
            chunks.append((g, kind * COL_CHUNK, scale, dilation))
        outs.append(("tok" if dilation == 1 else "res", gw, BF16, dilation))
    projs = _proj(x, sh, sc, g_norm, w_in.astype(BF16), chunks, outs, tables)
    os_, lses = [], []
    for proj, (window, dilation) in zip(projs, DIL_CONFIGS):
        o, lse = _dil_group(proj.reshape(b, dilation, s // dilation, gw), window, dilation)
        os_.append(o)
        lses.append(lse)
    return _dil_out(os_, lses, w_out.astype(BF16), x, gate)


def _diff_kernel(qi_tab, ki_tab, q_ref, kt_ref, v_ref, lq1_ref, lk1_ref, lq2_ref, lk2_ref, hg_ref, o_ref,
                 m_sc, acc_sc, *, t, r, lam_init):
    pair = pl.program_id(2)
    qi = qi_tab[pair]
    ki = ki_tab[pair]
    vw = v_ref.shape[-1]

    @pl.when(ki == 0)
    def _():
        m_sc[...] = jnp.full(m_sc.shape, NEG, F32)
        acc_sc[...] = jnp.zeros(acc_sc.shape, F32)

    def step(diagonal, r):
        kt = kt_ref[0]
        v_aug = jnp.concatenate([v_ref[0], jnp.ones((t, LANES), BF16)], axis=1)
        lane = lax.broadcasted_iota(I32, (r, vw), 1)
        work = [(mi, c) for mi in range(2) for c in range(t // r)]

        def scores(mi, c):
            in_half = (lane >= mi * HEAD_DIM) & (lane < (mi + 1) * HEAD_DIM)
            ncols = (c + 1) * r if diagonal else t
            q_c = jnp.where(in_half, q_ref[0, c * r:(c + 1) * r, :], jnp.zeros((r, vw), BF16))
            return jnp.dot(q_c, kt[:, :ncols], preferred_element_type=F32)

        ahead = 3
        pending = [scores(*wk) for wk in work[:ahead]]
        for idx, (mi, c) in enumerate(work):
            s = pending.pop(0)
            if idx + ahead < len(work):
                pending.append(scores(*work[idx + ahead]))
            ncols = s.shape[1]
            if diagonal:
                col = lax.broadcasted_iota(I32, (r, ncols), 1)
                row = lax.broadcasted_iota(I32, (r, ncols), 0) + c * r
                s = jnp.where(col <= row, s, NEG)
            tiles = [s[:, j * LANES:(j + 1) * LANES] for j in range(ncols // LANES)]
            tmax = functools.reduce(jnp.maximum, tiles)
            srows = slice(mi * t + c * r, mi * t + (c + 1) * r)
            m_prev = m_sc[srows, :]
            m_new = jnp.maximum(m_prev, jnp.max(tmax, axis=-1, keepdims=True))
            alpha = jnp.exp2(m_prev - m_new)
            p = jnp.concatenate([jnp.exp2(tl - m_new) for tl in tiles], axis=1).astype(BF16)
            pv = jnp.dot(p, v_aug[:ncols], preferred_element_type=F32)
            acc_sc[srows, :] = jnp.concatenate([alpha, alpha], axis=1) * acc_sc[srows, :] + pv
            m_sc[srows, :] = m_new

    @pl.when(ki < qi)
    def _():
        step(False, 2 * r)

    @pl.when(ki == qi)
    def _():
        step(True, 2 * r)
        lam = (jnp.exp(jnp.sum(lq1_ref[...] * lk1_ref[...], axis=-1, keepdims=True))
               - jnp.exp(jnp.sum(lq2_ref[...] * lk2_ref[...], axis=-1, keepdims=True)) + lam_init)
        o = acc_sc[:t, :vw] / acc_sc[:t, vw:] - lam * (acc_sc[t:, :vw] / acc_sc[t:, vw:])
        ms = jnp.mean(o * o, axis=-1, keepdims=True)
        o = o * lax.rsqrt(ms + DIFF_NORM_EPS) * hg_ref[...] * (1.0 - lam_init)
        o_ref[0] = o.astype(o_ref.dtype)


def _diff_attention(q, kt, v, lam_q1, lam_k1, lam_q2, lam_k2, head_g, lam_init, t=1024, r=128):
    b, s, d = q.shape
    vw = 2 * HEAD_DIM
    assert vw == LANES
    heads = d // vw
    t = min(t, s)
    r = min(r, t // 2)
    nq = s // t
    pairs = [(qi, ki) for qi in range(nq) for ki in range(qi + 1)]
    qi_tab = jnp.asarray([p[0] for p in pairs], I32)
    ki_tab = jnp.asarray([p[1] for p in pairs], I32)
    vec = lambda n: pl.BlockSpec((1, n), lambda bi, h, p, qt, kt_: (0, 0))
    return pl.pallas_call(
        functools.partial(_diff_kernel, t=t, r=r, lam_init=lam_init),
        grid_spec=pltpu.PrefetchScalarGridSpec(
            num_scalar_prefetch=2,
            grid=(b, heads, len(pairs)),
            in_specs=[
                pl.BlockSpec((1, t, vw), lambda bi, h, p, qt, kt_: (bi, qt[p], h)),
                pl.BlockSpec((1, vw, t), lambda bi, h, p, qt, kt_: (bi, h, kt_[p])),
                pl.BlockSpec((1, t, vw), lambda bi, h, p, qt, kt_: (bi, kt_[p], h)),
                vec(HEAD_DIM), vec(HEAD_DIM), vec(HEAD_DIM), vec(HEAD_DIM), vec(vw),
            ],
            out_specs=pl.BlockSpec((1, t, vw), lambda bi, h, p, qt, kt_: (bi, qt[p], h)),
            scratch_shapes=[
                pltpu.VMEM((2 * t, LANES), F32),
                pltpu.VMEM((2 * t, vw + LANES), F32),
            ],
        ),
        out_shape=jax.ShapeDtypeStruct((b, s, d), BF16),
        compiler_params=_params("parallel", "parallel", "arbitrary"),
        name="diff_attn",
    )(qi_tab, ki_tab, q, kt, v, lam_q1.reshape(1, -1), lam_k1.reshape(1, -1), lam_q2.reshape(1, -1),
      lam_k2.reshape(1, -1), head_g.reshape(1, -1))


def _diff_layer(x, sh, sc, gate, g_norm, w_in, lam_q1, lam_k1, lam_q2, lam_k2, head_g, w_out,
                tables, layer_idx):
    d = w_in.shape[1] // 3
    nch = d // COL_CHUNK
    q_scale = HEAD_DIM ** -0.5 * math.log2(math.e)
    chunks = ([(0, c * COL_CHUNK, q_scale, 1) for c in range(nch)]
              + [(1, c * COL_CHUNK, None, 1) for c in range(nch)])
    w_qv = jnp.concatenate([w_in[:, :d], w_in[:, 2 * d:]], axis=1).astype(BF16)
    w_kt = w_in[:, d:2 * d].T.astype(BF16)
    q, v, kt = _proj(x, sh, sc, g_norm, w_qv, chunks,
                     [("tok", d, BF16, 1), ("tok", d, BF16, 1), ("T", d, BF16, 1)], tables,
                     wt=w_kt, tplan=(2, 1.0))
    lam_init = 0.8 - 0.6 * math.exp(-0.3 * layer_idx)
    o = _diff_attention(q, kt, v, lam_q1, lam_k1, lam_q2, lam_k2, head_g, lam_init)
    return _outproj(o, w_out.astype(BF16), x, gate)


def _causal_conv_silu(cur_ref, tail_ref, w_ref, b_ref):
    cur = cur_ref[0].astype(F32)
    rows = cur.shape[0]
    tail = tail_ref[...]
    w = w_ref[...]
    row8 = lax.broadcasted_iota(I32, (SUBLANES, cur.shape[1]), 0)
    acc = cur * w[SSM_CONV - 1:SSM_CONV] + b_ref[...]
    for k in range(1, SSM_CONV):
        sh = pltpu.roll(cur, k, 0)
        top = jnp.where(row8 < k, pltpu.roll(tail, k, 0), sh[0:SUBLANES])
        shifted = jnp.concatenate([top, sh[SUBLANES:]], axis=0)
        acc = acc + shifted * w[SSM_CONV - 1 - k:SSM_CONV - k]
    tail_ref[...] = cur[rows - SUBLANES:rows]
    return _silu(acc)


def _ssd_kernel(z_ref, x_ref, b_ref, c_ref, dt_ref, wx_ref, wb_ref, wc_ref, bx_ref, bb_ref, bc_ref,
                dtb_ref, alog_ref, dskip_ref, ng_ref, y_ref,
                state_sc, tx_sc, tb_sc, tc_sc, ybuf_sc, wdx_sc, *, hpg):
    ci = pl.program_id(2)
    L = x_ref.shape[1]
    P = SSM_HEAD_DIM

    @pl.when(ci == 0)
    def _():
        state_sc[...] = jnp.zeros(state_sc.shape, F32)
        tx_sc[...] = jnp.zeros(tx_sc.shape, F32)
        tb_sc[...] = jnp.zeros(tb_sc.shape, F32)
        tc_sc[...] = jnp.zeros(tc_sc.shape, F32)

    xs = _causal_conv_silu(x_ref, tx_sc, wx_ref, bx_ref)
    bm = _causal_conv_silu(b_ref, tb_sc, wb_ref, bb_ref).astype(BF16)
    cm = _causal_conv_silu(c_ref, tc_sc, wc_ref, bc_ref).astype(BF16)

    raw = dt_ref[0] + dtb_ref[...]
    dt = jnp.maximum(raw, 0.0) + jnp.log(1.0 + jnp.exp(-jnp.abs(raw)))
    a = -jnp.exp(alog_ref[...])
    da = dt * a
    ti = lax.broadcasted_iota(I32, (L, L), 0)
    si = lax.broadcasted_iota(I32, (L, L), 1)
    causal = ti >= si
    acum = jnp.dot(causal.astype(F32), da, preferred_element_type=F32, precision=HIGHEST)
    acum_t = acum.T
    last = acum[L - 1:L, :]
    eacum = jnp.exp(acum)
    w_end = jnp.exp(last - acum)
    elast = jnp.exp(last)

    cb = lax.dot_general(cm, bm, _NT, preferred_element_type=F32)
    y_inter = lax.dot_general(cm, state_sc[...].astype(BF16), _NT, preferred_element_type=F32)

    for k in range(hpg):
        hs = slice(k * P, (k + 1) * P)
        seg = acum[:, k:k + 1] - acum_t[k:k + 1, :]
        decay = jnp.exp(jnp.where(causal, seg, NEG))
        mk = (cb * decay).astype(BF16)
        dx = dt[:, k:k + 1] * xs[:, hs]
        y = jnp.dot(mk, dx.astype(BF16), preferred_element_type=F32)
        ybuf_sc[:, hs] = y + y_inter[:, hs] * eacum[:, k:k + 1]
        wdx_sc[:, hs] = (w_end[:, k:k + 1] * dx).astype(BF16)
        state_sc[hs, :] = state_sc[hs, :] * elast[:, k:k + 1]

    state_sc[...] += lax.dot_general(wdx_sc[...], bm, _TN, preferred_element_type=F32)

    y = ybuf_sc[...] + dskip_ref[...] * xs
    y = y * _silu(z_ref[0].astype(F32))
    ms = jnp.mean(y * y, axis=-1, keepdims=True)
    y_ref[0] = (y * lax.rsqrt(ms + NORM_EPS) * ng_ref[...]).astype(y_ref.dtype)


def _ssd_scan(main, dt, conv_w, conv_b, dt_bias, a_log, d_skip, norm_g):
    b, s, _ = main.shape
    G, N, P, L = SSM_GROUPS, SSM_STATE, SSM_HEAD_DIM, SSM_CHUNK
    di = norm_g.shape[-1]
    gw = di // G
    hpg = gw // P
    assert s % L == 0
    xoff = di // gw
    boff = 2 * di // N
    coff = boff + G
    seq = lambda wd, off: pl.BlockSpec((1, L, wd), lambda bi, g, c: (bi, c, off + g))
    par = lambda r, wd, off: pl.BlockSpec((r, wd), lambda bi, g, c: (0, off + g))
    return pl.pallas_call(
        functools.partial(_ssd_kernel, hpg=hpg),
        grid=(b, G, s // L),
        in_specs=[
            seq(gw, 0), seq(gw, xoff), seq(N, boff), seq(N, coff), seq(LANES, 0),
            par(SSM_CONV, gw, 0), par(SSM_CONV, N, di // N), par(SSM_CONV, N, di // N + G),
            par(1, gw, 0), par(1, N, di // N), par(1, N, di // N + G),
            par(1, LANES, 0), par(1, LANES, 0), par(1, gw, 0), par(1, gw, 0),
        ],
        out_specs=seq(gw, 0),
        out_shape=jax.ShapeDtypeStruct((b, s, di), BF16),
        scratch_shapes=[
            pltpu.VMEM((gw, N), F32),
            pltpu.VMEM((SUBLANES, gw), F32),
            pltpu.VMEM((SUBLANES, N), F32),
            pltpu.VMEM((SUBLANES, N), F32),
            pltpu.VMEM((L, gw), F32),
            pltpu.VMEM((L, gw), BF16),
        ],
        compiler_params=_params("parallel", "parallel", "arbitrary"),
        name="ssd_scan",
    )(main, main, main, main, dt, conv_w, conv_w, conv_w, conv_b, conv_b, conv_b,
      dt_bias, a_log, d_skip, norm_g)


def _pad_heads(v, groups):
    hpg = v.shape[0] // groups
    return jnp.pad(v.reshape(groups, hpg), ((0, 0), (0, LANES - hpg))).reshape(1, groups * LANES)


def _ssd_layer(x, sh, sc, gate, g_norm, w_in, conv_w, conv_b, dt_bias, a_log, d_skip, norm_g, w_out):
    G = SSM_GROUPS
    di = norm_g.shape[0]
    heads = dt_bias.shape[0]
    hpg = heads // G
    n_main = 2 * di + 2 * G * SSM_STATE
    d = w_in.shape[0]
    w_dt = w_in[:, n_main:].reshape(d, G, hpg)
    w_dt = jnp.pad(w_dt, ((0, 0), (0, 0), (0, LANES - hpg))).reshape(d, G * LANES)
    w = jnp.concatenate([w_in[:, :n_main], w_dt], axis=1).astype(BF16)
    chunks = [(0, c * COL_CHUNK, None, 1) for c in range(n_main // COL_CHUNK)] + [(1, 0, None, 1)]
    main, dt = _proj(x, sh, sc, g_norm, w, chunks, [("tok", n_main, BF16, 1), ("tok", G * LANES, F32, 1)], None)
    y = _ssd_scan(main, dt, conv_w, conv_b.reshape(1, -1), _pad_heads(dt_bias, G), _pad_heads(a_log, G),
                  jnp.repeat(d_skip, SSM_HEAD_DIM).reshape(1, -1), norm_g.reshape(1, -1))
    return _outproj(y, w_out.astype(BF16), x, gate)


def _moe_pre_kernel(x_ref, sh_ref, sc_ref, g_ref, rwt_ref, rb_ref, h_ref, cls_ref, w_ref, cnt_ref):
    first = (pl.program_id(0) == 0) & (pl.program_id(1) == 0)

    @pl.when(first)
    def _():
        cnt_ref[...] = jnp.zeros(cnt_ref.shape, F32)

    tm, d = x_ref.shape[1], x_ref.shape[2]
    pieces = d // LANES
    h = _modulated_norm(x_ref[0], g_ref[...], sc_ref[0], sh_ref[0])
    for s in range(pieces):
        h_ref[pl.ds(s, tm, stride=pieces), :] = h[:, s * LANES:(s + 1) * LANES]
    logits = lax.dot_general(rwt_ref[...], h, _NT, preferred_element_type=F32, precision=HIGHEST)
    scores = jax.nn.sigmoid(logits)
    biased = scores + rb_ref[...]
    row = lambda a, e: a[e:e + 1, :]
    epg = EXPERTS_PER_GROUP

    best, g_sel = None, None
    for g in range(N_EXPERT_GROUPS):
        v = [row(biased, g * epg + j) for j in range(epg)]
        gs = None
        for i in range(epg):
            for j in range(i + 1, epg):
                pair = v[i] + v[j]
                gs = pair if gs is None else jnp.maximum(gs, pair)
        if best is None:
            best, g_sel = gs, jnp.zeros(gs.shape, I32)
        else:
            better = gs > best
            best = jnp.where(better, gs, best)
            g_sel = jnp.where(better, g, g_sel)

    def pick(a, j):
        out = row(a, j)
        for g in range(1, N_EXPERT_GROUPS):
            out = jnp.where(g_sel == g, row(a, g * epg + j), out)
        return out

    vb = [pick(biased, j) for j in range(epg)]
    vs = [pick(scores, j) for j in range(epg)]

    def argmax_first(vals, exclude):
        bv, bi, bs = None, None, None
        for j in range(epg):
            cand = vals[j] if exclude is None else jnp.where(exclude == j, -jnp.inf, vals[j])
            if bv is None:
                bv, bi, bs = cand, jnp.zeros(cand.shape, I32), vs[0]
            else:
                better = cand > bv
                bv = jnp.where(better, cand, bv)
                bi = jnp.where(better, j, bi)
                bs = jnp.where(better, vs[j], bs)
        return bi, bs

    i1, s1 = argmax_first(vb, None)
    i2, s2 = argmax_first(vb, i1)
    tot = s1 + s2
    first_lo = i1 < i2
    lo = jnp.where(first_lo, i1, i2)
    hi = jnp.where(first_lo, i2, i1)
    pair_base = jnp.where(lo == 0, 0, jnp.where(lo == 1, 3, 5))
    cls = g_sel * PAIRS_PER_GROUP + pair_base + (hi - lo - 1)
    cls_ref[...] = cls
    w_ref[...] = jnp.concatenate([jnp.where(first_lo, s1, s2) / tot, jnp.where(first_lo, s2, s1) / tot], axis=0)
    cid = lax.broadcasted_iota(I32, (cnt_ref.shape[0], cls.shape[1]), 0)
    cnt_ref[...] += jnp.sum((cid == cls).astype(F32), axis=1, keepdims=True)


def _moe_pre(x, sh, sc, g_norm, router_w, router_bias, tm=512):
    b, s, d = x.shape
    tm = min(tm, s)
    nt = s // tm
    t_all = b * s
    e = router_w.shape[1]
    pieces = d // LANES
    return pl.pallas_call(
        _moe_pre_kernel,
        grid=(b, nt),
        in_specs=[
            pl.BlockSpec((1, tm, d), lambda bi, i: (bi, i, 0)),
            pl.BlockSpec((1, 1, d), lambda bi, i: (bi, 0, 0)),
            pl.BlockSpec((1, 1, d), lambda bi, i: (bi, 0, 0)),
            pl.BlockSpec((1, d), lambda bi, i: (0, 0)),
            pl.BlockSpec((e, d), lambda bi, i: (0, 0)),
            pl.BlockSpec((e, 1), lambda bi, i: (0, 0)),
        ],
        out_specs=[
            pl.BlockSpec((tm * pieces, LANES), lambda bi, i: (bi * nt + i, 0)),
            pl.BlockSpec((1, tm), lambda bi, i: (0, bi * nt + i)),
            pl.BlockSpec((2, tm), lambda bi, i: (0, bi * nt + i)),
            pl.BlockSpec((N_CLASSES, LANES), lambda bi, i: (0, 0)),
        ],
        out_shape=[
            jax.ShapeDtypeStruct((t_all * pieces, LANES), F32),
            jax.ShapeDtypeStruct((1, t_all), I32),
            jax.ShapeDtypeStruct((2, t_all), F32),
            jax.ShapeDtypeStruct((N_CLASSES, LANES), F32),
        ],
        compiler_params=_params("arbitrary", "arbitrary"),
        name="moe_pre",
    )(x, sh, sc, g_norm, router_w.T, router_bias.reshape(e, 1))


def _moe_rank_kernel(cls_ref, cnt_ref, dest_ref, meta_ref, pstart_sc, run_sc, *, sub, n_sub, nbp):
    nc = cnt_ref.shape[0]

    @pl.when(pl.program_id(0) == 0)
    def _():
        cnt = cnt_ref[...]
        padded = jnp.ceil(cnt / MOE_BLOCK) * MOE_BLOCK
        blk = lax.broadcasted_iota(I32, (1, nbp), 1).astype(F32) * MOE_BLOCK
        running = jnp.zeros((1, LANES), F32)
        block_c = jnp.zeros((1, nbp), F32)
        for c in range(nc):
            pstart_sc[c:c + 1, :] = running
            running = running + padded[c:c + 1, :]
            block_c = block_c + (running[:, 0:1] <= blk).astype(F32)
        run_sc[...] = jnp.zeros(run_sc.shape, F32)
        block_c = jnp.minimum(block_c, nc - 1.0)
        group = jnp.floor(block_c / PAIRS_PER_GROUP)
        pair = block_c - group * PAIRS_PER_GROUP
        lo = (pair >= 3.0).astype(F32) + (pair >= 5.0).astype(F32)
        hi = jnp.where(pair == 0.0, 1.0, jnp.where((pair == 1.0) | (pair == 3.0), 2.0, 3.0))
        n_used = jnp.broadcast_to(running[:, 0:1] / MOE_BLOCK, (1, nbp))
        meta_ref[...] = jnp.concatenate(
            [group * EXPERTS_PER_GROUP + lo, group * EXPERTS_PER_GROUP + hi, n_used], axis=0).astype(I32)

    upper = (lax.broadcasted_iota(I32, (sub, sub), 0) < lax.broadcasted_iota(I32, (sub, sub), 1)).astype(BF16)
    cid = lax.broadcasted_iota(I32, (nc, sub), 0)
    for j in range(n_sub):
        oh = (cid == cls_ref[:, j * sub:(j + 1) * sub]).astype(F32)
        rank = jnp.dot(oh.astype(BF16), upper, preferred_element_type=F32)
        base = pstart_sc[:, 0:1] + run_sc[:, 0:1]
        dest = jnp.sum(oh * (rank + base), axis=0, keepdims=True)
        dest_ref[:, j * sub:(j + 1) * sub] = dest.astype(I32)
        run_sc[...] += jnp.sum(oh, axis=1, keepdims=True)


def _moe_rank(cls, cnt, n_blocks):
    t_all = cls.shape[1]
    sub = 256
    tr = min(2048, t_all)
    nbp = -(-n_blocks // LANES) * LANES
    nc = cnt.shape[0]
    return pl.pallas_call(
        functools.partial(_moe_rank_kernel, sub=sub, n_sub=tr // sub, nbp=nbp),
        grid=(t_all // tr,),
        in_specs=[
            pl.BlockSpec((1, tr), lambda i: (0, i)),
            pl.BlockSpec((nc, LANES), lambda i: (0, 0)),
        ],
        out_specs=[
            pl.BlockSpec((1, tr), lambda i: (0, i)),
            pl.BlockSpec((3, nbp), lambda i: (0, 0)),
        ],
        out_shape=[
            jax.ShapeDtypeStruct((1, t_all), I32),
            jax.ShapeDtypeStruct((3, nbp), I32),
        ],
        scratch_shapes=[pltpu.VMEM((nc, LANES), F32), pltpu.VMEM((nc, LANES), F32)],
        compiler_params=_params("arbitrary"),
        name="moe_rank",
    )(cls, cnt)


def _tile_copy(src_ref, s, dst_ref, d, sem, rows):
    return pltpu.make_async_copy(src_ref.at[pl.ds(pl.multiple_of(s * rows, rows), rows)],
                                 dst_ref.at[pl.ds(pl.multiple_of(d * rows, rows), rows)], sem)


def _moe_dispatch_kernel(dest_hbm, h_ref, xs_in, xs_out, idx_smem, idx_sem, sem, *, tm, pieces):
    del xs_in
    i = pl.program_id(0)
    cp = pltpu.make_async_copy(dest_hbm.at[i], idx_smem, idx_sem)
    cp.start()
    cp.wait()

    def issue(t, carry):
        _tile_copy(h_ref, t, xs_out, idx_smem[0, t], sem, pieces).start()
        return carry

    lax.fori_loop(0, tm, issue, 0, unroll=8)

    def drain(t, carry):
        _tile_copy(h_ref, 0, xs_out, 0, sem, pieces).wait()
        return carry

    lax.fori_loop(0, tm, drain, 0, unroll=8)


def _moe_dispatch(dest_tiles, h, rows, tm, pieces):
    t_all = h.shape[0] // pieces
    zeros = jnp.zeros((rows * pieces, LANES), F32)
    return pl.pallas_call(
        functools.partial(_moe_dispatch_kernel, tm=tm, pieces=pieces),
        grid=(t_all // tm,),
        in_specs=[
            pl.BlockSpec(memory_space=pl.ANY),
            pl.BlockSpec((tm * pieces, LANES), lambda i: (i, 0)),
            pl.BlockSpec(memory_space=pl.ANY),
        ],
        out_specs=pl.BlockSpec(memory_space=pl.ANY),
        out_shape=jax.ShapeDtypeStruct((rows * pieces, LANES), F32),
        scratch_shapes=[pltpu.SMEM((1, tm), I32), pltpu.SemaphoreType.DMA(()), pltpu.SemaphoreType.DMA(())],
        input_output_aliases={2: 0},
        compiler_params=_params("arbitrary"),
        name="moe_dispatch",
    )(dest_tiles, h, zeros)


def _moe_expert_kernel(meta_ref, x_ref, wga_ref, wua_ref, wda_ref, wgb_ref, wub_ref, wdb_ref, y_ref, xb_sc,
                       *, pieces):
    i = pl.program_id(0)
    rows = xb_sc.shape[0]

    @pl.when(i < meta_ref[2, 0])
    def _():
        for s in range(pieces):
            xb_sc[:, s * LANES:(s + 1) * LANES] = x_ref[pl.ds(s, rows, stride=pieces), :].astype(BF16)
        x = xb_sc[...]
        for k, (wg_ref, wu_ref, wd_ref) in enumerate(((wga_ref, wua_ref, wda_ref), (wgb_ref, wub_ref, wdb_ref))):
            gate = jnp.dot(x, wg_ref[0], preferred_element_type=F32)
            up = jnp.dot(x, wu_ref[0], preferred_element_type=F32)
            hidden = (_silu(gate) * up).astype(BF16)
            y = jnp.dot(hidden, wd_ref[0], preferred_element_type=F32)
            for s in range(pieces):
                y_ref[pl.ds(k * pieces + s, rows, stride=2 * pieces), :] = y[:, s * LANES:(s + 1) * LANES]

    @pl.when(i >= meta_ref[2, 0])
    def _():
        y_ref[...] = jnp.zeros(y_ref.shape, F32)


def _moe_experts(meta, xs, w_gate, w_up, w_down, n_blocks, pieces):
    d, f = w_gate.shape[1], w_gate.shape[2]
    wspec = lambda k, a, c: pl.BlockSpec((1, a, c), lambda i, meta: (meta[k, i], 0, 0))
    return pl.pallas_call(
        functools.partial(_moe_expert_kernel, pieces=pieces),
        grid_spec=pltpu.PrefetchScalarGridSpec(
            num_scalar_prefetch=1,
            grid=(n_blocks,),
            in_specs=[pl.BlockSpec((MOE_BLOCK * pieces, LANES), lambda i, meta: (i, 0)),
                      wspec(0, d, f), wspec(0, d, f), wspec(0, f, d),
                      wspec(1, d, f), wspec(1, d, f), wspec(1, f, d)],
            out_specs=pl.BlockSpec((MOE_BLOCK * 2 * pieces, LANES), lambda i, meta: (i, 0)),
            scratch_shapes=[pltpu.VMEM((MOE_BLOCK, d), BF16)],
        ),
        out_shape=jax.ShapeDtypeStruct((2 * xs.shape[0], LANES), F32),
        compiler_params=_params("arbitrary"),
        name="moe_experts",
    )(meta, xs, w_gate, w_up, w_down, w_gate, w_up, w_down)


def _moe_combine_kernel(dest_hbm, y_hbm, w_ref, x_ref, g_ref, o_ref, y_sc, idx_smem, idx_sem, sem,
                        *, tm, nt, pieces):
    tile = pl.program_id(0) * nt + pl.program_id(1)
    cp = pltpu.make_async_copy(dest_hbm.at[tile], idx_smem, idx_sem)
    cp.start()
    cp.wait()

    def issue(t, carry):
        _tile_copy(y_hbm, idx_smem[0, t], y_sc, t, sem, 2 * pieces).start()
        return carry

    lax.fori_loop(0, tm, issue, 0, unroll=8)

    def drain(t, carry):
        _tile_copy(y_hbm, 0, y_sc, 0, sem, 2 * pieces).wait()
        return carry

    lax.fori_loop(0, tm, drain, 0, unroll=8)
    w = w_ref[...]
    w_lo, w_hi = w[:, 0:1], w[:, 1:2]
    for s in range(pieces):
        cols = slice(s * LANES, (s + 1) * LANES)
        moe = (w_lo * y_sc[pl.ds(s, tm, stride=2 * pieces), :]
               + w_hi * y_sc[pl.ds(pieces + s, tm, stride=2 * pieces), :])
        o_ref[0, :, cols] = x_ref[0, :, cols] + g_ref[0, :, cols] * moe


def _moe_combine(dest_tiles, y, w_col, x, gate, tm, pieces):
    b, s, d = x.shape
    nt = s // tm
    return pl.pallas_call(
        functools.partial(_moe_combine_kernel, tm=tm, nt=nt, pieces=pieces),
        grid=(b, nt),
        in_specs=[
            pl.BlockSpec(memory_space=pl.ANY),
            pl.BlockSpec(memory_space=pl.ANY),
            pl.BlockSpec((tm, 2), lambda bi, i: (bi * nt + i, 0)),
            pl.BlockSpec((1, tm, d), lambda bi, i: (bi, i, 0)),
            pl.BlockSpec((1, 1, d), lambda bi, i: (bi, 0, 0)),
        ],
        out_specs=pl.BlockSpec((1, tm, d), lambda bi, i: (bi, i, 0)),
        out_shape=jax.ShapeDtypeStruct((b, s, d), F32),
        scratch_shapes=[
            pltpu.VMEM((tm * 2 * pieces, LANES), F32),
            pltpu.SMEM((1, tm), I32),
            pltpu.SemaphoreType.DMA(()),
            pltpu.SemaphoreType.DMA(()),
        ],
        compiler_params=_params("arbitrary", "arbitrary"),
        name="moe_combine",
    )(dest_tiles, y, w_col, x, gate)


def _moe_layer(x, sh, sc, gate, g_norm, router_w, router_bias, w_gate, w_up, w_down):
    b, s, d = x.shape
    t_all = b * s
    pieces = d // LANES
    tm = min(512, s)
    n_blocks = -(-t_all // MOE_BLOCK) + N_CLASSES
    rows = n_blocks * MOE_BLOCK
    h, cls, w, cnt = _moe_pre(x, sh, sc, g_norm, router_w, router_bias)
    dest, meta = _moe_rank(cls, cnt, n_blocks)
    dest_tiles = dest.reshape(t_all // tm, 1, tm)
    xs = _moe_dispatch(dest_tiles, h, rows, tm, pieces)
    y = _moe_experts(meta, xs, w_gate.astype(BF16), w_up.astype(BF16), w_down.astype(BF16), n_blocks, pieces)
    return _moe_combine(dest_tiles, y, w.T, x, gate, tm, pieces)


def _final_norm_kernel(x_ref, g_ref, o_ref):
    x = x_ref[0]
    ms = jnp.mean(x * x, axis=-1, keepdims=True)
    o_ref[0] = x * lax.rsqrt(ms + NORM_EPS) * g_ref[...]


def _final_norm(x, g, tm=1024):
    b, s, d = x.shape
    tm = min(tm, s)
    return pl.pallas_call(
        _final_norm_kernel,
        grid=(b, s // tm),
        in_specs=[
            pl.BlockSpec((1, tm, d), lambda bi, i: (bi, i, 0)),
            pl.BlockSpec((1, d), lambda bi, i: (0, 0)),
        ],
        out_specs=pl.BlockSpec((1, tm, d), lambda bi, i: (bi, i, 0)),
        out_shape=jax.ShapeDtypeStruct((b, s, d), F32),
        compiler_params=_params("parallel", "arbitrary"),
        name="final_norm",
    )(x, g.reshape(1, d))


def kernel(x, c, ada_w, ada_b, norm1_g, norm2_g, router_w, router_bias, moe_w_gate, moe_w_up, moe_w_down, dil_w_in, dil_w_out, diff_w_in, diff_lam_q1, diff_lam_k1, diff_lam_q2, diff_lam_k2, diff_head_norm_g, diff_w_out, ssm_w_in, ssm_conv_w, ssm_conv_b, ssm_dt_bias, ssm_A_log, ssm_D, ssm_norm_g, ssm_w_out, final_norm_g):
    b, s, d = x.shape
    depth = ada_w.shape[0]
    mod = _ada_mod(c, ada_w, ada_b).reshape(depth, b, 6, 1, d)
    tables = _rope_tables(s)
    for i in range(depth):
        sh1, sc1, g1, sh2, sc2, g2 = (mod[i, :, j] for j in range(6))
        n1 = norm1_g[i].reshape(1, d)
        kind, j = i % 3, i // 3
        if kind == 0:
            x = _dilated_layer(x, sh1, sc1, g1, n1, dil_w_in[j], dil_w_out[j], tables)
        elif kind == 1:
            x = _diff_layer(x, sh1, sc1, g1, n1, diff_w_in[j], diff_lam_q1[j], diff_lam_k1[j],
                            diff_lam_q2[j], diff_lam_k2[j], diff_head_norm_g[j], diff_w_out[j], tables, i)
        else:
            x = _ssd_layer(x, sh1, sc1, g1, n1, ssm_w_in[j], ssm_conv_w[j], ssm_conv_b[j], ssm_dt_bias[j],
                           ssm_A_log[j], ssm_D[j], ssm_norm_g[j], ssm_w_out[j])
        x = _moe_layer(x, sh2, sc2, g2, norm2_g[i].reshape(1, d), router_w, router_bias,
                       moe_w_gate[i], moe_w_up[i], moe_w_down[i])
    return _final_norm(x, final_norm_g)
```

```python
import functools
import math

import jax
import jax.numpy as jnp
from jax import lax
from jax.experimental import pallas as pl
from jax.experimental.pallas import tpu as pltpu

F32 = jnp.float32
BF16 = jnp.bfloat16
I32 = jnp.int32
U32 = jnp.uint32
HIGHEST = lax.Precision.HIGHEST

LANES = 128
SUBLANES = 8
VMEM_LIMIT_BYTES = 56 * 1024 * 1024

NORM_EPS = 1e-6
ROPE_THETA = 500000.0
ROPE_FRACTION = 4
HEAD_DIM = 64
ATTN_BLOCK = 128
DIL_CONFIGS = ((128, 1), (512, 4), (2048, 16))
DIFF_NORM_EPS = 1e-5
SSM_HEAD_DIM = 64
SSM_GROUPS = 4
SSM_STATE = 128
SSM_CONV = 4
SSM_CHUNK = 256
N_EXPERTS = 16
N_EXPERT_GROUPS = 4
EXPERTS_PER_GROUP = 4
PAIRS_PER_GROUP = 6
N_CLASSES = N_EXPERT_GROUPS * PAIRS_PER_GROUP
MOE_BLOCK = 256
COL_CHUNK = 512
NEG = -1e30

_NT = (((1,), (1,)), ((), ()))
_TN = (((0,), (0,)), ((), ()))


def _params(*sem):
    return pltpu.CompilerParams(dimension_semantics=sem, vmem_limit_bytes=VMEM_LIMIT_BYTES)


def _silu(v):
    return v * (0.5 * jnp.tanh(0.5 * v) + 0.5)


def _ada_kernel(c_ref, w_ref, b_ref, o_ref):
    cond = _silu(c_ref[...])
    o_ref[0] = jnp.dot(cond, w_ref[0], preferred_element_type=F32, precision=HIGHEST) + b_ref[0]


def _ada_mod(c, ada_w, ada_b):
    depth, d, n = ada_w.shape
    b = c.shape[0]
    tn = 1536
    return pl.pallas_call(
        _ada_kernel,
        grid=(depth, n // tn),
        in_specs=[
            pl.BlockSpec((b, d), lambda i, j: (0, 0)),
            pl.BlockSpec((1, d, tn), lambda i, j: (i, 0, j)),
            pl.BlockSpec((1, 1, tn), lambda i, j: (i, 0, j)),
        ],
        out_specs=pl.BlockSpec((1, b, tn), lambda i, j: (i, 0, j)),
        out_shape=jax.ShapeDtypeStruct((depth, b, n), F32),
        compiler_params=_params("arbitrary", "arbitrary"),
        name="ada_mod",
    )(c, ada_w, ada_b.reshape(depth, 1, n))


def _rope_tables(seq):
    r = HEAD_DIM // ROPE_FRACTION
    half = r // 2
    inv = jnp.power(ROPE_THETA, -jnp.arange(half, dtype=F32) * 2.0 / r)
    ang = jnp.arange(seq, dtype=F32)[:, None] * inv[None, :]
    cos, sin = jnp.cos(ang), jnp.sin(ang)
    ones = jnp.ones((seq, HEAD_DIM - r), F32)
    zeros = jnp.zeros((seq, HEAD_DIM - r), F32)
    zh = jnp.zeros((seq, half), F32)
    cos_t = jnp.concatenate([cos, cos, ones], axis=1)
    sin_a = jnp.concatenate([zh, sin, zeros], axis=1)
    sin_b = jnp.concatenate([-sin, zh, zeros], axis=1)
    rep = LANES // HEAD_DIM
    lane_form = tuple(jnp.tile(t, (1, rep)) for t in (cos_t, sin_a, sin_b))
    return lane_form, tuple(t.T for t in lane_form)


def _modulated_norm(x, g, sc, sh):
    ms = jnp.mean(x * x, axis=-1, keepdims=True)
    return x * lax.rsqrt(ms + NORM_EPS) * g * (1.0 + sc) + sh


def _rope(a, cos, sa, sb, scale, axis):
    half = HEAD_DIM // ROPE_FRACTION // 2
    r = a * cos + pltpu.roll(a, half, axis) * sa + pltpu.roll(a, LANES - half, axis) * sb
    return r if scale == 1.0 else r * scale


def _proj_kernel(*refs, chunks, tplan, n_out, use_rope, has_perm):
    x_ref, sh_ref, sc_ref, g_ref, w_ref = refs[:5]
    pos = 5
    if use_rope:
        cos_ref, sa_ref, sb_ref = refs[pos:pos + 3]
        pos += 3
    if tplan is not None:
        wt_ref, cost_ref, sat_ref, sbt_ref = refs[pos:pos + 4]
        pos += 4
    out_refs = refs[pos:pos + n_out]
    perm_sc = refs[pos + n_out] if has_perm else None
    tm = x_ref.shape[1]
    h = _modulated_norm(x_ref[0], g_ref[...], sc_ref[0], sh_ref[0]).astype(BF16)
    tiles_per_chunk = COL_CHUNK // LANES
    for c, (oi, off, scale, dil) in enumerate(chunks):
        o_ref = out_refs[oi]
        acc = jnp.dot(h, w_ref[:, c * COL_CHUNK:(c + 1) * COL_CHUNK], preferred_element_type=F32)
        tiles = None
        if scale is not None:
            cos, sa, sb = cos_ref[...], sa_ref[...], sb_ref[...]
            tiles = [_rope(acc[:, s * LANES:(s + 1) * LANES], cos, sa, sb, scale, 1)
                     for s in range(tiles_per_chunk)]
        if dil == 1:
            if tiles is None:
                o_ref[0, :, off:off + COL_CHUNK] = acc.astype(o_ref.dtype)
            else:
                for s, tl in enumerate(tiles):
                    o_ref[0, :, off + s * LANES:off + (s + 1) * LANES] = tl.astype(o_ref.dtype)
            continue
        if tiles is None:
            tiles = [acc[:, s * LANES:(s + 1) * LANES] for s in range(tiles_per_chunk)]
        for s, tl in enumerate(tiles):
            perm_sc[s] = tl
        for rho in range(dil):
            for s in range(tiles_per_chunk):
                o_ref[0, rho, :, off + s * LANES:off + (s + 1) * LANES] = (
                    perm_sc[s, pl.ds(rho, tm // dil, stride=dil), :].astype(o_ref.dtype))
    if tplan is not None:
        oi, scale = tplan
        o_ref = out_refs[oi]
        acct = lax.dot_general(wt_ref[...], h, _NT, preferred_element_type=F32)
        cost, sat, sbt = cost_ref[...], sat_ref[...], sbt_ref[...]
        for s in range(acct.shape[0] // LANES):
            rows = slice(s * LANES, (s + 1) * LANES)
            o_ref[0, rows, :] = _rope(acct[rows, :], cost, sat, sbt, scale, 0).astype(o_ref.dtype)


def _proj(x, sh, sc, g, w, chunks, outs, tables, wt=None, tplan=None, tm=512):
    b, s, d = x.shape
    n = w.shape[1]
    tm = min(tm, s)
    use_rope = any(c[2] is not None for c in chunks)
    has_perm = any(c[3] > 1 for c in chunks)
    in_specs = [
        pl.BlockSpec((1, tm, d), lambda bi, i: (bi, i, 0)),
        pl.BlockSpec((1, 1, d), lambda bi, i: (bi, 0, 0)),
        pl.BlockSpec((1, 1, d), lambda bi, i: (bi, 0, 0)),
        pl.BlockSpec((1, d), lambda bi, i: (0, 0)),
        pl.BlockSpec((d, n), lambda bi, i: (0, 0), pipeline_mode=pl.Buffered(1)),
    ]
    args = [x, sh, sc, g, w]
    if use_rope:
        in_specs += [pl.BlockSpec((tm, LANES), lambda bi, i: (i, 0))] * 3
        args += list(tables[0])
    if tplan is not None:
        in_specs.append(pl.BlockSpec(wt.shape, lambda bi, i: (0, 0), pipeline_mode=pl.Buffered(1)))
        in_specs += [pl.BlockSpec((LANES, tm), lambda bi, i: (0, i))] * 3
        args += [wt] + list(tables[1])
    out_specs, out_shape = [], []
    for layout, wd, dt, dil in outs:
        if layout == "tok":
            out_specs.append(pl.BlockSpec((1, tm, wd), lambda bi, i: (bi, i, 0)))
            out_shape.append(jax.ShapeDtypeStruct((b, s, wd), dt))
        elif layout == "res":
            out_specs.append(pl.BlockSpec((1, dil, tm // dil, wd), lambda bi, i: (bi, 0, i, 0)))
            out_shape.append(jax.ShapeDtypeStruct((b, dil, s // dil, wd), dt))
        else:
            out_specs.append(pl.BlockSpec((1, wd, tm), lambda bi, i: (bi, 0, i)))
            out_shape.append(jax.ShapeDtypeStruct((b, wd, s), dt))
    return pl.pallas_call(
        functools.partial(_proj_kernel, chunks=tuple(chunks), tplan=tplan, n_out=len(outs),
                          use_rope=use_rope, has_perm=has_perm),
        grid=(b, s // tm),
        in_specs=in_specs,
        out_specs=out_specs,
        out_shape=out_shape,
        scratch_shapes=[pltpu.VMEM((COL_CHUNK // LANES, tm, LANES), F32)] if has_perm else [],
        compiler_params=_params("parallel", "arbitrary"),
        name="norm_proj",
    )(*args)


def _outproj_kernel(y_ref, w_ref, x_ref, g_ref, o_ref):
    y = jnp.dot(y_ref[0], w_ref[...], preferred_element_type=F32)
    o_ref[0] = x_ref[0] + g_ref[0] * y


def _outproj(y, w, x, gate, tm=512):
    b, s, d = x.shape
    k = y.shape[-1]
    tm = min(tm, s)
    return pl.pallas_call(
        _outproj_kernel,
        grid=(b, s // tm),
        in_specs=[
            pl.BlockSpec((1, tm, k), lambda bi, i: (bi, i, 0)),
            pl.BlockSpec((k, d), lambda bi, i: (0, 0), pipeline_mode=pl.Buffered(1)),
            pl.BlockSpec((1, tm, d), lambda bi, i: (bi, i, 0)),
            pl.BlockSpec((1, 1, d), lambda bi, i: (bi, 0, 0)),
        ],
        out_specs=pl.BlockSpec((1, tm, d), lambda bi, i: (bi, i, 0)),
        out_shape=jax.ShapeDtypeStruct((b, s, d), F32),
        compiler_params=_params("parallel", "arbitrary"),
        name="out_proj",
    )(y, w, x, gate)


def _dil_kernel(q_ref, kc_ref, kp_ref, vc_ref, vp_ref, o_ref, lse_ref, kbuf, vbuf, *, tq, back, heads):
    n = pl.program_id(2)
    qb = ATTN_BLOCK
    kbuf[0:qb] = kp_ref[0, 0]
    kbuf[qb:] = kc_ref[0, 0]
    vbuf[0:qb] = vp_ref[0, 0]
    vbuf[qb:] = vc_ref[0, 0]
    rows = heads * qb
    qi = lax.broadcasted_iota(I32, (rows, 2 * qb), 0) & (qb - 1)
    kj = lax.broadcasted_iota(I32, (rows, 2 * qb), 1)
    rel = kj - qi
    band = (rel >= qb - back) & (rel <= qb)
    lane = lax.broadcasted_iota(I32, (qb, LANES), 1)
    low_half = lane < HEAD_DIM
    zero_q = jnp.zeros((qb, LANES), BF16)
    ones_k = jnp.ones((2 * qb, LANES), BF16)
    for j in range(tq // qb):
        first_key = n * tq + (j - 1) * qb
        valid = band & (kj + first_key >= 0)
        parts = []
        for hp in range(heads // 2):
            cols = slice(hp * LANES, (hp + 1) * LANES)
            q2 = q_ref[0, 0, j * qb:(j + 1) * qb, cols]
            q_stack = jnp.concatenate([jnp.where(low_half, q2, zero_q), jnp.where(low_half, zero_q, q2)], axis=0)
            parts.append(lax.dot_general(q_stack, kbuf[j * qb:(j + 2) * qb, cols], _NT,
                                         preferred_element_type=F32))
        s = jnp.where(valid, jnp.concatenate(parts, axis=0), NEG)
        m = jnp.max(s, axis=-1, keepdims=True)
        pb = jnp.exp2(s - m).astype(BF16)
        l = jnp.dot(pb, ones_k, preferred_element_type=F32)
        inv = 1.0 / l
        lse = (jnp.broadcast_to(m, l.shape) + jnp.log2(l)) * math.log(2.0)
        lse_tile = jnp.zeros((qb, LANES), F32)
        for hp in range(heads // 2):
            cols = slice(hp * LANES, (hp + 1) * LANES)
            ra = slice(2 * hp * qb, (2 * hp + 1) * qb)
            rb = slice((2 * hp + 1) * qb, (2 * hp + 2) * qb)
            pv = jnp.dot(pb[2 * hp * qb:(2 * hp + 2) * qb], vbuf[j * qb:(j + 2) * qb, cols],
                         preferred_element_type=F32)
            o = jnp.where(low_half, pv[:qb] * inv[ra], pv[qb:] * inv[rb])
            o_ref[0, 0, j * qb:(j + 1) * qb, cols] = o.astype(o_ref.dtype)
            lse_tile = jnp.where(lane == 2 * hp, lse[ra], lse_tile)
            lse_tile = jnp.where(lane == 2 * hp + 1, lse[rb], lse_tile)
        lse_ref[0, 0, j * qb:(j + 1) * qb, :] = lse_tile


def _dil_group(proj, window, dilation):
    b, dil, ln, c = proj.shape
    back = window // dilation
    assert dil == dilation and back <= ATTN_BLOCK and ln % ATTN_BLOCK == 0
    width = c // 3
    heads = width // HEAD_DIM
    tq = min(512, ln)
    sub = tq // ATTN_BLOCK

    def cur(col):
        return pl.BlockSpec((1, 1, tq, width), lambda bi, r, n: (bi, r, n, col))

    def prev(col):
        return pl.BlockSpec((1, 1, ATTN_BLOCK, width),
                            lambda bi, r, n: (bi, r, jnp.maximum(n * sub - 1, 0), col))

    return pl.pallas_call(
        functools.partial(_dil_kernel, tq=tq, back=back, heads=heads),
        grid=(b, dil, ln // tq),
        in_specs=[cur(0), cur(1), prev(1), cur(2), prev(2)],
        out_specs=[
            pl.BlockSpec((1, 1, tq, width), lambda bi, r, n: (bi, r, n, 0)),
            pl.BlockSpec((1, 1, tq, LANES), lambda bi, r, n: (bi, r, n, 0)),
        ],
        out_shape=[
            jax.ShapeDtypeStruct((b, dil, ln, width), BF16),
            jax.ShapeDtypeStruct((b, dil, ln, LANES), F32),
        ],
        scratch_shapes=[pltpu.VMEM((tq + ATTN_BLOCK, width), BF16)] * 2,
        compiler_params=_params("parallel", "parallel", "arbitrary"),
        name=f"dil_attn_d{dilation}",
    )(proj, proj, proj, proj, proj)


def _dil_out_kernel(o0_ref, o1_ref, o2_ref, l0_ref, l1_ref, l2_ref, w_ref, x_ref, g_ref, out_ref,
                    o_sc, l1_sc, l2_sc):
    tm = x_ref.shape[1]

    def to_token_order(src_ref, dst_sc):
        dil = src_ref.shape[1]
        for rho in range(dil):
            src = src_ref[0, rho].astype(F32)
            for s in range(dst_sc.shape[0]):
                dst_sc[s, pl.ds(rho, tm // dil, stride=dil), :] = src[:, s * LANES:(s + 1) * LANES]
        return jnp.concatenate([dst_sc[s] for s in range(dst_sc.shape[0])], axis=1)

    ls = [l0_ref[0, 0], to_token_order(l1_ref, l1_sc), to_token_order(l2_ref, l2_sc)]
    mx =jnp.maximum(jnp.maximum(ls[0], ls[1]), ls[2])
    es = [jnp.exp(v - mx) for v in ls]
    inv = 1.0 / (es[0] + es[1] + es[2])
    width = o0_ref.shape[-1]
    expand = (lax.broadcasted_iota(I32, (LANES, width), 0)
              == lax.broadcasted_iota(I32, (LANES, width), 1) // HEAD_DIM).astype(BF16)
    o = jnp.zeros((tm, width), F32)
    for gi, (e, o_ref) in enumerate(zip(es, (o0_ref, o1_ref, o2_ref))):
        alpha = e * inv
        hi = alpha.astype(BF16)
        lo = (alpha - hi.astype(F32)).astype(BF16)
        a_full = (jnp.dot(hi, expand, preferred_element_type=F32)
                  + jnp.dot(lo, expand, preferred_element_type=F32))
        og = o_ref[0, 0].astype(F32) if gi == 0 else to_token_order(o_ref, o_sc)
        o = o + a_full * og
    y = jnp.dot(o.astype(BF16), w_ref[...], preferred_element_type=F32)
    out_ref[0] = x_ref[0] + g_ref[0] * y


def _dil_out(os_, lses, w, x, gate, tm=512):
    b, s, d = x.shape
    width = os_[0].shape[-1]
    tm = min(tm, s)
    tok = lambda wd: pl.BlockSpec((1, tm, wd), lambda bi, i: (bi, i, 0))
    res = lambda a: pl.BlockSpec((1, a.shape[1], tm // a.shape[1], a.shape[3]), lambda bi, i: (bi, 0, i, 0))
    return pl.pallas_call(
        _dil_out_kernel,
        grid=(b, s // tm),
        in_specs=[res(a) for a in os_] + [res(a) for a in lses] + [
            pl.BlockSpec((width, d), lambda bi, i: (0, 0)),
            tok(d),
            pl.BlockSpec((1, 1, d), lambda bi, i: (bi, 0, 0)),
        ],
        out_specs=tok(d),
        out_shape=jax.ShapeDtypeStruct((b, s, d), F32),
        scratch_shapes=[pltpu.VMEM((width // LANES, tm, LANES), F32), pltpu.VMEM((1, tm, LANES), F32),
                        pltpu.VMEM((1, tm, LANES), F32)],
        compiler_params=_params("parallel", "arbitrary"),
        name="dil_out",
    )(*os_, *lses, w, x, gate)


def _dilated_layer(x, sh, sc, gate, g_norm, w_in, w_out, tables):
    b, s, _ = x.shape
    n = w_in.shape[1]
    gw = n // len(DIL_CONFIGS)
    chunks, outs = [], []
    for g, (_, dilation) in enumerate(DIL_CONFIGS):
        for kind in range(3):
            scale = (HEAD_DIM ** -0.5 * math.log2(math.e), 1.0, None)[kind]
            chunks.append((g, kind * COL_CHUNK, scale, dilation))
        outs.append(("tok" if dilation == 1 else "res", gw, BF16, dilation))
    projs = _proj(x, sh, sc, g_norm, w_in.astype(BF16), chunks, outs, tables)
    os_, lses = [], []
    for proj, (window, dilation) in zip(projs, DIL_CONFIGS):
        o, lse = _dil_group(proj.reshape(b, dilation, s // dilation, gw), window, dilation)
        os_.append(o)
        lses.append(lse)
    return _dil_out(os_, lses, w_out.astype(BF16), x, gate)


def _diff_kernel(qi_tab, ki_tab, q_ref, kt_ref, v_ref, lq1_ref, lk1_ref, lq2_ref, lk2_ref, hg_ref, o_ref,
                 m_sc, acc_sc, *, t, r, lam_init):
    pair = pl.program_id(2)
    qi = qi_tab[pair]
    ki = ki_tab[pair]
    vw = v_ref.shape[-1]

    @pl.when(ki == 0)
    def _():
        m_sc[...] = jnp.full(m_sc.shape, NEG, F32)
        acc_sc[...] = jnp.zeros(acc_sc.shape, F32)

    def step(diagonal, r):
        kt = kt_ref[0]
        v_aug = jnp.concatenate([v_ref[0], jnp.ones((t, LANES), BF16)], axis=1)
        lane = lax.broadcasted_iota(I32, (r, vw), 1)
        work = [(mi, c) for mi in range(2) for c in range(t // r)]

        def scores(mi, c):
            in_half = (lane >= mi * HEAD_DIM) & (lane < (mi + 1) * HEAD_DIM)
            ncols = (c + 1) * r if diagonal else t
            q_c = jnp.where(in_half, q_ref[0, c * r:(c + 1) * r, :], jnp.zeros((r, vw), BF16))
            return jnp.dot(q_c, kt[:, :ncols], preferred_element_type=F32)

        ahead = 3
        pending = [scores(*wk) for wk in work[:ahead]]
        for idx, (mi, c) in enumerate(work):
            s = pending.pop(0)
            if idx + ahead < len(work):
                pending.append(scores(*work[idx + ahead]))
            ncols = s.shape[1]
            if diagonal:
                col = lax.broadcasted_iota(I32, (r, ncols), 1)
                row = lax.broadcasted_iota(I32, (r, ncols), 0) + c * r
                s = jnp.where(col <= row, s, NEG)
            tiles = [s[:, j * LANES:(j + 1) * LANES] for j in range(ncols // LANES)]
            tmax = functools.reduce(jnp.maximum, tiles)
            srows = slice(mi * t + c * r, mi * t + (c + 1) * r)
            m_prev = m_sc[srows, :]
            m_new = jnp.maximum(m_prev, jnp.max(tmax, axis=-1, keepdims=True))
            alpha = jnp.exp2(m_prev - m_new)
            p = jnp.concatenate([jnp.exp2(tl - m_new) for tl in tiles], axis=1).astype(BF16)
            pv = jnp.dot(p, v_aug[:ncols], preferred_element_type=F32)
            acc_sc[srows, :] = jnp.concatenate([alpha, alpha], axis=1) * acc_sc[srows, :] + pv
            m_sc[srows, :] = m_new

    @pl.when(ki < qi)
    def _():
        step(False, 2 * r)

    @pl.when(ki == qi)
    def _():
        step(True, 2 * r)
        lam = (jnp.exp(jnp.sum(lq1_ref[...] * lk1_ref[...], axis=-1, keepdims=True))
               - jnp.exp(jnp.sum(lq2_ref[...] * lk2_ref[...], axis=-1, keepdims=True)) + lam_init)
        o = acc_sc[:t, :vw] / acc_sc[:t, vw:] - lam * (acc_sc[t:, :vw] / acc_sc[t:, vw:])
        ms = jnp.mean(o * o, axis=-1, keepdims=True)
        o = o * lax.rsqrt(ms + DIFF_NORM_EPS) * hg_ref[...] * (1.0 - lam_init)
        o_ref[0] = o.astype(o_ref.dtype)


def _diff_attention(q, kt, v, lam_q1, lam_k1, lam_q2, lam_k2, head_g, lam_init, t=1024, r=128):
    b, s, d = q.shape
    vw = 2 * HEAD_DIM
    assert vw == LANES
    heads = d // vw
    t = min(t, s)
    r = min(r, t // 2)
    nq = s // t
    pairs = [(qi, ki) for qi in range(nq) for ki in range(qi + 1)]
    qi_tab = jnp.asarray([p[0] for p in pairs], I32)
    ki_tab = jnp.asarray([p[1] for p in pairs], I32)
    vec = lambda n: pl.BlockSpec((1, n), lambda bi, h, p, qt, kt_: (0, 0))
    return pl.pallas_call(
        functools.partial(_diff_kernel, t=t, r=r, lam_init=lam_init),
        grid_spec=pltpu.PrefetchScalarGridSpec(
            num_scalar_prefetch=2,
            grid=(b, heads, len(pairs)),
            in_specs=[
                pl.BlockSpec((1, t, vw), lambda bi, h, p, qt, kt_: (bi, qt[p], h)),
                pl.BlockSpec((1, vw, t), lambda bi, h, p, qt, kt_: (bi, h, kt_[p])),
                pl.BlockSpec((1, t, vw), lambda bi, h, p, qt, kt_: (bi, kt_[p], h)),
                vec(HEAD_DIM), vec(HEAD_DIM), vec(HEAD_DIM), vec(HEAD_DIM), vec(vw),
            ],
            out_specs=pl.BlockSpec((1, t, vw), lambda bi, h, p, qt, kt_: (bi, qt[p], h)),
            scratch_shapes=[
                pltpu.VMEM((2 * t, LANES), F32),
                pltpu.VMEM((2 * t, vw + LANES), F32),
            ],
        ),
        out_shape=jax.ShapeDtypeStruct((b, s, d), BF16),
        compiler_params=_params("parallel", "parallel", "arbitrary"),
        name="diff_attn",
    )(qi_tab, ki_tab, q, kt, v, lam_q1.reshape(1, -1), lam_k1.reshape(1, -1), lam_q2.reshape(1, -1),
      lam_k2.reshape(1, -1), head_g.reshape(1, -1))


def _diff_layer(x, sh, sc, gate, g_norm, w_in, lam_q1, lam_k1, lam_q2, lam_k2, head_g, w_out,
                tables, layer_idx):
    d = w_in.shape[1] // 3
    nch = d // COL_CHUNK
    q_scale = HEAD_DIM ** -0.5 * math.log2(math.e)
    chunks = ([(0, c * COL_CHUNK, q_scale, 1) for c in range(nch)]
              + [(1, c * COL_CHUNK, None, 1) for c in range(nch)])
    w_qv = jnp.concatenate([w_in[:, :d], w_in[:, 2 * d:]], axis=1).astype(BF16)
    w_kt = w_in[:, d:2 * d].T.astype(BF16)
    q, v, kt = _proj(x, sh, sc, g_norm, w_qv, chunks,
                     [("tok", d, BF16, 1), ("tok", d, BF16, 1), ("T", d, BF16, 1)], tables,
                     wt=w_kt, tplan=(2, 1.0))
    lam_init = 0.8 - 0.6 * math.exp(-0.3 * layer_idx)
    o = _diff_attention(q, kt, v, lam_q1, lam_k1, lam_q2, lam_k2, head_g, lam_init)
    return _outproj(o, w_out.astype(BF16), x, gate)


def _causal_conv_silu(cur_ref, ext_sc, w_ref, b_ref):
    cur = cur_ref[0].astype(F32)
    rows = cur.shape[0]
    ext_sc[SUBLANES:, :] = cur
    w = w_ref[...]
    acc = cur * w[SSM_CONV - 1:SSM_CONV] + b_ref[...]
    for k in range(1, SSM_CONV):
        acc = acc + ext_sc[SUBLANES - k:SUBLANES - k + rows, :] * w[SSM_CONV - 1 - k:SSM_CONV - k]
    ext_sc[0:SUBLANES, :] = cur[rows - SUBLANES:rows]
    return _silu(acc)


def _ssd_kernel(z_ref, x_ref, b_ref, c_ref, dt_ref, wx_ref, wb_ref, wc_ref, bx_ref, bb_ref, bc_ref,
                dtb_ref, alog_ref, dskip_ref, ng_ref, y_ref,
                state_sc, tx_sc, tb_sc, tc_sc, *, hpg):
    ci = pl.program_id(2)
    L = x_ref.shape[1]
    P = SSM_HEAD_DIM
    gw = hpg * P

    @pl.when(ci == 0)
    def _():
        state_sc[...] = jnp.zeros(state_sc.shape, F32)
        tx_sc[0:SUBLANES, :] = jnp.zeros((SUBLANES, tx_sc.shape[1]), F32)
        tb_sc[0:SUBLANES, :] = jnp.zeros((SUBLANES, tb_sc.shape[1]), F32)
        tc_sc[0:SUBLANES, :] = jnp.zeros((SUBLANES, tc_sc.shape[1]), F32)

    xs = _causal_conv_silu(x_ref, tx_sc, wx_ref, bx_ref)
    bm = _causal_conv_silu(b_ref, tb_sc, wb_ref, bb_ref).astype(BF16)
    cm = _causal_conv_silu(c_ref, tc_sc, wc_ref, bc_ref).astype(BF16)

    raw = dt_ref[0] + dtb_ref[...]
    dt = jnp.maximum(raw, 0.0) + jnp.log(1.0 + jnp.exp(-jnp.abs(raw)))
    a = -jnp.exp(alog_ref[...])
    da = dt * a
    ti = lax.broadcasted_iota(I32, (L, L), 0)
    si = lax.broadcasted_iota(I32, (L, L), 1)
    causal = ti >= si
    acum = jnp.dot(causal.astype(F32), da, preferred_element_type=F32, precision=HIGHEST)
    acum2 = acum * math.log2(math.e)
    acum2_t = acum2.T
    last = acum[L - 1:L, :]
    H = L // 2
    tri = causal[:H, :H]
    expand = (lax.broadcasted_iota(I32, (LANES, gw), 0)
              == lax.broadcasted_iota(I32, (LANES, gw), 1) // P).astype(BF16)

    def per_head_lanes(a):
        hi = a.astype(BF16)
        lo = (a - hi.astype(F32)).astype(BF16)
        return (jnp.dot(hi, expand, preferred_element_type=F32)
                + jnp.dot(lo, expand, preferred_element_type=F32))

    dx = per_head_lanes(dt) * xs
    dxb = dx.astype(BF16)
    eacum_e = per_head_lanes(jnp.exp(acum))
    wdx = (per_head_lanes(jnp.exp(last - acum)) * dx).astype(BF16)

    cb = lax.dot_general(cm, bm, _NT, preferred_element_type=F32)
    y_inter = jnp.dot(cm, state_sc[...].astype(BF16), preferred_element_type=F32)

    low_half = lax.broadcasted_iota(I32, (L, LANES), 1) < P
    y_pairs = []
    for hp in range(hpg // 2):
        cols = slice(hp * LANES, (hp + 1) * LANES)
        ys = []
        for k in (2 * hp, 2 * hp + 1):
            col = acum2[:, k:k + 1]
            row = acum2_t[k:k + 1, :]
            m00 = cb[:H, :H] * jnp.exp2(jnp.where(tri, col[:H] - row[:, :H], NEG))
            m10 = cb[H:, :H] * jnp.exp2(col[H:] - row[:, :H])
            m11 = cb[H:, H:] * jnp.exp2(jnp.where(tri, col[H:] - row[:, H:], NEG))
            top = jnp.dot(m00.astype(BF16), dxb[:H, cols], preferred_element_type=F32)
            bot = (jnp.dot(m10.astype(BF16), dxb[:H, cols], preferred_element_type=F32)
                   + jnp.dot(m11.astype(BF16), dxb[H:, cols], preferred_element_type=F32))
            ys.append(jnp.concatenate([top, bot], axis=0))
        y_pairs.append(jnp.where(low_half, ys[0], ys[1]))

    state_sc[...] = (state_sc[...] * eacum_e[L - 1:L, :]
                     + lax.dot_general(bm, wdx, _TN, preferred_element_type=F32))

    y = jnp.concatenate(y_pairs, axis=1) + y_inter * eacum_e + dskip_ref[...] * xs
    y = y * _silu(z_ref[0].astype(F32))
    ms = jnp.mean(y * y, axis=-1, keepdims=True)
    y_ref[0] = (y * lax.rsqrt(ms + NORM_EPS) * ng_ref[...]).astype(y_ref.dtype)


def _ssd_scan(main, dt, conv_w, conv_b, dt_bias, a_log, d_skip, norm_g):
    b, s, _ = main.shape
    G, N, P, L = SSM_GROUPS, SSM_STATE, SSM_HEAD_DIM, SSM_CHUNK
    di = norm_g.shape[-1]
    gw = di // G
    hpg = gw // P
    assert s % L == 0
    xoff = di // gw
    boff = 2 * di // N
    coff = boff + G
    seq = lambda wd, off: pl.BlockSpec((1, L, wd), lambda bi, g, c: (bi, c, off + g))
    par = lambda r, wd, off: pl.BlockSpec((r, wd), lambda bi, g, c: (0, off + g))
    return pl.pallas_call(
        functools.partial(_ssd_kernel, hpg=hpg),
        grid=(b, G, s // L),
        in_specs=[
            seq(gw, 0), seq(gw, xoff), seq(N, boff), seq(N, coff), seq(LANES, 0),
            par(SSM_CONV, gw, 0), par(SSM_CONV, N, di // N), par(SSM_CONV, N, di // N + G),
            par(1, gw, 0), par(1, N, di // N), par(1, N, di // N + G),
            par(1, LANES, 0), par(1, LANES, 0), par(1, gw, 0), par(1, gw, 0),
        ],
        out_specs=seq(gw, 0),
        out_shape=jax.ShapeDtypeStruct((b, s, di), BF16),
        scratch_shapes=[
            pltpu.VMEM((N, gw), F32),
            pltpu.VMEM((SUBLANES + L, gw), F32),
            pltpu.VMEM((SUBLANES + L, N), F32),
            pltpu.VMEM((SUBLANES + L, N), F32),
        ],
        compiler_params=_params("parallel", "parallel", "arbitrary"),
        name="ssd_scan",
    )(main, main, main, main, dt, conv_w, conv_w, conv_w, conv_b, conv_b, conv_b,
      dt_bias, a_log, d_skip, norm_g)


def _pad_heads(v, groups):
    hpg = v.shape[0] // groups
    return jnp.pad(v.reshape(groups, hpg), ((0, 0), (0, LANES - hpg))).reshape(1, groups * LANES)


def _ssd_layer(x, sh, sc, gate, g_norm, w_in, conv_w, conv_b, dt_bias, a_log, d_skip, norm_g, w_out):
    G = SSM_GROUPS
    di = norm_g.shape[0]
    heads = dt_bias.shape[0]
    hpg = heads // G
    n_main = 2 * di + 2 * G * SSM_STATE
    d = w_in.shape[0]
    w_dt = w_in[:, n_main:].reshape(d, G, hpg)
    w_dt = jnp.pad(w_dt, ((0, 0), (0, 0), (0, LANES - hpg))).reshape(d, G * LANES)
    w = jnp.concatenate([w_in[:, :n_main], w_dt], axis=1).astype(BF16)
    chunks = [(0, c * COL_CHUNK, None, 1) for c in range(n_main // COL_CHUNK)] + [(1, 0, None, 1)]
    main, dt = _proj(x, sh, sc, g_norm, w, chunks, [("tok", n_main, BF16, 1), ("tok", G * LANES, F32, 1)], None)
    y = _ssd_scan(main, dt, conv_w, conv_b.reshape(1, -1), _pad_heads(dt_bias, G), _pad_heads(a_log, G),
                  jnp.repeat(d_skip, SSM_HEAD_DIM).reshape(1, -1), norm_g.reshape(1, -1))
    return _outproj(y, w_out.astype(BF16), x, gate)


def _moe_pre_kernel(x_ref, sh_ref, sc_ref, g_ref, rwt_ref, rb_ref, h_ref, cls_ref, w_ref, cnt_ref):
    first = (pl.program_id(0) == 0) & (pl.program_id(1) == 0)

    @pl.when(first)
    def _():
        cnt_ref[...] = jnp.zeros(cnt_ref.shape, F32)

    tm, d = x_ref.shape[1], x_ref.shape[2]
    pieces = d // LANES
    h = _modulated_norm(x_ref[0], g_ref[...], sc_ref[0], sh_ref[0])
    for s in range(pieces):
        h_ref[pl.ds(s, tm, stride=pieces), :] = h[:, s * LANES:(s + 1) * LANES]
    logits = lax.dot_general(rwt_ref[...], h, _NT, preferred_element_type=F32, precision=HIGHEST)
    scores = jax.nn.sigmoid(logits)
    biased = scores + rb_ref[...]
    row = lambda a, e: a[e:e + 1, :]
    epg = EXPERTS_PER_GROUP

    best, g_sel = None, None
    for g in range(N_EXPERT_GROUPS):
        v = [row(biased, g * epg + j) for j in range(epg)]
        gs = None
        for i in range(epg):
            for j in range(i + 1, epg):
                pair = v[i] + v[j]
                gs = pair if gs is None else jnp.maximum(gs, pair)
        if best is None:
            best, g_sel = gs, jnp.zeros(gs.shape, I32)
        else:
            better = gs > best
            best = jnp.where(better, gs, best)
            g_sel = jnp.where(better, g, g_sel)

    def pick(a, j):
        out = row(a, j)
        for g in range(1, N_EXPERT_GROUPS):
            out = jnp.where(g_sel == g, row(a, g * epg + j), out)
        return out

    vb = [pick(biased, j) for j in range(epg)]
    vs = [pick(scores, j) for j in range(epg)]

    def argmax_first(vals, exclude):
        bv, bi, bs = None, None, None
        for j in range(epg):
            cand = vals[j] if exclude is None else jnp.where(exclude == j, -jnp.inf, vals[j])
            if bv is None:
                bv, bi, bs = cand, jnp.zeros(cand.shape, I32), vs[0]
            else:
                better = cand > bv
                bv = jnp.where(better, cand, bv)
                bi = jnp.where(better, j, bi)
                bs = jnp.where(better, vs[j], bs)
        return bi, bs

    i1, s1 = argmax_first(vb, None)
    i2, s2 = argmax_first(vb, i1)
    tot = s1 + s2
    first_lo = i1 < i2
    lo = jnp.where(first_lo, i1, i2)
    hi = jnp.where(first_lo, i2, i1)
    pair_base = jnp.where(lo == 0, 0, jnp.where(lo == 1, 3, 5))
    cls = g_sel * PAIRS_PER_GROUP + pair_base + (hi - lo - 1)
    cls_ref[...] = cls
    w_ref[...] = jnp.concatenate([jnp.where(first_lo, s1, s2) / tot, jnp.where(first_lo, s2, s1) / tot], axis=0)
    cid = lax.broadcasted_iota(I32, (cnt_ref.shape[0], cls.shape[1]), 0)
    cnt_ref[...] += jnp.sum((cid == cls).astype(F32), axis=1, keepdims=True)


def _moe_pre(x, sh, sc, g_norm, router_w, router_bias, tm=512):
    b, s, d = x.shape
    tm = min(tm, s)
    nt = s // tm
    t_all = b * s
    e = router_w.shape[1]
    pieces = d // LANES
    return pl.pallas_call(
        _moe_pre_kernel,
        grid=(b, nt),
        in_specs=[
            pl.BlockSpec((1, tm, d), lambda bi, i: (bi, i, 0)),
            pl.BlockSpec((1, 1, d), lambda bi, i: (bi, 0, 0)),
            pl.BlockSpec((1, 1, d), lambda bi, i: (bi, 0, 0)),
            pl.BlockSpec((1, d), lambda bi, i: (0, 0)),
            pl.BlockSpec((e, d), lambda bi, i: (0, 0)),
            pl.BlockSpec((e, 1), lambda bi, i: (0, 0)),
        ],
        out_specs=[
            pl.BlockSpec((tm * pieces, LANES), lambda bi, i: (bi * nt + i, 0)),
            pl.BlockSpec((1, tm), lambda bi, i: (0, bi * nt + i)),
            pl.BlockSpec((2, tm), lambda bi, i: (0, bi * nt + i)),
            pl.BlockSpec((N_CLASSES, LANES), lambda bi, i: (0, 0)),
        ],
        out_shape=[
            jax.ShapeDtypeStruct((t_all * pieces, LANES), F32),
            jax.ShapeDtypeStruct((1, t_all), I32),
            jax.ShapeDtypeStruct((2, t_all), F32),
            jax.ShapeDtypeStruct((N_CLASSES, LANES), F32),
        ],
        compiler_params=_params("arbitrary", "arbitrary"),
        name="moe_pre",
    )(x, sh, sc, g_norm, router_w.T, router_bias.reshape(e, 1))


def _moe_rank_kernel(cls_ref, cnt_ref, dest_ref, meta_ref, pstart_sc, run_sc, *, sub, n_sub, nbp):
    nc = cnt_ref.shape[0]

    @pl.when(pl.program_id(0) == 0)
    def _():
        cnt = cnt_ref[...]
        padded = jnp.ceil(cnt / MOE_BLOCK) * MOE_BLOCK
        blk = lax.broadcasted_iota(I32, (1, nbp), 1).astype(F32) * MOE_BLOCK
        running = jnp.zeros((1, LANES), F32)
        block_c = jnp.zeros((1, nbp), F32)
        for c in range(nc):
            pstart_sc[c:c + 1, :] = running
            running = running + padded[c:c + 1, :]
            block_c = block_c + (running[:, 0:1] <= blk).astype(F32)
        run_sc[...] = jnp.zeros(run_sc.shape, F32)
        block_c = jnp.minimum(block_c, nc - 1.0)
        group = jnp.floor(block_c / PAIRS_PER_GROUP)
        pair = block_c - group * PAIRS_PER_GROUP
        lo = (pair >= 3.0).astype(F32) + (pair >= 5.0).astype(F32)
        hi = jnp.where(pair == 0.0, 1.0, jnp.where((pair == 1.0) | (pair == 3.0), 2.0, 3.0))
        n_used = jnp.broadcast_to(running[:, 0:1] / MOE_BLOCK, (1, nbp))
        meta_ref[...] = jnp.concatenate(
            [group * EXPERTS_PER_GROUP + lo, group * EXPERTS_PER_GROUP + hi, n_used], axis=0).astype(I32)

    upper = (lax.broadcasted_iota(I32, (sub, sub), 0) < lax.broadcasted_iota(I32, (sub, sub), 1)).astype(BF16)
    cid = lax.broadcasted_iota(I32, (nc, sub), 0)
    for j in range(n_sub):
        oh = (cid == cls_ref[:, j * sub:(j + 1) * sub]).astype(F32)
        rank = jnp.dot(oh.astype(BF16), upper, preferred_element_type=F32)
        base = pstart_sc[:, 0:1] + run_sc[:, 0:1]
        dest = jnp.sum(oh * (rank + base), axis=0, keepdims=True)
        dest_ref[:, j * sub:(j + 1) * sub] = dest.astype(I32)
        run_sc[...] += jnp.sum(oh, axis=1, keepdims=True)


def _moe_rank(cls, cnt, n_blocks):
    t_all = cls.shape[1]
    sub = 256
    tr = min(2048, t_all)
    nbp = -(-n_blocks // LANES) * LANES
    nc = cnt.shape[0]
    return pl.pallas_call(
        functools.partial(_moe_rank_kernel, sub=sub, n_sub=tr // sub, nbp=nbp),
        grid=(t_all // tr,),
        in_specs=[
            pl.BlockSpec((1, tr), lambda i: (0, i)),
            pl.BlockSpec((nc, LANES), lambda i: (0, 0)),
        ],
        out_specs=[
            pl.BlockSpec((1, tr), lambda i: (0, i)),
            pl.BlockSpec((3, nbp), lambda i: (0, 0)),
        ],
        out_shape=[
            jax.ShapeDtypeStruct((1, t_all), I32),
            jax.ShapeDtypeStruct((3, nbp), I32),
        ],
        scratch_shapes=[pltpu.VMEM((nc, LANES), F32), pltpu.VMEM((nc, LANES), F32)],
        compiler_params=_params("arbitrary"),
        name="moe_rank",
    )(cls, cnt)


def _tile_copy(src_ref, s, dst_ref, d, sem, rows):
    return pltpu.make_async_copy(src_ref.at[pl.ds(pl.multiple_of(s * rows, rows), rows)],
                                 dst_ref.at[pl.ds(pl.multiple_of(d * rows, rows), rows)], sem)


def _moe_dispatch_kernel(dest_hbm, h_ref, xs_in, xs_out, idx_smem, idx_sem, sem, *, tm, pieces):
    del xs_in
    i = pl.program_id(0)
    cp = pltpu.make_async_copy(dest_hbm.at[i], idx_smem, idx_sem)
    cp.start()
    cp.wait()

    def issue(t, carry):
        _tile_copy(h_ref, t, xs_out, idx_smem[0, t], sem, pieces).start()
        return carry

    lax.fori_loop(0, tm, issue, 0, unroll=8)

    def drain(t, carry):
        _tile_copy(h_ref, 0, xs_out, 0, sem, pieces).wait()
        return carry

    lax.fori_loop(0, tm, drain, 0, unroll=8)


def _moe_dispatch(dest_tiles, h, rows, tm, pieces):
    t_all = h.shape[0] // pieces
    zeros = jnp.zeros((rows * pieces, LANES), F32)
    return pl.pallas_call(
        functools.partial(_moe_dispatch_kernel, tm=tm, pieces=pieces),
        grid=(t_all // tm,),
        in_specs=[
            pl.BlockSpec(memory_space=pl.ANY),
            pl.BlockSpec((tm * pieces, LANES), lambda i: (i, 0)),
            pl.BlockSpec(memory_space=pl.ANY),
        ],
        out_specs=pl.BlockSpec(memory_space=pl.ANY),
        out_shape=jax.ShapeDtypeStruct((rows * pieces, LANES), F32),
        scratch_shapes=[pltpu.SMEM((1, tm), I32), pltpu.SemaphoreType.DMA(()), pltpu.SemaphoreType.DMA(())],
        input_output_aliases={2: 0},
        compiler_params=_params("arbitrary"),
        name="moe_dispatch",
    )(dest_tiles, h, zeros)


def _moe_expert_kernel(meta_ref, x_ref, wga_ref, wua_ref, wda_ref, wgb_ref, wub_ref, wdb_ref, y_ref, xb_sc,
                       *, pieces):
    i = pl.program_id(0)
    rows = xb_sc.shape[0]

    @pl.when(i < meta_ref[2, 0])
    def _():
        for s in range(pieces):
            xb_sc[:, s * LANES:(s + 1) * LANES] = x_ref[pl.ds(s, rows, stride=pieces), :].astype(BF16)
        x = xb_sc[...]
        halves = []
        for wg_ref, wu_ref, wd_ref in ((wga_ref, wua_ref, wda_ref), (wgb_ref, wub_ref, wdb_ref)):
            gate = jnp.dot(x, wg_ref[0], preferred_element_type=F32)
            up = jnp.dot(x, wu_ref[0], preferred_element_type=F32)
            hidden = (_silu(gate) * up).astype(BF16)
            y = jnp.dot(hidden, wd_ref[0], preferred_element_type=F32)
            halves.append(lax.bitcast_convert_type(y.astype(BF16).astype(F32), U32))
        word = halves[0] | (halves[1] >> 16)
        for s in range(pieces):
            y_ref[pl.ds(s, rows, stride=pieces), :] = word[:, s * LANES:(s + 1) * LANES]

    @pl.when(i >= meta_ref[2, 0])
    def _():
        y_ref[...] = jnp.zeros(y_ref.shape, U32)


def _moe_experts(meta, xs, w_gate, w_up, w_down, n_blocks, pieces):
    d, f = w_gate.shape[1], w_gate.shape[2]
    wspec = lambda k, a, c: pl.BlockSpec((1, a, c), lambda i, meta: (meta[k, i], 0, 0))
    return pl.pallas_call(
        functools.partial(_moe_expert_kernel, pieces=pieces),
        grid_spec=pltpu.PrefetchScalarGridSpec(
            num_scalar_prefetch=1,
            grid=(n_blocks,),
            in_specs=[pl.BlockSpec((MOE_BLOCK * pieces, LANES), lambda i, meta: (i, 0)),
                      wspec(0, d, f), wspec(0, d, f), wspec(0, f, d),
                      wspec(1, d, f), wspec(1, d, f), wspec(1, f, d)],
            out_specs=pl.BlockSpec((MOE_BLOCK * pieces, LANES), lambda i, meta: (i, 0)),
            scratch_shapes=[pltpu.VMEM((MOE_BLOCK, d), BF16)],
        ),
        out_shape=jax.ShapeDtypeStruct(xs.shape, U32),
        compiler_params=_params("arbitrary"),
        name="moe_experts",
    )(meta, xs, w_gate, w_up, w_down, w_gate, w_up, w_down)


def _moe_combine_kernel(dest_hbm, y_hbm, w_ref, x_ref, g_ref, o_ref, y_sc, idx_smem, idx_sem, sem,
                        *, tm, nt, pieces):
    tile = pl.program_id(0) * nt + pl.program_id(1)
    cp = pltpu.make_async_copy(dest_hbm.at[tile], idx_smem, idx_sem)
    cp.start()
    cp.wait()

    def issue(t, carry):
        _tile_copy(y_hbm, idx_smem[0, t], y_sc, t, sem, pieces).start()
        return carry

    lax.fori_loop(0, tm, issue, 0, unroll=8)

    def drain(t, carry):
        _tile_copy(y_hbm, 0, y_sc, 0, sem, pieces).wait()
        return carry

    lax.fori_loop(0, tm, drain, 0, unroll=8)
    w = w_ref[...]
    w_lo, w_hi = w[:, 0:1], w[:, 1:2]
    for s in range(pieces):
        cols = slice(s * LANES, (s + 1) * LANES)
        word = y_sc[pl.ds(s, tm, stride=pieces), :]
        y_lo = lax.bitcast_convert_type(word & jnp.uint32(0xFFFF0000), F32)
        y_hi = lax.bitcast_convert_type(word << 16, F32)
        moe = w_lo * y_lo + w_hi * y_hi
        o_ref[0, :, cols] = x_ref[0, :, cols] + g_ref[0, :, cols] * moe


def _moe_combine(dest_tiles, y, w_col, x, gate, tm, pieces):
    b, s, d = x.shape
    nt = s // tm
    return pl.pallas_call(
        functools.partial(_moe_combine_kernel, tm=tm, nt=nt, pieces=pieces),
        grid=(b, nt),
        in_specs=[
            pl.BlockSpec(memory_space=pl.ANY),
            pl.BlockSpec(memory_space=pl.ANY),
            pl.BlockSpec((tm, 2), lambda bi, i: (bi * nt + i, 0)),
            pl.BlockSpec((1, tm, d), lambda bi, i: (bi, i, 0)),
            pl.BlockSpec((1, 1, d), lambda bi, i: (bi, 0, 0)),
        ],
        out_specs=pl.BlockSpec((1, tm, d), lambda bi, i: (bi, i, 0)),
        out_shape=jax.ShapeDtypeStruct((b, s, d), F32),
        scratch_shapes=[
            pltpu.VMEM((tm * pieces, LANES), U32),
            pltpu.SMEM((1, tm), I32),
            pltpu.SemaphoreType.DMA(()),
            pltpu.SemaphoreType.DMA(()),
        ],
        compiler_params=_params("arbitrary", "arbitrary"),
        name="moe_combine",
    )(dest_tiles, y, w_col, x, gate)


def _moe_layer(x, sh, sc, gate, g_norm, router_w, router_bias, w_gate, w_up, w_down):
    b, s, d = x.shape
    t_all = b * s
    pieces = d // LANES
    tm = min(512, s)
    n_blocks = -(-t_all // MOE_BLOCK) + N_CLASSES
    rows = n_blocks * MOE_BLOCK
    h, cls, w, cnt = _moe_pre(x, sh, sc, g_norm, router_w, router_bias)
    dest, meta = _moe_rank(cls, cnt, n_blocks)
    dest_tiles = dest.reshape(t_all // tm, 1, tm)
    xs = _moe_dispatch(dest_tiles, h, rows, tm, pieces)
    y = _moe_experts(meta, xs, w_gate.astype(BF16), w_up.astype(BF16), w_down.astype(BF16), n_blocks, pieces)
    return _moe_combine(dest_tiles, y, w.T, x, gate, tm, pieces)


def _final_norm_kernel(x_ref, g_ref, o_ref):
    x = x_ref[0]
    ms = jnp.mean(x * x, axis=-1, keepdims=True)
    o_ref[0] = x * lax.rsqrt(ms + NORM_EPS) * g_ref[...]


def _final_norm(x, g, tm=1024):
    b, s, d = x.shape
    tm = min(tm, s)
    return pl.pallas_call(
        _final_norm_kernel,
        grid=(b, s // tm),
        in_specs=[
            pl.BlockSpec((1, tm, d), lambda bi, i: (bi, i, 0)),
            pl.BlockSpec((1, d), lambda bi, i: (0, 0)),
        ],
        out_specs=pl.BlockSpec((1, tm, d), lambda bi, i: (bi, i, 0)),
        out_shape=jax.ShapeDtypeStruct((b, s, d), F32),
        compiler_params=_params("parallel", "arbitrary"),
        name="final_norm",
    )(x, g.reshape(1, d))


def kernel(x, c, ada_w, ada_b, norm1_g, norm2_g, router_w, router_bias, moe_w_gate, moe_w_up, moe_w_down, dil_w_in, dil_w_out, diff_w_in, diff_lam_q1, diff_lam_k1, diff_lam_q2, diff_lam_k2, diff_head_norm_g, diff_w_out, ssm_w_in, ssm_conv_w, ssm_conv_b, ssm_dt_bias, ssm_A_log, ssm_D, ssm_norm_g, ssm_w_out, final_norm_g):
    b, s, d = x.shape
    depth = ada_w.shape[0]
    mod = _ada_mod(c, ada_w, ada_b).reshape(depth, b, 6, 1, d)
    tables = _rope_tables(s)
    for i in range(depth):
        sh1, sc1, g1, sh2, sc2, g2 = (mod[i, :, j] for j in range(6))
        n1 = norm1_g[i].reshape(1, d)
        kind, j = i % 3, i // 3
        if kind == 0:
            x = _dilated_layer(x, sh1, sc1, g1, n1, dil_w_in[j], dil_w_out[j], tables)
        elif kind == 1:
            x = _diff_layer(x, sh1, sc1, g1, n1, diff_w_in[j], diff_lam_q1[j], diff_lam_k1[j],
                            diff_lam_q2[j], diff_lam_k2[j], diff_head_norm_g[j], diff_w_out[j], tables, i)
        else:
            x = _ssd_layer(x, sh1, sc1, g1, n1, ssm_w_in[j], ssm_conv_w[j], ssm_conv_b[j], ssm_dt_bias[j],
                           ssm_A_log[j], ssm_D[j], ssm_norm_g[j], ssm_w_out[j])
        x = _moe_layer(x, sh2, sc2, g2, norm2_g[i].reshape(1, d), router_w, router_bias,
                       moe_w_gate[i], moe_w_up[i], moe_w_down[i])
    return _final_norm(x, final_norm_g)
```

```python
import functools
import math

import jax
import jax.numpy as jnp
from jax import lax
from jax.experimental import pallas as pl
from jax.experimental.pallas import tpu as pltpu

F32 = jnp.float32
BF16 = jnp.bfloat16
I32 = jnp.int32
U32 = jnp.uint32
HIGHEST = lax.Precision.HIGHEST

LANES = 128
SUBLANES = 8
VMEM_LIMIT_BYTES = 56 * 1024 * 1024

NORM_EPS = 1e-6
ROPE_THETA = 500000.0
ROPE_FRACTION = 4
HEAD_DIM = 64
ATTN_BLOCK = 128
DIL_CONFIGS = ((128, 1), (512, 4), (2048, 16))
DIFF_NORM_EPS = 1e-5
SSM_HEAD_DIM = 64
SSM_GROUPS = 4
SSM_STATE = 128
SSM_CONV = 4
SSM_CHUNK = 256
N_EXPERTS = 16
N_EXPERT_GROUPS = 4
EXPERTS_PER_GROUP = 4
PAIRS_PER_GROUP = 6
N_CLASSES = N_EXPERT_GROUPS * PAIRS_PER_GROUP
MOE_BLOCK = 256
COL_CHUNK = 512
NEG = -1e30

_NT = (((1,), (1,)), ((), ()))
_TN = (((0,), (0,)), ((), ()))


def _params(*sem):
    return pltpu.CompilerParams(dimension_semantics=sem, vmem_limit_bytes=VMEM_LIMIT_BYTES)


def _silu(v):
    return v * (0.5 * jnp.tanh(0.5 * v) + 0.5)


def _pack_bf16_pair(a, b):
    hi = lax.bitcast_convert_type(a.astype(BF16).astype(F32), U32)
    lo = lax.bitcast_convert_type(b.astype(BF16).astype(F32), U32)
    return hi | (lo >> 16)


def _unpack_bf16_pair(word):
    return (lax.bitcast_convert_type(word & jnp.uint32(0xFFFF0000), F32),
            lax.bitcast_convert_type(word << 16, F32))


def _ada_kernel(c_ref, w_ref, b_ref, o_ref):
    cond = _silu(c_ref[...])
    o_ref[0] = jnp.dot(cond, w_ref[0], preferred_element_type=F32, precision=HIGHEST) + b_ref[0]


def _ada_mod(c, ada_w, ada_b):
    depth, d, n = ada_w.shape
    b = c.shape[0]
    tn = 1536
    return pl.pallas_call(
        _ada_kernel,
        grid=(depth, n // tn),
        in_specs=[
            pl.BlockSpec((b, d), lambda i, j: (0, 0)),
            pl.BlockSpec((1, d, tn), lambda i, j: (i, 0, j)),
            pl.BlockSpec((1, 1, tn), lambda i, j: (i, 0, j)),
        ],
        out_specs=pl.BlockSpec((1, b, tn), lambda i, j: (i, 0, j)),
        out_shape=jax.ShapeDtypeStruct((depth, b, n), F32),
        compiler_params=_params("arbitrary", "arbitrary"),
        name="ada_mod",
    )(c, ada_w, ada_b.reshape(depth, 1, n))


def _rope_tables(seq):
    r = HEAD_DIM // ROPE_FRACTION
    half = r // 2
    inv = jnp.power(ROPE_THETA, -jnp.arange(half, dtype=F32) * 2.0 / r)
    ang = jnp.arange(seq, dtype=F32)[:, None] * inv[None, :]
    cos, sin = jnp.cos(ang), jnp.sin(ang)
    ones = jnp.ones((seq, HEAD_DIM - r), F32)
    zeros = jnp.zeros((seq, HEAD_DIM - r), F32)
    zh = jnp.zeros((seq, half), F32)
    cos_t = jnp.concatenate([cos, cos, ones], axis=1)
    sin_a = jnp.concatenate([zh, sin, zeros], axis=1)
    sin_b = jnp.concatenate([-sin, zh, zeros], axis=1)
    rep = LANES // HEAD_DIM
    lane_form = tuple(jnp.tile(t, (1, rep)) for t in (cos_t, sin_a, sin_b))
    return lane_form, tuple(t.T for t in lane_form)


def _modulated_norm(x, g, sc, sh):
    ms = jnp.mean(x * x, axis=-1, keepdims=True)
    return x * lax.rsqrt(ms + NORM_EPS) * g * (1.0 + sc) + sh


def _rope(a, cos, sa, sb, scale, axis):
    half = HEAD_DIM // ROPE_FRACTION // 2
    r = a * cos + pltpu.roll(a, half, axis) * sa + pltpu.roll(a, LANES - half, axis) * sb
    return r if scale == 1.0 else r * scale


def _proj_kernel(*refs, chunks, tplan, n_out, use_rope, has_perm):
    x_ref, sh_ref, sc_ref, g_ref, w_ref = refs[:5]
    pos = 5
    if use_rope:
        cos_ref, sa_ref, sb_ref = refs[pos:pos + 3]
        pos += 3
    if tplan is not None:
        wt_ref, cost_ref, sat_ref, sbt_ref = refs[pos:pos + 4]
        pos += 4
    out_refs = refs[pos:pos + n_out]
    perm_sc = refs[pos + n_out] if has_perm else None
    tm = x_ref.shape[1]
    h = _modulated_norm(x_ref[0], g_ref[...], sc_ref[0], sh_ref[0]).astype(BF16)
    tiles_per_chunk = COL_CHUNK // LANES
    for c, (oi, off, scale, dil) in enumerate(chunks):
        o_ref = out_refs[oi]
        acc = jnp.dot(h, w_ref[:, c * COL_CHUNK:(c + 1) * COL_CHUNK], preferred_element_type=F32)
        tiles = None
        if scale is not None:
            cos, sa, sb = cos_ref[...], sa_ref[...], sb_ref[...]
            tiles = [_rope(acc[:, s * LANES:(s + 1) * LANES], cos, sa, sb, scale, 1)
                     for s in range(tiles_per_chunk)]
        if dil == 1:
            if tiles is None:
                o_ref[0, :, off:off + COL_CHUNK] = acc.astype(o_ref.dtype)
            else:
                for s, tl in enumerate(tiles):
                    o_ref[0, :, off + s * LANES:off + (s + 1) * LANES] = tl.astype(o_ref.dtype)
            continue
        if tiles is None:
            tiles = [acc[:, s * LANES:(s + 1) * LANES] for s in range(tiles_per_chunk)]
        for s, tl in enumerate(tiles):
            perm_sc[s] = tl
        for rho in range(dil):
            for s in range(tiles_per_chunk):
                o_ref[0, rho, :, off + s * LANES:off + (s + 1) * LANES] = (
                    perm_sc[s, pl.ds(rho, tm // dil, stride=dil), :].astype(o_ref.dtype))
    if tplan is not None:
        oi, scale = tplan
        o_ref = out_refs[oi]
        acct = lax.dot_general(wt_ref[...], h, _NT, preferred_element_type=F32)
        cost, sat, sbt = cost_ref[...], sat_ref[...], sbt_ref[...]
        for s in range(acct.shape[0] // LANES):
            rows = slice(s * LANES, (s + 1) * LANES)
            o_ref[0, rows, :] = _rope(acct[rows, :], cost, sat, sbt, scale, 0).astype(o_ref.dtype)


def _proj(x, sh, sc, g, w, chunks, outs, tables, wt=None, tplan=None, tm=512):
    b, s, d = x.shape
    n = w.shape[1]
    tm = min(tm, s)
    use_rope = any(c[2] is not None for c in chunks)
    has_perm = any(c[3] > 1 for c in chunks)
    in_specs = [
        pl.BlockSpec((1, tm, d), lambda bi, i: (bi, i, 0)),
        pl.BlockSpec((1, 1, d), lambda bi, i: (bi, 0, 0)),
        pl.BlockSpec((1, 1, d), lambda bi, i: (bi, 0, 0)),
        pl.BlockSpec((1, d), lambda bi, i: (0, 0)),
        pl.BlockSpec((d, n), lambda bi, i: (0, 0), pipeline_mode=pl.Buffered(1)),
    ]
    args = [x, sh, sc, g, w]
    if use_rope:
        in_specs += [pl.BlockSpec((tm, LANES), lambda bi, i: (i, 0))] * 3
        args += list(tables[0])
    if tplan is not None:
        in_specs.append(pl.BlockSpec(wt.shape, lambda bi, i: (0, 0), pipeline_mode=pl.Buffered(1)))
        in_specs += [pl.BlockSpec((LANES, tm), lambda bi, i: (0, i))] * 3
        args += [wt] + list(tables[1])
    out_specs, out_shape = [], []
    for layout, wd, dt, dil in outs:
        if layout == "tok":
            out_specs.append(pl.BlockSpec((1, tm, wd), lambda bi, i: (bi, i, 0)))
            out_shape.append(jax.ShapeDtypeStruct((b, s, wd), dt))
        elif layout == "res":
            out_specs.append(pl.BlockSpec((1, dil, tm // dil, wd), lambda bi, i: (bi, 0, i, 0)))
            out_shape.append(jax.ShapeDtypeStruct((b, dil, s // dil, wd), dt))
        else:
            out_specs.append(pl.BlockSpec((1, wd, tm), lambda bi, i: (bi, 0, i)))
            out_shape.append(jax.ShapeDtypeStruct((b, wd, s), dt))
    return pl.pallas_call(
        functools.partial(_proj_kernel, chunks=tuple(chunks), tplan=tplan, n_out=len(outs),
                          use_rope=use_rope, has_perm=has_perm),
        grid=(b, s // tm),
        in_specs=in_specs,
        out_specs=out_specs,
        out_shape=out_shape,
        scratch_shapes=[pltpu.VMEM((COL_CHUNK // LANES, tm, LANES), F32)] if has_perm else [],
        compiler_params=_params("parallel", "arbitrary"),
        name="norm_proj",
    )(*args)


def _outproj_kernel(y_ref, w_ref, x_ref, g_ref, o_ref):
    y = jnp.dot(y_ref[0], w_ref[...], preferred_element_type=F32)
    o_ref[0] = x_ref[0] + g_ref[0] * y


def _outproj(y, w, x, gate, tm=512):
    b, s, d = x.shape
    k = y.shape[-1]
    tm = min(tm, s)
    return pl.pallas_call(
        _outproj_kernel,
        grid=(b, s // tm),
        in_specs=[
            pl.BlockSpec((1, tm, k), lambda bi, i: (bi, i, 0)),
            pl.BlockSpec((k, d), lambda bi, i: (0, 0), pipeline_mode=pl.Buffered(1)),
            pl.BlockSpec((1, tm, d), lambda bi, i: (bi, i, 0)),
            pl.BlockSpec((1, 1, d), lambda bi, i: (bi, 0, 0)),
        ],
        out_specs=pl.BlockSpec((1, tm, d), lambda bi, i: (bi, i, 0)),
        out_shape=jax.ShapeDtypeStruct((b, s, d), F32),
        compiler_params=_params("parallel", "arbitrary"),
        name="out_proj",
    )(y, w, x, gate)


def _dil_kernel(q_ref, kc_ref, kp_ref, vc_ref, vp_ref, o_ref, lse_ref, kbuf, vbuf, *, tq, back, heads):
    n = pl.program_id(2)
    qb = ATTN_BLOCK
    kbuf[0:qb] = kp_ref[0, 0]
    kbuf[qb:] = kc_ref[0, 0]
    vbuf[0:qb] = vp_ref[0, 0]
    vbuf[qb:] = vc_ref[0, 0]
    rows = heads * qb
    qi = lax.broadcasted_iota(I32, (rows, 2 * qb), 0) & (qb - 1)
    kj = lax.broadcasted_iota(I32, (rows, 2 * qb), 1)
    rel = kj - qi
    band = (rel >= qb - back) & (rel <= qb)
    lane = lax.broadcasted_iota(I32, (qb, LANES), 1)
    low_half = lane < HEAD_DIM
    zero_q = jnp.zeros((qb, LANES), BF16)
    ones_k = jnp.ones((2 * qb, LANES), BF16)
    for j in range(tq // qb):
        first_key = n * tq + (j - 1) * qb
        valid = band & (kj + first_key >= 0)
        parts = []
        for hp in range(heads // 2):
            cols = slice(hp * LANES, (hp + 1) * LANES)
            q2 = q_ref[0, 0, j * qb:(j + 1) * qb, cols]
            q_stack = jnp.concatenate([jnp.where(low_half, q2, zero_q), jnp.where(low_half, zero_q, q2)], axis=0)
            parts.append(lax.dot_general(q_stack, kbuf[j * qb:(j + 2) * qb, cols], _NT,
                                         preferred_element_type=F32))
        s = jnp.where(valid, jnp.concatenate(parts, axis=0), NEG)
        m = jnp.max(s, axis=-1, keepdims=True)
        pb = jnp.exp2(s - m).astype(BF16)
        l = jnp.dot(pb, ones_k, preferred_element_type=F32)
        inv = 1.0 / l
        lse = (jnp.broadcast_to(m, l.shape) + jnp.log2(l)) * math.log(2.0)
        lse_tile = jnp.zeros((qb, LANES), F32)
        for hp in range(heads // 2):
            cols = slice(hp * LANES, (hp + 1) * LANES)
            ra = slice(2 * hp * qb, (2 * hp + 1) * qb)
            rb = slice((2 * hp + 1) * qb, (2 * hp + 2) * qb)
            pv = jnp.dot(pb[2 * hp * qb:(2 * hp + 2) * qb], vbuf[j * qb:(j + 2) * qb, cols],
                         preferred_element_type=F32)
            o = jnp.where(low_half, pv[:qb] * inv[ra], pv[qb:] * inv[rb])
            o_ref[0, 0, j * qb:(j + 1) * qb, cols] = o.astype(o_ref.dtype)
            lse_tile = jnp.where(lane == 2 * hp, lse[ra], lse_tile)
            lse_tile = jnp.where(lane == 2 * hp + 1, lse[rb], lse_tile)
        lse_ref[0, 0, j * qb:(j + 1) * qb, :] = lse_tile


def _dil_group(proj, window, dilation):
    b, dil, ln, c = proj.shape
    back = window // dilation
    assert dil == dilation and back <= ATTN_BLOCK and ln % ATTN_BLOCK == 0
    width = c // 3
    heads = width // HEAD_DIM
    tq = min(512, ln)
    sub = tq // ATTN_BLOCK

    def cur(col):
        return pl.BlockSpec((1, 1, tq, width), lambda bi, r, n: (bi, r, n, col))

    def prev(col):
        return pl.BlockSpec((1, 1, ATTN_BLOCK, width),
                            lambda bi, r, n: (bi, r, jnp.maximum(n * sub - 1, 0), col))

    return pl.pallas_call(
        functools.partial(_dil_kernel, tq=tq, back=back, heads=heads),
        grid=(b, dil, ln // tq),
        in_specs=[cur(0), cur(1), prev(1), cur(2), prev(2)],
        out_specs=[
            pl.BlockSpec((1, 1, tq, width), lambda bi, r, n: (bi, r, n, 0)),
            pl.BlockSpec((1, 1, tq, LANES), lambda bi, r, n: (bi, r, n, 0)),
        ],
        out_shape=[
            jax.ShapeDtypeStruct((b, dil, ln, width), BF16),
            jax.ShapeDtypeStruct((b, dil, ln, LANES), F32),
        ],
        scratch_shapes=[pltpu.VMEM((tq + ATTN_BLOCK, width), BF16)] * 2,
        compiler_params=_params("parallel", "parallel", "arbitrary"),
        name=f"dil_attn_d{dilation}",
    )(proj, proj, proj, proj, proj)


def _dil_out_kernel(o0_ref, o1_ref, o2_ref, l0_ref, l1_ref, l2_ref, w_ref, x_ref, g_ref, out_ref,
                    o_sc, l1_sc, l2_sc):
    tm = x_ref.shape[1]

    def to_token_order(src_ref, dst_sc):
        dil = src_ref.shape[1]
        for rho in range(dil):
            src = src_ref[0, rho].astype(F32)
            for s in range(dst_sc.shape[0]):
                dst_sc[s, pl.ds(rho, tm // dil, stride=dil), :] = src[:, s * LANES:(s + 1) * LANES]
        return jnp.concatenate([dst_sc[s] for s in range(dst_sc.shape[0])], axis=1)

    ls = [l0_ref[0, 0], to_token_order(l1_ref, l1_sc), to_token_order(l2_ref, l2_sc)]
    mx =jnp.maximum(jnp.maximum(ls[0], ls[1]), ls[2])
    es = [jnp.exp(v - mx) for v in ls]
    inv = 1.0 / (es[0] + es[1] + es[2])
    width = o0_ref.shape[-1]
    expand = (lax.broadcasted_iota(I32, (LANES, width), 0)
              == lax.broadcasted_iota(I32, (LANES, width), 1) // HEAD_DIM).astype(BF16)
    o = jnp.zeros((tm, width), F32)
    for gi, (e, o_ref) in enumerate(zip(es, (o0_ref, o1_ref, o2_ref))):
        alpha = e * inv
        hi = alpha.astype(BF16)
        lo = (alpha - hi.astype(F32)).astype(BF16)
        a_full = (jnp.dot(hi, expand, preferred_element_type=F32)
                  + jnp.dot(lo, expand, preferred_element_type=F32))
        og = o_ref[0, 0].astype(F32) if gi == 0 else to_token_order(o_ref, o_sc)
        o = o + a_full * og
    y = jnp.dot(o.astype(BF16), w_ref[...], preferred_element_type=F32)
    out_ref[0] = x_ref[0] + g_ref[0] * y


def _dil_out(os_, lses, w, x, gate, tm=512):
    b, s, d = x.shape
    width = os_[0].shape[-1]
    tm = min(tm, s)
    tok = lambda wd: pl.BlockSpec((1, tm, wd), lambda bi, i: (bi, i, 0))
    res = lambda a: pl.BlockSpec((1, a.shape[1], tm // a.shape[1], a.shape[3]), lambda bi, i: (bi, 0, i, 0))
    return pl.pallas_call(
        _dil_out_kernel,
        grid=(b, s // tm),
        in_specs=[res(a) for a in os_] + [res(a) for a in lses] + [
            pl.BlockSpec((width, d), lambda bi, i: (0, 0)),
            tok(d),
            pl.BlockSpec((1, 1, d), lambda bi, i: (bi, 0, 0)),
        ],
        out_specs=tok(d),
        out_shape=jax.ShapeDtypeStruct((b, s, d), F32),
        scratch_shapes=[pltpu.VMEM((width // LANES, tm, LANES), F32), pltpu.VMEM((1, tm, LANES), F32),
                        pltpu.VMEM((1, tm, LANES), F32)],
        compiler_params=_params("parallel", "arbitrary"),
        name="dil_out",
    )(*os_, *lses, w, x, gate)


def _dilated_layer(x, sh, sc, gate, g_norm, w_in, w_out, tables):
    b, s, _ = x.shape
    n = w_in.shape[1]
    gw = n // len(DIL_CONFIGS)
    chunks, outs = [], []
    for g, (_, dilation) in enumerate(DIL_CONFIGS):
        for kind in range(3):
            scale = (HEAD_DIM ** -0.5 * math.log2(math.e), 1.0, None)[kind]
            chunks.append((g, kind * COL_CHUNK, scale, dilation))
        outs.append(("tok" if dilation == 1 else "res", gw, BF16, dilation))
    projs = _proj(x, sh, sc, g_norm, w_in.astype(BF16), chunks, outs, tables)
    os_, lses = [], []
    for proj, (window, dilation) in zip(projs, DIL_CONFIGS):
        o, lse = _dil_group(proj.reshape(b, dilation, s // dilation, gw), window, dilation)
        os_.append(o)
        lses.append(lse)
    return _dil_out(os_, lses, w_out.astype(BF16), x, gate)


def _diff_kernel(qi_tab, ki_tab, q_ref, kt_ref, v_ref, lq1_ref, lk1_ref, lq2_ref, lk2_ref, hg_ref, o_ref,
                 m_sc, acc_sc, *, t, r, lam_init):
    pair = pl.program_id(2)
    qi = qi_tab[pair]
    ki = ki_tab[pair]
    vw = v_ref.shape[-1]

    @pl.when(ki == 0)
    def _():
        m_sc[...] = jnp.full(m_sc.shape, NEG, F32)
        acc_sc[...] = jnp.zeros(acc_sc.shape, F32)

    def step(diagonal, r):
        kt = kt_ref[0]
        v_aug = jnp.concatenate([v_ref[0], jnp.ones((t, LANES), BF16)], axis=1)
        lane = lax.broadcasted_iota(I32, (r, vw), 1)
        work = [(mi, c) for mi in range(2) for c in range(t // r)]

        def scores(mi, c):
            in_half = (lane >= mi * HEAD_DIM) & (lane < (mi + 1) * HEAD_DIM)
            ncols = (c + 1) * r if diagonal else t
            q_c = jnp.where(in_half, q_ref[0, c * r:(c + 1) * r, :], jnp.zeros((r, vw), BF16))
            return jnp.dot(q_c, kt[:, :ncols], preferred_element_type=F32)

        ahead = 3
        pending = [scores(*wk) for wk in work[:ahead]]
        for idx, (mi, c) in enumerate(work):
            s = pending.pop(0)
            if idx + ahead < len(work):
                pending.append(scores(*work[idx + ahead]))
            ncols = s.shape[1]
            if diagonal:
                col = lax.broadcasted_iota(I32, (r, ncols), 1)
                row = lax.broadcasted_iota(I32, (r, ncols), 0) + c * r
                s = jnp.where(col <= row, s, NEG)
            tiles = [s[:, j * LANES:(j + 1) * LANES] for j in range(ncols // LANES)]
            tmax = functools.reduce(jnp.maximum, tiles)
            srows = slice(mi * t + c * r, mi * t + (c + 1) * r)
            m_prev = m_sc[srows, :]
            m_new = jnp.maximum(m_prev, jnp.max(tmax, axis=-1, keepdims=True))
            alpha = jnp.exp2(m_prev - m_new)
            p = jnp.concatenate([jnp.exp2(tl - m_new) for tl in tiles], axis=1).astype(BF16)
            pv = jnp.dot(p, v_aug[:ncols], preferred_element_type=F32)
            acc_sc[srows, :] = jnp.concatenate([alpha, alpha], axis=1) * acc_sc[srows, :] + pv
            m_sc[srows, :] = m_new

    @pl.when(ki < qi)
    def _():
        step(False, 2 * r)

    @pl.when(ki == qi)
    def _():
        step(True, 2 * r)
        lam = (jnp.exp(jnp.sum(lq1_ref[...] * lk1_ref[...], axis=-1, keepdims=True))
               - jnp.exp(jnp.sum(lq2_ref[...] * lk2_ref[...], axis=-1, keepdims=True)) + lam_init)
        o = acc_sc[:t, :vw] / acc_sc[:t, vw:] - lam * (acc_sc[t:, :vw] / acc_sc[t:, vw:])
        ms = jnp.mean(o * o, axis=-1, keepdims=True)
        o = o * lax.rsqrt(ms + DIFF_NORM_EPS) * hg_ref[...] * (1.0 - lam_init)
        o_ref[0] = o.astype(o_ref.dtype)


def _diff_attention(q, kt, v, lam_q1, lam_k1, lam_q2, lam_k2, head_g, lam_init, t=1024, r=128):
    b, s, d = q.shape
    vw = 2 * HEAD_DIM
    assert vw == LANES
    heads = d // vw
    t = min(t, s)
    r = min(r, t // 2)
    nq = s // t
    pairs = [(qi, ki) for qi in range(nq) for ki in range(qi + 1)]
    qi_tab = jnp.asarray([p[0] for p in pairs], I32)
    ki_tab = jnp.asarray([p[1] for p in pairs], I32)
    vec = lambda n: pl.BlockSpec((1, n), lambda bi, h, p, qt, kt_: (0, 0))
    return pl.pallas_call(
        functools.partial(_diff_kernel, t=t, r=r, lam_init=lam_init),
        grid_spec=pltpu.PrefetchScalarGridSpec(
            num_scalar_prefetch=2,
            grid=(b, heads, len(pairs)),
            in_specs=[
                pl.BlockSpec((1, t, vw), lambda bi, h, p, qt, kt_: (bi, qt[p], h)),
                pl.BlockSpec((1, vw, t), lambda bi, h, p, qt, kt_: (bi, h, kt_[p])),
                pl.BlockSpec((1, t, vw), lambda bi, h, p, qt, kt_: (bi, kt_[p], h)),
                vec(HEAD_DIM), vec(HEAD_DIM), vec(HEAD_DIM), vec(HEAD_DIM), vec(vw),
            ],
            out_specs=pl.BlockSpec((1, t, vw), lambda bi, h, p, qt, kt_: (bi, qt[p], h)),
            scratch_shapes=[
                pltpu.VMEM((2 * t, LANES), F32),
                pltpu.VMEM((2 * t, vw + LANES), F32),
            ],
        ),
        out_shape=jax.ShapeDtypeStruct((b, s, d), BF16),
        compiler_params=_params("parallel", "parallel", "arbitrary"),
        name="diff_attn",
    )(qi_tab, ki_tab, q, kt, v, lam_q1.reshape(1, -1), lam_k1.reshape(1, -1), lam_q2.reshape(1, -1),
      lam_k2.reshape(1, -1), head_g.reshape(1, -1))


def _diff_layer(x, sh, sc, gate, g_norm, w_in, lam_q1, lam_k1, lam_q2, lam_k2, head_g, w_out,
                tables, layer_idx):
    d = w_in.shape[1] // 3
    nch = d // COL_CHUNK
    q_scale = HEAD_DIM ** -0.5 * math.log2(math.e)
    chunks = ([(0, c * COL_CHUNK, q_scale, 1) for c in range(nch)]
              + [(1, c * COL_CHUNK, None, 1) for c in range(nch)])
    w_qv = jnp.concatenate([w_in[:, :d], w_in[:, 2 * d:]], axis=1).astype(BF16)
    w_kt = w_in[:, d:2 * d].T.astype(BF16)
    q, v, kt = _proj(x, sh, sc, g_norm, w_qv, chunks,
                     [("tok", d, BF16, 1), ("tok", d, BF16, 1), ("T", d, BF16, 1)], tables,
                     wt=w_kt, tplan=(2, 1.0))
    lam_init = 0.8 - 0.6 * math.exp(-0.3 * layer_idx)
    o = _diff_attention(q, kt, v, lam_q1, lam_k1, lam_q2, lam_k2, head_g, lam_init)
    return _outproj(o, w_out.astype(BF16), x, gate)


def _causal_conv_silu(cur_ref, ext_sc, w_ref, b_ref):
    cur = cur_ref[0].astype(F32)
    rows = cur.shape[0]
    ext_sc[SUBLANES:, :] = cur
    w = w_ref[...]
    acc = cur * w[SSM_CONV - 1:SSM_CONV] + b_ref[...]
    for k in range(1, SSM_CONV):
        acc = acc + ext_sc[SUBLANES - k:SUBLANES - k + rows, :] * w[SSM_CONV - 1 - k:SSM_CONV - k]
    ext_sc[0:SUBLANES, :] = cur[rows - SUBLANES:rows]
    return _silu(acc)


def _ssd_kernel(z_ref, x_ref, b_ref, c_ref, dt_ref, wx_ref, wb_ref, wc_ref, bx_ref, bb_ref, bc_ref,
                dtb_ref, alog_ref, dskip_ref, ng_ref, y_ref,
                state_sc, tx_sc, tb_sc, tc_sc, *, hpg):
    ci = pl.program_id(2)
    L = x_ref.shape[1]
    P = SSM_HEAD_DIM
    gw = hpg * P

    @pl.when(ci == 0)
    def _():
        state_sc[...] = jnp.zeros(state_sc.shape, F32)
        tx_sc[0:SUBLANES, :] = jnp.zeros((SUBLANES, tx_sc.shape[1]), F32)
        tb_sc[0:SUBLANES, :] = jnp.zeros((SUBLANES, tb_sc.shape[1]), F32)
        tc_sc[0:SUBLANES, :] = jnp.zeros((SUBLANES, tc_sc.shape[1]), F32)

    xs = _causal_conv_silu(x_ref, tx_sc, wx_ref, bx_ref)
    bm = _causal_conv_silu(b_ref, tb_sc, wb_ref, bb_ref).astype(BF16)
    cm = _causal_conv_silu(c_ref, tc_sc, wc_ref, bc_ref).astype(BF16)

    raw = dt_ref[0] + dtb_ref[...]
    dt = jnp.maximum(raw, 0.0) + jnp.log(1.0 + jnp.exp(-jnp.abs(raw)))
    a = -jnp.exp(alog_ref[...])
    da = dt * a
    ti = lax.broadcasted_iota(I32, (L, L), 0)
    si = lax.broadcasted_iota(I32, (L, L), 1)
    causal = ti >= si
    acum = jnp.dot(causal.astype(F32), da, preferred_element_type=F32, precision=HIGHEST)
    acum2 = acum * math.log2(math.e)
    acum2_t = acum2.T
    last = acum[L - 1:L, :]
    H = L // 2
    tri = causal[:H, :H]
    expand = (lax.broadcasted_iota(I32, (LANES, gw), 0)
              == lax.broadcasted_iota(I32, (LANES, gw), 1) // P).astype(BF16)

    def per_head_lanes(a):
        hi = a.astype(BF16)
        lo = (a - hi.astype(F32)).astype(BF16)
        return (jnp.dot(hi, expand, preferred_element_type=F32)
                + jnp.dot(lo, expand, preferred_element_type=F32))

    dx = per_head_lanes(dt) * xs
    dxb = dx.astype(BF16)
    eacum_e = per_head_lanes(jnp.exp(acum))
    wdx = (per_head_lanes(jnp.exp(last - acum)) * dx).astype(BF16)

    cb = lax.dot_general(cm, bm, _NT, preferred_element_type=F32)
    y_inter = jnp.dot(cm, state_sc[...].astype(BF16), preferred_element_type=F32)

    low_half = lax.broadcasted_iota(I32, (L, LANES), 1) < P
    y_pairs = []
    for hp in range(hpg // 2):
        cols = slice(hp * LANES, (hp + 1) * LANES)
        ys = []
        for k in (2 * hp, 2 * hp + 1):
            col = acum2[:, k:k + 1]
            row = acum2_t[k:k + 1, :]
            m00 = cb[:H, :H] * jnp.exp2(jnp.where(tri, col[:H] - row[:, :H], NEG))
            m10 = cb[H:, :H] * jnp.exp2(col[H:] - row[:, :H])
            m11 = cb[H:, H:] * jnp.exp2(jnp.where(tri, col[H:] - row[:, H:], NEG))
            top = jnp.dot(m00.astype(BF16), dxb[:H, cols], preferred_element_type=F32)
            bot = (jnp.dot(m10.astype(BF16), dxb[:H, cols], preferred_element_type=F32)
                   + jnp.dot(m11.astype(BF16), dxb[H:, cols], preferred_element_type=F32))
            ys.append(jnp.concatenate([top, bot], axis=0))
        y_pairs.append(jnp.where(low_half, ys[0], ys[1]))

    state_sc[...] = (state_sc[...] * eacum_e[L - 1:L, :]
                     + lax.dot_general(bm, wdx, _TN, preferred_element_type=F32))

    y = jnp.concatenate(y_pairs, axis=1) + y_inter * eacum_e + dskip_ref[...] * xs
    y = y * _silu(z_ref[0].astype(F32))
    ms = jnp.mean(y * y, axis=-1, keepdims=True)
    y_ref[0] = (y * lax.rsqrt(ms + NORM_EPS) * ng_ref[...]).astype(y_ref.dtype)


def _ssd_scan(main, dt, conv_w, conv_b, dt_bias, a_log, d_skip, norm_g):
    b, s, _ = main.shape
    G, N, P, L = SSM_GROUPS, SSM_STATE, SSM_HEAD_DIM, SSM_CHUNK
    di = norm_g.shape[-1]
    gw = di // G
    hpg = gw // P
    assert s % L == 0
    xoff = di // gw
    boff = 2 * di // N
    coff = boff + G
    seq = lambda wd, off: pl.BlockSpec((1, L, wd), lambda bi, g, c: (bi, c, off + g))
    par = lambda r, wd, off: pl.BlockSpec((r, wd), lambda bi, g, c: (0, off + g))
    return pl.pallas_call(
        functools.partial(_ssd_kernel, hpg=hpg),
        grid=(b, G, s // L),
        in_specs=[
            seq(gw, 0), seq(gw, xoff), seq(N, boff), seq(N, coff), seq(LANES, 0),
            par(SSM_CONV, gw, 0), par(SSM_CONV, N, di // N), par(SSM_CONV, N, di // N + G),
            par(1, gw, 0), par(1, N, di // N), par(1, N, di // N + G),
            par(1, LANES, 0), par(1, LANES, 0), par(1, gw, 0), par(1, gw, 0),
        ],
        out_specs=seq(gw, 0),
        out_shape=jax.ShapeDtypeStruct((b, s, di), BF16),
        scratch_shapes=[
            pltpu.VMEM((N, gw), F32),
            pltpu.VMEM((SUBLANES + L, gw), F32),
            pltpu.VMEM((SUBLANES + L, N), F32),
            pltpu.VMEM((SUBLANES + L, N), F32),
        ],
        compiler_params=_params("parallel", "parallel", "arbitrary"),
        name="ssd_scan",
    )(main, main, main, main, dt, conv_w, conv_w, conv_w, conv_b, conv_b, conv_b,
      dt_bias, a_log, d_skip, norm_g)


def _pad_heads(v, groups):
    hpg = v.shape[0] // groups
    return jnp.pad(v.reshape(groups, hpg), ((0, 0), (0, LANES - hpg))).reshape(1, groups * LANES)


def _ssd_layer(x, sh, sc, gate, g_norm, w_in, conv_w, conv_b, dt_bias, a_log, d_skip, norm_g, w_out):
    G = SSM_GROUPS
    di = norm_g.shape[0]
    heads = dt_bias.shape[0]
    hpg = heads // G
    n_main = 2 * di + 2 * G * SSM_STATE
    d = w_in.shape[0]
    w_dt = w_in[:, n_main:].reshape(d, G, hpg)
    w_dt = jnp.pad(w_dt, ((0, 0), (0, 0), (0, LANES - hpg))).reshape(d, G * LANES)
    w = jnp.concatenate([w_in[:, :n_main], w_dt], axis=1).astype(BF16)
    chunks = [(0, c * COL_CHUNK, None, 1) for c in range(n_main // COL_CHUNK)] + [(1, 0, None, 1)]
    main, dt = _proj(x, sh, sc, g_norm, w, chunks, [("tok", n_main, BF16, 1), ("tok", G * LANES, F32, 1)], None)
    y = _ssd_scan(main, dt, conv_w, conv_b.reshape(1, -1), _pad_heads(dt_bias, G), _pad_heads(a_log, G),
                  jnp.repeat(d_skip, SSM_HEAD_DIM).reshape(1, -1), norm_g.reshape(1, -1))
    return _outproj(y, w_out.astype(BF16), x, gate)


def _moe_pre_kernel(x_ref, sh_ref, sc_ref, g_ref, rwt_ref, rb_ref, h_ref, cls_ref, w_ref, cnt_ref):
    first = (pl.program_id(0) == 0) & (pl.program_id(1) == 0)

    @pl.when(first)
    def _():
        cnt_ref[...] = jnp.zeros(cnt_ref.shape, F32)

    tm, d = x_ref.shape[1], x_ref.shape[2]
    pieces = d // LANES
    h = _modulated_norm(x_ref[0], g_ref[...], sc_ref[0], sh_ref[0])
    half = pieces // 2
    for s in range(half):
        h_ref[pl.ds(s, tm, stride=half), :] = _pack_bf16_pair(h[:, s * LANES:(s + 1) * LANES],
                                                              h[:, (half + s) * LANES:(half + s + 1) * LANES])
    logits = lax.dot_general(rwt_ref[...], h, _NT, preferred_element_type=F32, precision=HIGHEST)
    scores = jax.nn.sigmoid(logits)
    biased = scores + rb_ref[...]
    row = lambda a, e: a[e:e + 1, :]
    epg = EXPERTS_PER_GROUP

    best, g_sel = None, None
    for g in range(N_EXPERT_GROUPS):
        v = [row(biased, g * epg + j) for j in range(epg)]
        gs = None
        for i in range(epg):
            for j in range(i + 1, epg):
                pair = v[i] + v[j]
                gs = pair if gs is None else jnp.maximum(gs, pair)
        if best is None:
            best, g_sel = gs, jnp.zeros(gs.shape, I32)
        else:
            better = gs > best
            best = jnp.where(better, gs, best)
            g_sel = jnp.where(better, g, g_sel)

    def pick(a, j):
        out = row(a, j)
        for g in range(1, N_EXPERT_GROUPS):
            out = jnp.where(g_sel == g, row(a, g * epg + j), out)
        return out

    vb = [pick(biased, j) for j in range(epg)]
    vs = [pick(scores, j) for j in range(epg)]

    def argmax_first(vals, exclude):
        bv, bi, bs = None, None, None
        for j in range(epg):
            cand = vals[j] if exclude is None else jnp.where(exclude == j, -jnp.inf, vals[j])
            if bv is None:
                bv, bi, bs = cand, jnp.zeros(cand.shape, I32), vs[0]
            else:
                better = cand > bv
                bv = jnp.where(better, cand, bv)
                bi = jnp.where(better, j, bi)
                bs = jnp.where(better, vs[j], bs)
        return bi, bs

    i1, s1 = argmax_first(vb, None)
    i2, s2 = argmax_first(vb, i1)
    tot = s1 + s2
    first_lo = i1 < i2
    lo = jnp.where(first_lo, i1, i2)
    hi = jnp.where(first_lo, i2, i1)
    pair_base = jnp.where(lo == 0, 0, jnp.where(lo == 1, 3, 5))
    cls = g_sel * PAIRS_PER_GROUP + pair_base + (hi - lo - 1)
    cls_ref[...] = cls
    w_ref[...] = jnp.concatenate([jnp.where(first_lo, s1, s2) / tot, jnp.where(first_lo, s2, s1) / tot], axis=0)
    cid = lax.broadcasted_iota(I32, (cnt_ref.shape[0], cls.shape[1]), 0)
    cnt_ref[...] += jnp.sum((cid == cls).astype(F32), axis=1, keepdims=True)


def _moe_pre(x, sh, sc, g_norm, router_w, router_bias, tm=512):
    b, s, d = x.shape
    tm = min(tm, s)
    nt = s // tm
    t_all = b * s
    e = router_w.shape[1]
    pieces = d // LANES
    return pl.pallas_call(
        _moe_pre_kernel,
        grid=(b, nt),
        in_specs=[
            pl.BlockSpec((1, tm, d), lambda bi, i: (bi, i, 0)),
            pl.BlockSpec((1, 1, d), lambda bi, i: (bi, 0, 0)),
            pl.BlockSpec((1, 1, d), lambda bi, i: (bi, 0, 0)),
            pl.BlockSpec((1, d), lambda bi, i: (0, 0)),
            pl.BlockSpec((e, d), lambda bi, i: (0, 0)),
            pl.BlockSpec((e, 1), lambda bi, i: (0, 0)),
        ],
        out_specs=[
            pl.BlockSpec((tm * pieces // 2, LANES), lambda bi, i: (bi * nt + i, 0)),
            pl.BlockSpec((1, tm), lambda bi, i: (0, bi * nt + i)),
            pl.BlockSpec((2, tm), lambda bi, i: (0, bi * nt + i)),
            pl.BlockSpec((N_CLASSES, LANES), lambda bi, i: (0, 0)),
        ],
        out_shape=[
            jax.ShapeDtypeStruct((t_all * pieces // 2, LANES), U32),
            jax.ShapeDtypeStruct((1, t_all), I32),
            jax.ShapeDtypeStruct((2, t_all), F32),
            jax.ShapeDtypeStruct((N_CLASSES, LANES), F32),
        ],
        compiler_params=_params("arbitrary", "arbitrary"),
        name="moe_pre",
    )(x, sh, sc, g_norm, router_w.T, router_bias.reshape(e, 1))


def _moe_rank_kernel(cls_ref, cnt_ref, dest_ref, meta_ref, pstart_sc, run_sc, *, sub, n_sub, nbp):
    nc = cnt_ref.shape[0]

    @pl.when(pl.program_id(0) == 0)
    def _():
        cnt = cnt_ref[...]
        padded = jnp.ceil(cnt / MOE_BLOCK) * MOE_BLOCK
        blk = lax.broadcasted_iota(I32, (1, nbp), 1).astype(F32) * MOE_BLOCK
        running = jnp.zeros((1, LANES), F32)
        block_c = jnp.zeros((1, nbp), F32)
        for c in range(nc):
            pstart_sc[c:c + 1, :] = running
            running = running + padded[c:c + 1, :]
            block_c = block_c + (running[:, 0:1] <= blk).astype(F32)
        run_sc[...] = jnp.zeros(run_sc.shape, F32)
        block_c = jnp.minimum(block_c, nc - 1.0)
        group = jnp.floor(block_c / PAIRS_PER_GROUP)
        pair = block_c - group * PAIRS_PER_GROUP
        lo = (pair >= 3.0).astype(F32) + (pair >= 5.0).astype(F32)
        hi = jnp.where(pair == 0.0, 1.0, jnp.where((pair == 1.0) | (pair == 3.0), 2.0, 3.0))
        n_used = jnp.broadcast_to(running[:, 0:1] / MOE_BLOCK, (1, nbp))
        meta_ref[...] = jnp.concatenate(
            [group * EXPERTS_PER_GROUP + lo, group * EXPERTS_PER_GROUP + hi, n_used], axis=0).astype(I32)

    upper = (lax.broadcasted_iota(I32, (sub, sub), 0) < lax.broadcasted_iota(I32, (sub, sub), 1)).astype(BF16)
    cid = lax.broadcasted_iota(I32, (nc, sub), 0)
    for j in range(n_sub):
        oh = (cid == cls_ref[:, j * sub:(j + 1) * sub]).astype(F32)
        rank = jnp.dot(oh.astype(BF16), upper, preferred_element_type=F32)
        base = pstart_sc[:, 0:1] + run_sc[:, 0:1]
        dest = jnp.sum(oh * (rank + base), axis=0, keepdims=True)
        dest_ref[:, j * sub:(j + 1) * sub] = dest.astype(I32)
        run_sc[...] += jnp.sum(oh, axis=1, keepdims=True)


def _moe_rank(cls, cnt, n_blocks):
    t_all = cls.shape[1]
    sub = 256
    tr = min(2048, t_all)
    nbp = -(-n_blocks // LANES) * LANES
    nc = cnt.shape[0]
    return pl.pallas_call(
        functools.partial(_moe_rank_kernel, sub=sub, n_sub=tr // sub, nbp=nbp),
        grid=(t_all // tr,),
        in_specs=[
            pl.BlockSpec((1, tr), lambda i: (0, i)),
            pl.BlockSpec((nc, LANES), lambda i: (0, 0)),
        ],
        out_specs=[
            pl.BlockSpec((1, tr), lambda i: (0, i)),
            pl.BlockSpec((3, nbp), lambda i: (0, 0)),
        ],
        out_shape=[
            jax.ShapeDtypeStruct((1, t_all), I32),
            jax.ShapeDtypeStruct((3, nbp), I32),
        ],
        scratch_shapes=[pltpu.VMEM((nc, LANES), F32), pltpu.VMEM((nc, LANES), F32)],
        compiler_params=_params("arbitrary"),
        name="moe_rank",
    )(cls, cnt)


def _tile_copy(src_ref, s, dst_ref, d, sem, rows):
    return pltpu.make_async_copy(src_ref.at[pl.ds(pl.multiple_of(s * rows, rows), rows)],
                                 dst_ref.at[pl.ds(pl.multiple_of(d * rows, rows), rows)], sem)


def _moe_dispatch_kernel(dest_hbm, h_ref, xs_in, xs_out, idx_smem, idx_sem, sem, *, tm, pieces):
    del xs_in
    i = pl.program_id(0)
    cp = pltpu.make_async_copy(dest_hbm.at[i], idx_smem, idx_sem)
    cp.start()
    cp.wait()

    def issue(t, carry):
        _tile_copy(h_ref, t, xs_out, idx_smem[0, t], sem, pieces).start()
        return carry

    lax.fori_loop(0, tm, issue, 0, unroll=8)

    def drain(t, carry):
        _tile_copy(h_ref, 0, xs_out, 0, sem, pieces).wait()
        return carry

    lax.fori_loop(0, tm, drain, 0, unroll=8)


def _moe_dispatch(dest_tiles, h, rows, tm, pieces):
    t_all = h.shape[0] // pieces
    zeros = jnp.zeros((rows * pieces, LANES), U32)
    return pl.pallas_call(
        functools.partial(_moe_dispatch_kernel, tm=tm, pieces=pieces),
        grid=(t_all // tm,),
        in_specs=[
            pl.BlockSpec(memory_space=pl.ANY),
            pl.BlockSpec((tm * pieces, LANES), lambda i: (i, 0)),
            pl.BlockSpec(memory_space=pl.ANY),
        ],
        out_specs=pl.BlockSpec(memory_space=pl.ANY),
        out_shape=jax.ShapeDtypeStruct((rows * pieces, LANES), U32),
        scratch_shapes=[pltpu.SMEM((1, tm), I32), pltpu.SemaphoreType.DMA(()), pltpu.SemaphoreType.DMA(())],
        input_output_aliases={2: 0},
        compiler_params=_params("arbitrary"),
        name="moe_dispatch",
    )(dest_tiles, h, zeros)


def _moe_expert_kernel(meta_ref, x_ref, wga_ref, wua_ref, wda_ref, wgb_ref, wub_ref, wdb_ref, y_ref, xb_sc,
                       *, pieces):
    i = pl.program_id(0)
    rows = xb_sc.shape[0]

    @pl.when(i < meta_ref[2, 0])
    def _():
        half = pieces // 2
        for s in range(half):
            first, second = _unpack_bf16_pair(x_ref[pl.ds(s, rows, stride=half), :])
            xb_sc[:, s * LANES:(s + 1) * LANES] = first.astype(BF16)
            xb_sc[:, (half + s) * LANES:(half + s + 1) * LANES] = second.astype(BF16)
        x = xb_sc[...]
        halves = []
        for wg_ref, wu_ref, wd_ref in ((wga_ref, wua_ref, wda_ref), (wgb_ref, wub_ref, wdb_ref)):
            gate = jnp.dot(x, wg_ref[0], preferred_element_type=F32)
            up = jnp.dot(x, wu_ref[0], preferred_element_type=F32)
            hidden = (_silu(gate) * up).astype(BF16)
            y = jnp.dot(hidden, wd_ref[0], preferred_element_type=F32)
            halves.append(y)
        word = _pack_bf16_pair(halves[0], halves[1])
        for s in range(pieces):
            y_ref[pl.ds(s, rows, stride=pieces), :] = word[:, s * LANES:(s + 1) * LANES]

    @pl.when(i >= meta_ref[2, 0])
    def _():
        y_ref[...] = jnp.zeros(y_ref.shape, U32)


def _moe_experts(meta, xs, w_gate, w_up, w_down, n_blocks, pieces):
    d, f = w_gate.shape[1], w_gate.shape[2]
    wspec = lambda k, a, c: pl.BlockSpec((1, a, c), lambda i, meta: (meta[k, i], 0, 0))
    return pl.pallas_call(
        functools.partial(_moe_expert_kernel, pieces=pieces),
        grid_spec=pltpu.PrefetchScalarGridSpec(
            num_scalar_prefetch=1,
            grid=(n_blocks,),
            in_specs=[pl.BlockSpec((MOE_BLOCK * pieces // 2, LANES), lambda i, meta: (i, 0)),
                      wspec(0, d, f), wspec(0, d, f), wspec(0, f, d),
                      wspec(1, d, f), wspec(1, d, f), wspec(1, f, d)],
            out_specs=pl.BlockSpec((MOE_BLOCK * pieces, LANES), lambda i, meta: (i, 0)),
            scratch_shapes=[pltpu.VMEM((MOE_BLOCK, d), BF16)],
        ),
        out_shape=jax.ShapeDtypeStruct((2 * xs.shape[0], LANES), U32),
        compiler_params=_params("arbitrary"),
        name="moe_experts",
    )(meta, xs, w_gate, w_up, w_down, w_gate, w_up, w_down)


def _moe_combine_kernel(dest_hbm, y_hbm, w_ref, x_ref, g_ref, o_ref, y_sc, idx_smem, idx_sem, sem,
                        *, tm, nt, n_tiles, pieces):
    tile = pl.program_id(0) * nt + pl.program_id(1)
    slot = tile % 2
    other = 1 - slot

    def idx_copy(tl, sl):
        return pltpu.make_async_copy(dest_hbm.at[tl], idx_smem.at[pl.ds(sl, 1)], idx_sem.at[sl])

    def gather(sl):
        def issue(t, carry):
            _tile_copy(y_hbm, idx_smem[sl, t], y_sc.at[sl], t, sem.at[sl], pieces).start()
            return carry

        lax.fori_loop(0, tm, issue, 0, unroll=8)

    @pl.when(tile == 0)
    def _():
        idx_copy(0, 0).start()
        idx_copy(0, 0).wait()
        gather(0)
        if n_tiles > 1:
            idx_copy(1, 1).start()

    @pl.when(tile + 1 < n_tiles)
    def _():
        idx_copy(tile + 1, other).wait()
        gather(other)

        @pl.when(tile + 2 < n_tiles)
        def _():
            idx_copy(tile + 2, slot).start()

    def drain(t, carry):
        _tile_copy(y_hbm, 0, y_sc.at[slot], 0, sem.at[slot], pieces).wait()
        return carry

    lax.fori_loop(0, tm, drain, 0, unroll=8)
    w = w_ref[...]
    w_lo, w_hi = w[:, 0:1], w[:, 1:2]
    for s in range(pieces):
        cols = slice(s * LANES, (s + 1) * LANES)
        y_lo, y_hi = _unpack_bf16_pair(y_sc[slot, pl.ds(s, tm, stride=pieces), :])
        moe = w_lo * y_lo + w_hi * y_hi
        o_ref[0, :, cols] = x_ref[0, :, cols] + g_ref[0, :, cols] * moe


def _moe_combine(dest_tiles, y, w_col, x, gate, tm, pieces):
    b, s, d = x.shape
    nt = s // tm
    return pl.pallas_call(
        functools.partial(_moe_combine_kernel, tm=tm, nt=nt, n_tiles=b * nt, pieces=pieces),
        grid=(b, nt),
        in_specs=[
            pl.BlockSpec(memory_space=pl.ANY),
            pl.BlockSpec(memory_space=pl.ANY),
            pl.BlockSpec((tm, 2), lambda bi, i: (bi * nt + i, 0)),
            pl.BlockSpec((1, tm, d), lambda bi, i: (bi, i, 0)),
            pl.BlockSpec((1, 1, d), lambda bi, i: (bi, 0, 0)),
        ],
        out_specs=pl.BlockSpec((1, tm, d), lambda bi, i: (bi, i, 0)),
        out_shape=jax.ShapeDtypeStruct((b, s, d), F32),
        scratch_shapes=[
            pltpu.VMEM((2, tm * pieces, LANES), U32),
            pltpu.SMEM((2, tm), I32),
            pltpu.SemaphoreType.DMA((2,)),
            pltpu.SemaphoreType.DMA((2,)),
        ],
        compiler_params=_params("arbitrary", "arbitrary"),
        name="moe_combine",
    )(dest_tiles, y, w_col, x, gate)


def _moe_layer(x, sh, sc, gate, g_norm, router_w, router_bias, w_gate, w_up, w_down):
    b, s, d = x.shape
    t_all = b * s
    pieces = d // LANES
    tm = min(512, s)
    n_blocks = -(-t_all // MOE_BLOCK) + N_CLASSES
    rows = n_blocks * MOE_BLOCK
    h, cls, w, cnt = _moe_pre(x, sh, sc, g_norm, router_w, router_bias)
    dest, meta = _moe_rank(cls, cnt, n_blocks)
    dest_tiles = dest.reshape(t_all // tm, 1, tm)
    xs = _moe_dispatch(dest_tiles, h, rows, tm, pieces // 2)
    y = _moe_experts(meta, xs, w_gate.astype(BF16), w_up.astype(BF16), w_down.astype(BF16), n_blocks, pieces)
    return _moe_combine(dest_tiles, y, w.T, x, gate, tm, pieces)


def _final_norm_kernel(x_ref, g_ref, o_ref):
    x = x_ref[0]
    ms = jnp.mean(x * x, axis=-1, keepdims=True)
    o_ref[0] = x * lax.rsqrt(ms + NORM_EPS) * g_ref[...]


def _final_norm(x, g, tm=1024):
    b, s, d = x.shape
    tm = min(tm, s)
    return pl.pallas_call(
        _final_norm_kernel,
        grid=(b, s // tm),
        in_specs=[
            pl.BlockSpec((1, tm, d), lambda bi, i: (bi, i, 0)),
            pl.BlockSpec((1, d), lambda bi, i: (0, 0)),
        ],
        out_specs=pl.BlockSpec((1, tm, d), lambda bi, i: (bi, i, 0)),
        out_shape=jax.ShapeDtypeStruct((b, s, d), F32),
        compiler_params=_params("parallel", "arbitrary"),
        name="final_norm",
    )(x, g.reshape(1, d))


def kernel(x, c, ada_w, ada_b, norm1_g, norm2_g, router_w, router_bias, moe_w_gate, moe_w_up, moe_w_down, dil_w_in, dil_w_out, diff_w_in, diff_lam_q1, diff_lam_k1, diff_lam_q2, diff_lam_k2, diff_head_norm_g, diff_w_out, ssm_w_in, ssm_conv_w, ssm_conv_b, ssm_dt_bias, ssm_A_log, ssm_D, ssm_norm_g, ssm_w_out, final_norm_g):
    b, s, d = x.shape
    depth = ada_w.shape[0]
    mod = _ada_mod(c, ada_w, ada_b).reshape(depth, b, 6, 1, d)
    tables = _rope_tables(s)
    for i in range(depth):
        sh1, sc1, g1, sh2, sc2, g2 = (mod[i, :, j] for j in range(6))
        n1 = norm1_g[i].reshape(1, d)
        kind, j = i % 3, i // 3
        if kind == 0:
            x = _dilated_layer(x, sh1, sc1, g1, n1, dil_w_in[j], dil_w_out[j], tables)
        elif kind == 1:
            x = _diff_layer(x, sh1, sc1, g1, n1, diff_w_in[j], diff_lam_q1[j], diff_lam_k1[j],
                            diff_lam_q2[j], diff_lam_k2[j], diff_head_norm_g[j], diff_w_out[j], tables, i)
        else:
            x = _ssd_layer(x, sh1, sc1, g1, n1, ssm_w_in[j], ssm_conv_w[j], ssm_conv_b[j], ssm_dt_bias[j],
                           ssm_A_log[j], ssm_D[j], ssm_norm_g[j], ssm_w_out[j])
        x = _moe_layer(x, sh2, sc2, g2, norm2_g[i].reshape(1, d), router_w, router_bias,
                       moe_w_gate[i], moe_w_up[i], moe_w_down[i])
    return _final_norm(x, final_norm_g)
```

```python
import functools
import math

import jax
import jax.numpy as jnp
from jax import lax
from jax.experimental import pallas as pl
from jax.experimental.pallas import tpu as pltpu

F32 = jnp.float32
BF16 = jnp.bfloat16
I32 = jnp.int32
U32 = jnp.uint32
HIGHEST = lax.Precision.HIGHEST

LANES = 128
SUBLANES = 8
VMEM_LIMIT_BYTES = 56 * 1024 * 1024

NORM_EPS = 1e-6
ROPE_THETA = 500000.0
ROPE_FRACTION = 4
HEAD_DIM = 64
ATTN_BLOCK = 128
DIL_CONFIGS = ((128, 1), (512, 4), (2048, 16))
DIFF_NORM_EPS = 1e-5
SSM_HEAD_DIM = 64
SSM_GROUPS = 4
SSM_STATE = 128
SSM_CONV = 4
SSM_CHUNK = 256
N_EXPERTS = 16
N_EXPERT_GROUPS = 4
EXPERTS_PER_GROUP = 4
PAIRS_PER_GROUP = 6
N_CLASSES = N_EXPERT_GROUPS * PAIRS_PER_GROUP
MOE_BLOCK = 256
COL_CHUNK = 512
NEG = -1e30

_NT = (((1,), (1,)), ((), ()))
_TN = (((0,), (0,)), ((), ()))


def _params(*sem):
    return pltpu.CompilerParams(dimension_semantics=sem, vmem_limit_bytes=VMEM_LIMIT_BYTES)


def _silu(v):
    return v * (0.5 * jnp.tanh(0.5 * v) + 0.5)


def _pack_bf16_pair(a, b):
    hi = lax.bitcast_convert_type(a.astype(BF16).astype(F32), U32)
    lo = lax.bitcast_convert_type(b.astype(BF16).astype(F32), U32)
    return hi | (lo >> 16)


def _unpack_bf16_pair(word):
    return (lax.bitcast_convert_type(word & jnp.uint32(0xFFFF0000), F32),
            lax.bitcast_convert_type(word << 16, F32))


def _ada_kernel(c_ref, w_ref, b_ref, o_ref):
    cond = _silu(c_ref[...])
    o_ref[0] = jnp.dot(cond, w_ref[0], preferred_element_type=F32, precision=HIGHEST) + b_ref[0]


def _ada_mod(c, ada_w, ada_b):
    depth, d, n = ada_w.shape
    b = c.shape[0]
    tn = 1536
    return pl.pallas_call(
        _ada_kernel,
        grid=(depth, n // tn),
        in_specs=[
            pl.BlockSpec((b, d), lambda i, j: (0, 0)),
            pl.BlockSpec((1, d, tn), lambda i, j: (i, 0, j)),
            pl.BlockSpec((1, 1, tn), lambda i, j: (i, 0, j)),
        ],
        out_specs=pl.BlockSpec((1, b, tn), lambda i, j: (i, 0, j)),
        out_shape=jax.ShapeDtypeStruct((depth, b, n), F32),
        compiler_params=_params("arbitrary", "arbitrary"),
        name="ada_mod",
    )(c, ada_w, ada_b.reshape(depth, 1, n))


def _rope_tables(seq):
    r = HEAD_DIM // ROPE_FRACTION
    half = r // 2
    inv = jnp.power(ROPE_THETA, -jnp.arange(half, dtype=F32) * 2.0 / r)
    ang = jnp.arange(seq, dtype=F32)[:, None] * inv[None, :]
    cos, sin = jnp.cos(ang), jnp.sin(ang)
    ones = jnp.ones((seq, HEAD_DIM - r), F32)
    zeros = jnp.zeros((seq, HEAD_DIM - r), F32)
    zh = jnp.zeros((seq, half), F32)
    cos_t = jnp.concatenate([cos, cos, ones], axis=1)
    sin_a = jnp.concatenate([zh, sin, zeros], axis=1)
    sin_b = jnp.concatenate([-sin, zh, zeros], axis=1)
    rep = LANES // HEAD_DIM
    lane_form = tuple(jnp.tile(t, (1, rep)) for t in (cos_t, sin_a, sin_b))
    return lane_form, tuple(t.T for t in lane_form)


def _modulated_norm(x, g, sc, sh):
    ms = jnp.mean(x * x, axis=-1, keepdims=True)
    return x * lax.rsqrt(ms + NORM_EPS) * g * (1.0 + sc) + sh


def _rope(a, cos, sa, sb, scale, axis):
    half = HEAD_DIM // ROPE_FRACTION // 2
    r = a * cos + pltpu.roll(a, half, axis) * sa + pltpu.roll(a, LANES - half, axis) * sb
    return r if scale == 1.0 else r * scale


def _proj_kernel(*refs, chunks, tplan, n_out, use_rope, has_perm, pending):
    x_ref, sh_ref, sc_ref, g_ref, w_ref = refs[:5]
    pos = 5
    if use_rope:
        cos_ref, sa_ref, sb_ref = refs[pos:pos + 3]
        pos += 3
    if tplan is not None:
        wt_ref, cost_ref, sat_ref, sbt_ref = refs[pos:pos + 4]
        pos += 4
    if pending:
        dest_hbm, y_hbm, wcol_ref, g2_ref = refs[pos:pos + 4]
        pos += 4
    out_refs = refs[pos:pos + n_out]
    pos += n_out
    if pending:
        xnew_ref = refs[pos]
        pos += 1
    perm_sc = None
    if has_perm:
        perm_sc = refs[pos]
        pos += 1
    tm = x_ref.shape[1]
    if pending:
        _combine_tile(dest_hbm, y_hbm, wcol_ref, x_ref, g2_ref, xnew_ref, *refs[pos:pos + 4],
                      tile=pl.program_id(0) * pl.num_programs(1) + pl.program_id(1),
                      n_tiles=pl.num_programs(0) * pl.num_programs(1), tm=tm, pieces=x_ref.shape[2] // LANES)
        x_ref = xnew_ref
    h = _modulated_norm(x_ref[0], g_ref[...], sc_ref[0], sh_ref[0]).astype(BF16)
    tiles_per_chunk = COL_CHUNK // LANES
    for c, (oi, off, scale, dil) in enumerate(chunks):
        o_ref = out_refs[oi]
        acc = jnp.dot(h, w_ref[:, c * COL_CHUNK:(c + 1) * COL_CHUNK], preferred_element_type=F32)
        tiles = None
        if scale is not None:
            cos, sa, sb = cos_ref[...], sa_ref[...], sb_ref[...]
            tiles = [_rope(acc[:, s * LANES:(s + 1) * LANES], cos, sa, sb, scale, 1)
                     for s in range(tiles_per_chunk)]
        if dil == 1:
            if tiles is None:
                o_ref[0, :, off:off + COL_CHUNK] = acc.astype(o_ref.dtype)
            else:
                for s, tl in enumerate(tiles):
                    o_ref[0, :, off + s * LANES:off + (s + 1) * LANES] = tl.astype(o_ref.dtype)
            continue
        if tiles is None:
            tiles = [acc[:, s * LANES:(s + 1) * LANES] for s in range(tiles_per_chunk)]
        for s, tl in enumerate(tiles):
            perm_sc[s] = tl
        for rho in range(dil):
            for s in range(tiles_per_chunk):
                o_ref[0, rho, :, off + s * LANES:off + (s + 1) * LANES] = (
                    perm_sc[s, pl.ds(rho, tm // dil, stride=dil), :].astype(o_ref.dtype))
    if tplan is not None:
        oi, scale = tplan
        o_ref = out_refs[oi]
        acct = lax.dot_general(wt_ref[...], h, _NT, preferred_element_type=F32)
        cost, sat, sbt = cost_ref[...], sat_ref[...], sbt_ref[...]
        for s in range(acct.shape[0] // LANES):
            rows = slice(s * LANES, (s + 1) * LANES)
            o_ref[0, rows, :] = _rope(acct[rows, :], cost, sat, sbt, scale, 0).astype(o_ref.dtype)


def _proj(x, sh, sc, g, w, chunks, outs, tables, wt=None, tplan=None, pending=None, tm=512):
    b, s, d = x.shape
    n = w.shape[1]
    tm = min(tm, s)
    use_rope = any(c[2] is not None for c in chunks)
    has_perm = any(c[3] > 1 for c in chunks)
    in_specs = [
        pl.BlockSpec((1, tm, d), lambda bi, i: (bi, i, 0)),
        pl.BlockSpec((1, 1, d), lambda bi, i: (bi, 0, 0)),
        pl.BlockSpec((1, 1, d), lambda bi, i: (bi, 0, 0)),
        pl.BlockSpec((1, d), lambda bi, i: (0, 0)),
        pl.BlockSpec((d, n), lambda bi, i: (0, 0), pipeline_mode=pl.Buffered(1)),
    ]
    args = [x, sh, sc, g, w]
    if use_rope:
        in_specs += [pl.BlockSpec((tm, LANES), lambda bi, i: (i, 0))] * 3
        args += list(tables[0])
    if tplan is not None:
        in_specs.append(pl.BlockSpec(wt.shape, lambda bi, i: (0, 0), pipeline_mode=pl.Buffered(1)))
        in_specs += [pl.BlockSpec((LANES, tm), lambda bi, i: (0, i))] * 3
        args += [wt] + list(tables[1])
    if pending is not None:
        dest_tiles, y, w_col, gate2 = pending
        assert dest_tiles.shape[-1] == tm
        nt = s // tm
        in_specs += [pl.BlockSpec(memory_space=pl.ANY), pl.BlockSpec(memory_space=pl.ANY),
                     pl.BlockSpec((tm, 2), lambda bi, i: (bi * nt + i, 0)),
                     pl.BlockSpec((1, 1, d), lambda bi, i: (bi, 0, 0))]
        args += [dest_tiles, y, w_col, gate2]
    out_specs, out_shape = [], []
    for layout, wd, dt, dil in outs:
        if layout == "tok":
            out_specs.append(pl.BlockSpec((1, tm, wd), lambda bi, i: (bi, i, 0)))
            out_shape.append(jax.ShapeDtypeStruct((b, s, wd), dt))
        elif layout == "res":
            out_specs.append(pl.BlockSpec((1, dil, tm // dil, wd), lambda bi, i: (bi, 0, i, 0)))
            out_shape.append(jax.ShapeDtypeStruct((b, dil, s // dil, wd), dt))
        else:
            out_specs.append(pl.BlockSpec((1, wd, tm), lambda bi, i: (bi, 0, i)))
            out_shape.append(jax.ShapeDtypeStruct((b, wd, s), dt))
    scratch = [pltpu.VMEM((COL_CHUNK // LANES, tm, LANES), F32)] if has_perm else []
    if pending is not None:
        out_specs.append(pl.BlockSpec((1, tm, d), lambda bi, i: (bi, i, 0)))
        out_shape.append(jax.ShapeDtypeStruct((b, s, d), F32))
        scratch += _combine_scratch(tm, d // LANES)
    return pl.pallas_call(
        functools.partial(_proj_kernel, chunks=tuple(chunks), tplan=tplan, n_out=len(outs),
                          use_rope=use_rope, has_perm=has_perm, pending=pending is not None),
        grid=(b, s // tm),
        in_specs=in_specs,
        out_specs=out_specs,
        out_shape=out_shape,
        scratch_shapes=scratch,
        compiler_params=_params(*(("arbitrary", "arbitrary") if pending is not None else ("parallel", "arbitrary"))),
        name="norm_proj",
    )(*args)


def _outproj_kernel(y_ref, w_ref, x_ref, g_ref, o_ref):
    y = jnp.dot(y_ref[0], w_ref[...], preferred_element_type=F32)
    o_ref[0] = x_ref[0] + g_ref[0] * y


def _outproj(y, w, x, gate, tm=512):
    b, s, d = x.shape
    k = y.shape[-1]
    tm = min(tm, s)
    return pl.pallas_call(
        _outproj_kernel,
        grid=(b, s // tm),
        in_specs=[
            pl.BlockSpec((1, tm, k), lambda bi, i: (bi, i, 0)),
            pl.BlockSpec((k, d), lambda bi, i: (0, 0), pipeline_mode=pl.Buffered(1)),
            pl.BlockSpec((1, tm, d), lambda bi, i: (bi, i, 0)),
            pl.BlockSpec((1, 1, d), lambda bi, i: (bi, 0, 0)),
        ],
        out_specs=pl.BlockSpec((1, tm, d), lambda bi, i: (bi, i, 0)),
        out_shape=jax.ShapeDtypeStruct((b, s, d), F32),
        compiler_params=_params("parallel", "arbitrary"),
        name="out_proj",
    )(y, w, x, gate)


def _dil_kernel(q_ref, kc_ref, kp_ref, vc_ref, vp_ref, o_ref, lse_ref, kbuf, vbuf, *, tq, back, heads):
    n = pl.program_id(2)
    qb = ATTN_BLOCK
    kbuf[0:qb] = kp_ref[0, 0]
    kbuf[qb:] = kc_ref[0, 0]
    vbuf[0:qb] = vp_ref[0, 0]
    vbuf[qb:] = vc_ref[0, 0]
    rows = heads * qb
    qi = lax.broadcasted_iota(I32, (rows, 2 * qb), 0) & (qb - 1)
    kj = lax.broadcasted_iota(I32, (rows, 2 * qb), 1)
    rel = kj - qi
    band = (rel >= qb - back) & (rel <= qb)
    lane = lax.broadcasted_iota(I32, (qb, LANES), 1)
    low_half = lane < HEAD_DIM
    zero_q = jnp.zeros((qb, LANES), BF16)
    ones_k = jnp.ones((2 * qb, LANES), BF16)
    for j in range(tq // qb):
        first_key = n * tq + (j - 1) * qb
        valid = band & (kj + first_key >= 0)
        parts = []
        for hp in range(heads // 2):
            cols = slice(hp * LANES, (hp + 1) * LANES)
            q2 = q_ref[0, 0, j * qb:(j + 1) * qb, cols]
            q_stack = jnp.concatenate([jnp.where(low_half, q2, zero_q), jnp.where(low_half, zero_q, q2)], axis=0)
            parts.append(lax.dot_general(q_stack, kbuf[j * qb:(j + 2) * qb, cols], _NT,
                                         preferred_element_type=F32))
        s = jnp.where(valid, jnp.concatenate(parts, axis=0), NEG)
        m = jnp.max(s, axis=-1, keepdims=True)
        pb = jnp.exp2(s - m).astype(BF16)
        l = jnp.dot(pb, ones_k, preferred_element_type=F32)
        inv = 1.0 / l
        lse = (jnp.broadcast_to(m, l.shape) + jnp.log2(l)) * math.log(2.0)
        lse_tile = jnp.zeros((qb, LANES), F32)
        for hp in range(heads // 2):
            cols = slice(hp * LANES, (hp + 1) * LANES)
            ra = slice(2 * hp * qb, (2 * hp + 1) * qb)
            rb = slice((2 * hp + 1) * qb, (2 * hp + 2) * qb)
            pv = jnp.dot(pb[2 * hp * qb:(2 * hp + 2) * qb], vbuf[j * qb:(j + 2) * qb, cols],
                         preferred_element_type=F32)
            o = jnp.where(low_half, pv[:qb] * inv[ra], pv[qb:] * inv[rb])
            o_ref[0, 0, j * qb:(j + 1) * qb, cols] = o.astype(o_ref.dtype)
            lse_tile = jnp.where(lane == 2 * hp, lse[ra], lse_tile)
            lse_tile = jnp.where(lane == 2 * hp + 1, lse[rb], lse_tile)
        lse_ref[0, 0, j * qb:(j + 1) * qb, :] = lse_tile


def _dil_group(proj, window, dilation):
    b, dil, ln, c = proj.shape
    back = window // dilation
    assert dil == dilation and back <= ATTN_BLOCK and ln % ATTN_BLOCK == 0
    width = c // 3
    heads = width // HEAD_DIM
    tq = min(512, ln)
    sub = tq // ATTN_BLOCK

    def cur(col):
        return pl.BlockSpec((1, 1, tq, width), lambda bi, r, n: (bi, r, n, col))

    def prev(col):
        return pl.BlockSpec((1, 1, ATTN_BLOCK, width),
                            lambda bi, r, n: (bi, r, jnp.maximum(n * sub - 1, 0), col))

    return pl.pallas_call(
        functools.partial(_dil_kernel, tq=tq, back=back, heads=heads),
        grid=(b, dil, ln // tq),
        in_specs=[cur(0), cur(1), prev(1), cur(2), prev(2)],
        out_specs=[
            pl.BlockSpec((1, 1, tq, width), lambda bi, r, n: (bi, r, n, 0)),
            pl.BlockSpec((1, 1, tq, LANES), lambda bi, r, n: (bi, r, n, 0)),
        ],
        out_shape=[
            jax.ShapeDtypeStruct((b, dil, ln, width), BF16),
            jax.ShapeDtypeStruct((b, dil, ln, LANES), F32),
        ],
        scratch_shapes=[pltpu.VMEM((tq + ATTN_BLOCK, width), BF16)] * 2,
        compiler_params=_params("parallel", "parallel", "arbitrary"),
        name=f"dil_attn_d{dilation}",
    )(proj, proj, proj, proj, proj)


def _dil_out_kernel(o0_ref, o1_ref, o2_ref, l0_ref, l1_ref, l2_ref, w_ref, x_ref, g_ref, out_ref,
                    o_sc, l1_sc, l2_sc):
    tm = x_ref.shape[1]

    def to_token_order(src_ref, dst_sc):
        dil = src_ref.shape[1]
        for rho in range(dil):
            src = src_ref[0, rho].astype(F32)
            for s in range(dst_sc.shape[0]):
                dst_sc[s, pl.ds(rho, tm // dil, stride=dil), :] = src[:, s * LANES:(s + 1) * LANES]
        return jnp.concatenate([dst_sc[s] for s in range(dst_sc.shape[0])], axis=1)

    ls = [l0_ref[0, 0], to_token_order(l1_ref, l1_sc), to_token_order(l2_ref, l2_sc)]
    mx =jnp.maximum(jnp.maximum(ls[0], ls[1]), ls[2])
    es = [jnp.exp(v - mx) for v in ls]
    inv = 1.0 / (es[0] + es[1] + es[2])
    width = o0_ref.shape[-1]
    expand = (lax.broadcasted_iota(I32, (LANES, width), 0)
              == lax.broadcasted_iota(I32, (LANES, width), 1) // HEAD_DIM).astype(BF16)
    o = jnp.zeros((tm, width), F32)
    for gi, (e, o_ref) in enumerate(zip(es, (o0_ref, o1_ref, o2_ref))):
        alpha = e * inv
        hi = alpha.astype(BF16)
        lo = (alpha - hi.astype(F32)).astype(BF16)
        a_full = (jnp.dot(hi, expand, preferred_element_type=F32)
                  + jnp.dot(lo, expand, preferred_element_type=F32))
        og = o_ref[0, 0].astype(F32) if gi == 0 else to_token_order(o_ref, o_sc)
        o = o + a_full * og
    y = jnp.dot(o.astype(BF16), w_ref[...], preferred_element_type=F32)
    out_ref[0] = x_ref[0] + g_ref[0] * y


def _dil_out(os_, lses, w, x, gate, tm=512):
    b, s, d = x.shape
    width = os_[0].shape[-1]
    tm = min(tm, s)
    tok = lambda wd: pl.BlockSpec((1, tm, wd), lambda bi, i: (bi, i, 0))
    res = lambda a: pl.BlockSpec((1, a.shape[1], tm // a.shape[1], a.shape[3]), lambda bi, i: (bi, 0, i, 0))
    return pl.pallas_call(
        _dil_out_kernel,
        grid=(b, s // tm),
        in_specs=[res(a) for a in os_] + [res(a) for a in lses] + [
            pl.BlockSpec((width, d), lambda bi, i: (0, 0)),
            tok(d),
            pl.BlockSpec((1, 1, d), lambda bi, i: (bi, 0, 0)),
        ],
        out_specs=tok(d),
        out_shape=jax.ShapeDtypeStruct((b, s, d), F32),
        scratch_shapes=[pltpu.VMEM((width // LANES, tm, LANES), F32), pltpu.VMEM((1, tm, LANES), F32),
                        pltpu.VMEM((1, tm, LANES), F32)],
        compiler_params=_params("parallel", "arbitrary"),
        name="dil_out",
    )(*os_, *lses, w, x, gate)


def _dilated_layer(x, pending, sh, sc, gate, g_norm, w_in, w_out, tables):
    b, s, _ = x.shape
    n = w_in.shape[1]
    gw = n // len(DIL_CONFIGS)
    chunks, outs = [], []
    for g, (_, dilation) in enumerate(DIL_CONFIGS):
        for kind in range(3):
            scale = (HEAD_DIM ** -0.5 * math.log2(math.e), 1.0, None)[kind]
            chunks.append((g, kind * COL_CHUNK, scale, dilation))
        outs.append(("tok" if dilation == 1 else "res", gw, BF16, dilation))
    projs = list(_proj(x, sh, sc, g_norm, w_in.astype(BF16), chunks, outs, tables, pending=pending))
    if pending is not None:
        x = projs.pop()
    os_, lses = [], []
    for proj, (window, dilation) in zip(projs, DIL_CONFIGS):
        o, lse = _dil_group(proj.reshape(b, dilation, s // dilation, gw), window, dilation)
        os_.append(o)
        lses.append(lse)
    return _dil_out(os_, lses, w_out.astype(BF16), x, gate)


def _diff_kernel(qi_tab, ki_tab, q_ref, kt_ref, v_ref, lq1_ref, lk1_ref, lq2_ref, lk2_ref, hg_ref, o_ref,
                 m_sc, acc_sc, *, t, r, lam_init):
    pair = pl.program_id(2)
    qi = qi_tab[pair]
    ki = ki_tab[pair]
    vw = v_ref.shape[-1]

    @pl.when(ki == 0)
    def _():
        m_sc[...] = jnp.full(m_sc.shape, NEG, F32)
        acc_sc[...] = jnp.zeros(acc_sc.shape, F32)

    def step(diagonal, r):
        kt = kt_ref[0]
        v_aug = jnp.concatenate([v_ref[0], jnp.ones((t, LANES), BF16)], axis=1)
        lane = lax.broadcasted_iota(I32, (r, vw), 1)
        work = [(mi, c) for mi in range(2) for c in range(t // r)]

        def scores(mi, c):
            in_half = (lane >= mi * HEAD_DIM) & (lane < (mi + 1) * HEAD_DIM)
            ncols = (c + 1) * r if diagonal else t
            q_c = jnp.where(in_half, q_ref[0, c * r:(c + 1) * r, :], jnp.zeros((r, vw), BF16))
            return jnp.dot(q_c, kt[:, :ncols], preferred_element_type=F32)

        ahead = 3
        pending = [scores(*wk) for wk in work[:ahead]]
        for idx, (mi, c) in enumerate(work):
            s = pending.pop(0)
            if idx + ahead < len(work):
                pending.append(scores(*work[idx + ahead]))
            ncols = s.shape[1]
            if diagonal:
                col = lax.broadcasted_iota(I32, (r, ncols), 1)
                row = lax.broadcasted_iota(I32, (r, ncols), 0) + c * r
                s = jnp.where(col <= row, s, NEG)
            tiles = [s[:, j * LANES:(j + 1) * LANES] for j in range(ncols // LANES)]
            tmax = functools.reduce(jnp.maximum, tiles)
            srows = slice(mi * t + c * r, mi * t + (c + 1) * r)
            m_prev = m_sc[srows, :]
            m_new = jnp.maximum(m_prev, jnp.max(tmax, axis=-1, keepdims=True))
            alpha = jnp.exp2(m_prev - m_new)
            p = jnp.concatenate([jnp.exp2(tl - m_new) for tl in tiles], axis=1).astype(BF16)
            pv = jnp.dot(p, v_aug[:ncols], preferred_element_type=F32)
            acc_sc[srows, :] = jnp.concatenate([alpha, alpha], axis=1) * acc_sc[srows, :] + pv
            m_sc[srows, :] = m_new

    @pl.when(ki < qi)
    def _():
        step(False, 2 * r)

    @pl.when(ki == qi)
    def _():
        step(True, 2 * r)
        lam = (jnp.exp(jnp.sum(lq1_ref[...] * lk1_ref[...], axis=-1, keepdims=True))
               - jnp.exp(jnp.sum(lq2_ref[...] * lk2_ref[...], axis=-1, keepdims=True)) + lam_init)
        o = acc_sc[:t, :vw] / acc_sc[:t, vw:] - lam * (acc_sc[t:, :vw] / acc_sc[t:, vw:])
        ms = jnp.mean(o * o, axis=-1, keepdims=True)
        o = o * lax.rsqrt(ms + DIFF_NORM_EPS) * hg_ref[...] * (1.0 - lam_init)
        o_ref[0] = o.astype(o_ref.dtype)


def _diff_attention(q, kt, v, lam_q1, lam_k1, lam_q2, lam_k2, head_g, lam_init, t=1024, r=128):
    b, s, d = q.shape
    vw = 2 * HEAD_DIM
    assert vw == LANES
    heads = d // vw
    t = min(t, s)
    r = min(r, t // 2)
    nq = s // t
    pairs = [(qi, ki) for qi in range(nq) for ki in range(qi + 1)]
    qi_tab = jnp.asarray([p[0] for p in pairs], I32)
    ki_tab = jnp.asarray([p[1] for p in pairs], I32)
    vec = lambda n: pl.BlockSpec((1, n), lambda bi, h, p, qt, kt_: (0, 0))
    return pl.pallas_call(
        functools.partial(_diff_kernel, t=t, r=r, lam_init=lam_init),
        grid_spec=pltpu.PrefetchScalarGridSpec(
            num_scalar_prefetch=2,
            grid=(b, heads, len(pairs)),
            in_specs=[
                pl.BlockSpec((1, t, vw), lambda bi, h, p, qt, kt_: (bi, qt[p], h)),
                pl.BlockSpec((1, vw, t), lambda bi, h, p, qt, kt_: (bi, h, kt_[p])),
                pl.BlockSpec((1, t, vw), lambda bi, h, p, qt, kt_: (bi, kt_[p], h)),
                vec(HEAD_DIM), vec(HEAD_DIM), vec(HEAD_DIM), vec(HEAD_DIM), vec(vw),
            ],
            out_specs=pl.BlockSpec((1, t, vw), lambda bi, h, p, qt, kt_: (bi, qt[p], h)),
            scratch_shapes=[
                pltpu.VMEM((2 * t, LANES), F32),
                pltpu.VMEM((2 * t, vw + LANES), F32),
            ],
        ),
        out_shape=jax.ShapeDtypeStruct((b, s, d), BF16),
        compiler_params=_params("parallel", "parallel", "arbitrary"),
        name="diff_attn",
    )(qi_tab, ki_tab, q, kt, v, lam_q1.reshape(1, -1), lam_k1.reshape(1, -1), lam_q2.reshape(1, -1),
      lam_k2.reshape(1, -1), head_g.reshape(1, -1))


def _diff_layer(x, pending, sh, sc, gate, g_norm, w_in, lam_q1, lam_k1, lam_q2, lam_k2, head_g, w_out,
                tables, layer_idx):
    d = w_in.shape[1] // 3
    nch = d // COL_CHUNK
    q_scale = HEAD_DIM ** -0.5 * math.log2(math.e)
    chunks = ([(0, c * COL_CHUNK, q_scale, 1) for c in range(nch)]
              + [(1, c * COL_CHUNK, None, 1) for c in range(nch)])
    w_qv = jnp.concatenate([w_in[:, :d], w_in[:, 2 * d:]], axis=1).astype(BF16)
    w_kt = w_in[:, d:2 * d].T.astype(BF16)
    q, v, kt, *rest = _proj(x, sh, sc, g_norm, w_qv, chunks,
                            [("tok", d, BF16, 1), ("tok", d, BF16, 1), ("T", d, BF16, 1)], tables,
                            wt=w_kt, tplan=(2, 1.0), pending=pending)
    if pending is not None:
        x = rest[0]
    lam_init = 0.8 - 0.6 * math.exp(-0.3 * layer_idx)
    o = _diff_attention(q, kt, v, lam_q1, lam_k1, lam_q2, lam_k2, head_g, lam_init)
    return _outproj(o, w_out.astype(BF16), x, gate)


def _causal_conv_silu(cur_ref, ext_sc, w_ref, b_ref):
    cur = cur_ref[0].astype(F32)
    rows = cur.shape[0]
    ext_sc[SUBLANES:, :] = cur
    w = w_ref[...]
    acc = cur * w[SSM_CONV - 1:SSM_CONV] + b_ref[...]
    for k in range(1, SSM_CONV):
        acc = acc + ext_sc[SUBLANES - k:SUBLANES - k + rows, :] * w[SSM_CONV - 1 - k:SSM_CONV - k]
    ext_sc[0:SUBLANES, :] = cur[rows - SUBLANES:rows]
    return _silu(acc)


def _ssd_kernel(z_ref, x_ref, b_ref, c_ref, dt_ref, wx_ref, wb_ref, wc_ref, bx_ref, bb_ref, bc_ref,
                dtb_ref, alog_ref, dskip_ref, ng_ref, y_ref,
                state_sc, tx_sc, tb_sc, tc_sc, *, hpg):
    ci = pl.program_id(2)
    L = x_ref.shape[1]
    P = SSM_HEAD_DIM
    gw = hpg * P

    @pl.when(ci == 0)
    def _():
        state_sc[...] = jnp.zeros(state_sc.shape, F32)
        tx_sc[0:SUBLANES, :] = jnp.zeros((SUBLANES, tx_sc.shape[1]), F32)
        tb_sc[0:SUBLANES, :] = jnp.zeros((SUBLANES, tb_sc.shape[1]), F32)
        tc_sc[0:SUBLANES, :] = jnp.zeros((SUBLANES, tc_sc.shape[1]), F32)

    xs = _causal_conv_silu(x_ref, tx_sc, wx_ref, bx_ref)
    bm = _causal_conv_silu(b_ref, tb_sc, wb_ref, bb_ref).astype(BF16)
    cm = _causal_conv_silu(c_ref, tc_sc, wc_ref, bc_ref).astype(BF16)

    raw = dt_ref[0] + dtb_ref[...]
    dt = jnp.maximum(raw, 0.0) + jnp.log(1.0 + jnp.exp(-jnp.abs(raw)))
    a = -jnp.exp(alog_ref[...])
    da = dt * a
    ti = lax.broadcasted_iota(I32, (L, L), 0)
    si = lax.broadcasted_iota(I32, (L, L), 1)
    causal = ti >= si
    acum = jnp.dot(causal.astype(F32), da, preferred_element_type=F32, precision=HIGHEST)
    acum2 = acum * math.log2(math.e)
    acum2_t = acum2.T
    last = acum[L - 1:L, :]
    H = L // 2
    tri = causal[:H, :H]
    expand = (lax.broadcasted_iota(I32, (LANES, gw), 0)
              == lax.broadcasted_iota(I32, (LANES, gw), 1) // P).astype(BF16)

    def per_head_lanes(a):
        hi = a.astype(BF16)
        lo = (a - hi.astype(F32)).astype(BF16)
        return (jnp.dot(hi, expand, preferred_element_type=F32)
                + jnp.dot(lo, expand, preferred_element_type=F32))

    dx = per_head_lanes(dt) * xs
    dxb = dx.astype(BF16)
    eacum_e = per_head_lanes(jnp.exp(acum))
    wdx = (per_head_lanes(jnp.exp(last - acum)) * dx).astype(BF16)

    cb = lax.dot_general(cm, bm, _NT, preferred_element_type=F32)
    y_inter = jnp.dot(cm, state_sc[...].astype(BF16), preferred_element_type=F32)

    low_half = lax.broadcasted_iota(I32, (L, LANES), 1) < P
    y_pairs = []
    for hp in range(hpg // 2):
        cols = slice(hp * LANES, (hp + 1) * LANES)
        ys = []
        for k in (2 * hp, 2 * hp + 1):
            col = acum2[:, k:k + 1]
            row = acum2_t[k:k + 1, :]
            m00 = cb[:H, :H] * jnp.exp2(jnp.where(tri, col[:H] - row[:, :H], NEG))
            m10 = cb[H:, :H] * jnp.exp2(col[H:] - row[:, :H])
            m11 = cb[H:, H:] * jnp.exp2(jnp.where(tri, col[H:] - row[:, H:], NEG))
            top = jnp.dot(m00.astype(BF16), dxb[:H, cols], preferred_element_type=F32)
            bot = (jnp.dot(m10.astype(BF16), dxb[:H, cols], preferred_element_type=F32)
                   + jnp.dot(m11.astype(BF16), dxb[H:, cols], preferred_element_type=F32))
            ys.append(jnp.concatenate([top, bot], axis=0))
        y_pairs.append(jnp.where(low_half, ys[0], ys[1]))

    state_sc[...] = (state_sc[...] * eacum_e[L - 1:L, :]
                     + lax.dot_general(bm, wdx, _TN, preferred_element_type=F32))

    y = jnp.concatenate(y_pairs, axis=1) + y_inter * eacum_e + dskip_ref[...] * xs
    y = y * _silu(z_ref[0].astype(F32))
    ms = jnp.mean(y * y, axis=-1, keepdims=True)
    y_ref[0] = (y * lax.rsqrt(ms + NORM_EPS) * ng_ref[...]).astype(y_ref.dtype)


def _ssd_scan(main, dt, conv_w, conv_b, dt_bias, a_log, d_skip, norm_g):
    b, s, _ = main.shape
    G, N, P, L = SSM_GROUPS, SSM_STATE, SSM_HEAD_DIM, SSM_CHUNK
    di = norm_g.shape[-1]
    gw = di // G
    hpg = gw // P
    assert s % L == 0
    xoff = di // gw
    boff = 2 * di // N
    coff = boff + G
    seq = lambda wd, off: pl.BlockSpec((1, L, wd), lambda bi, g, c: (bi, c, off + g))
    par = lambda r, wd, off: pl.BlockSpec((r, wd), lambda bi, g, c: (0, off + g))
    return pl.pallas_call(
        functools.partial(_ssd_kernel, hpg=hpg),
        grid=(b, G, s // L),
        in_specs=[
            seq(gw, 0), seq(gw, xoff), seq(N, boff), seq(N, coff), seq(LANES, 0),
            par(SSM_CONV, gw, 0), par(SSM_CONV, N, di // N), par(SSM_CONV, N, di // N + G),
            par(1, gw, 0), par(1, N, di // N), par(1, N, di // N + G),
            par(1, LANES, 0), par(1, LANES, 0), par(1, gw, 0), par(1, gw, 0),
        ],
        out_specs=seq(gw, 0),
        out_shape=jax.ShapeDtypeStruct((b, s, di), BF16),
        scratch_shapes=[
            pltpu.VMEM((N, gw), F32),
            pltpu.VMEM((SUBLANES + L, gw), F32),
            pltpu.VMEM((SUBLANES + L, N), F32),
            pltpu.VMEM((SUBLANES + L, N), F32),
        ],
        compiler_params=_params("parallel", "parallel", "arbitrary"),
        name="ssd_scan",
    )(main, main, main, main, dt, conv_w, conv_w, conv_w, conv_b, conv_b, conv_b,
      dt_bias, a_log, d_skip, norm_g)


def _pad_heads(v, groups):
    hpg = v.shape[0] // groups
    return jnp.pad(v.reshape(groups, hpg), ((0, 0), (0, LANES - hpg))).reshape(1, groups * LANES)


def _ssd_layer(x, pending, sh, sc, gate, g_norm, w_in, conv_w, conv_b, dt_bias, a_log, d_skip, norm_g, w_out):
    G = SSM_GROUPS
    di = norm_g.shape[0]
    heads = dt_bias.shape[0]
    hpg = heads // G
    n_main = 2 * di + 2 * G * SSM_STATE
    d = w_in.shape[0]
    w_dt = w_in[:, n_main:].reshape(d, G, hpg)
    w_dt = jnp.pad(w_dt, ((0, 0), (0, 0), (0, LANES - hpg))).reshape(d, G * LANES)
    w = jnp.concatenate([w_in[:, :n_main], w_dt], axis=1).astype(BF16)
    chunks = [(0, c * COL_CHUNK, None, 1) for c in range(n_main // COL_CHUNK)] + [(1, 0, None, 1)]
    main, dt, *rest = _proj(x, sh, sc, g_norm, w, chunks, [("tok", n_main, BF16, 1), ("tok", G * LANES, F32, 1)],
                            None, pending=pending)
    if pending is not None:
        x = rest[0]
    y = _ssd_scan(main, dt, conv_w, conv_b.reshape(1, -1), _pad_heads(dt_bias, G), _pad_heads(a_log, G),
                  jnp.repeat(d_skip, SSM_HEAD_DIM).reshape(1, -1), norm_g.reshape(1, -1))
    return _outproj(y, w_out.astype(BF16), x, gate)


def _moe_pre_kernel(x_ref, sh_ref, sc_ref, g_ref, rwt_ref, rb_ref, h_ref, cls_ref, w_ref, cnt_ref):
    first = (pl.program_id(0) == 0) & (pl.program_id(1) == 0)

    @pl.when(first)
    def _():
        cnt_ref[...] = jnp.zeros(cnt_ref.shape, F32)

    tm, d = x_ref.shape[1], x_ref.shape[2]
    pieces = d // LANES
    h = _modulated_norm(x_ref[0], g_ref[...], sc_ref[0], sh_ref[0])
    half = pieces // 2
    for s in range(half):
        h_ref[pl.ds(s, tm, stride=half), :] = _pack_bf16_pair(h[:, s * LANES:(s + 1) * LANES],
                                                              h[:, (half + s) * LANES:(half + s + 1) * LANES])
    logits = lax.dot_general(rwt_ref[...], h, _NT, preferred_element_type=F32, precision=HIGHEST)
    scores = jax.nn.sigmoid(logits)
    biased = scores + rb_ref[...]
    row = lambda a, e: a[e:e + 1, :]
    epg = EXPERTS_PER_GROUP

    best, g_sel = None, None
    for g in range(N_EXPERT_GROUPS):
        v = [row(biased, g * epg + j) for j in range(epg)]
        gs = None
        for i in range(epg):
            for j in range(i + 1, epg):
                pair = v[i] + v[j]
                gs = pair if gs is None else jnp.maximum(gs, pair)
        if best is None:
            best, g_sel = gs, jnp.zeros(gs.shape, I32)
        else:
            better = gs > best
            best = jnp.where(better, gs, best)
            g_sel = jnp.where(better, g, g_sel)

    def pick(a, j):
        out = row(a, j)
        for g in range(1, N_EXPERT_GROUPS):
            out = jnp.where(g_sel == g, row(a, g * epg + j), out)
        return out

    vb = [pick(biased, j) for j in range(epg)]
    vs = [pick(scores, j) for j in range(epg)]

    def argmax_first(vals, exclude):
        bv, bi, bs = None, None, None
        for j in range(epg):
            cand = vals[j] if exclude is None else jnp.where(exclude == j, -jnp.inf, vals[j])
            if bv is None:
                bv, bi, bs = cand, jnp.zeros(cand.shape, I32), vs[0]
            else:
                better = cand > bv
                bv = jnp.where(better, cand, bv)
                bi = jnp.where(better, j, bi)
                bs = jnp.where(better, vs[j], bs)
        return bi, bs

    i1, s1 = argmax_first(vb, None)
    i2, s2 = argmax_first(vb, i1)
    tot = s1 + s2
    first_lo = i1 < i2
    lo = jnp.where(first_lo, i1, i2)
    hi = jnp.where(first_lo, i2, i1)
    pair_base = jnp.where(lo == 0, 0, jnp.where(lo == 1, 3, 5))
    cls = g_sel * PAIRS_PER_GROUP + pair_base + (hi - lo - 1)
    cls_ref[...] = cls
    w_ref[...] = jnp.concatenate([jnp.where(first_lo, s1, s2) / tot, jnp.where(first_lo, s2, s1) / tot], axis=0)
    cid = lax.broadcasted_iota(I32, (cnt_ref.shape[0], cls.shape[1]), 0)
    cnt_ref[...] += jnp.sum((cid == cls).astype(F32), axis=1, keepdims=True)


def _moe_pre(x, sh, sc, g_norm, router_w, router_bias, tm=512):
    b, s, d = x.shape
    tm = min(tm, s)
    nt = s // tm
    t_all = b * s
    e = router_w.shape[1]
    pieces = d // LANES
    return pl.pallas_call(
        _moe_pre_kernel,
        grid=(b, nt),
        in_specs=[
            pl.BlockSpec((1, tm, d), lambda bi, i: (bi, i, 0)),
            pl.BlockSpec((1, 1, d), lambda bi, i: (bi, 0, 0)),
            pl.BlockSpec((1, 1, d), lambda bi, i: (bi, 0, 0)),
            pl.BlockSpec((1, d), lambda bi, i: (0, 0)),
            pl.BlockSpec((e, d), lambda bi, i: (0, 0)),
            pl.BlockSpec((e, 1), lambda bi, i: (0, 0)),
        ],
        out_specs=[
            pl.BlockSpec((tm * pieces // 2, LANES), lambda bi, i: (bi * nt + i, 0)),
            pl.BlockSpec((1, tm), lambda bi, i: (0, bi * nt + i)),
            pl.BlockSpec((2, tm), lambda bi, i: (0, bi * nt + i)),
            pl.BlockSpec((N_CLASSES, LANES), lambda bi, i: (0, 0)),
        ],
        out_shape=[
            jax.ShapeDtypeStruct((t_all * pieces // 2, LANES), U32),
            jax.ShapeDtypeStruct((1, t_all), I32),
            jax.ShapeDtypeStruct((2, t_all), F32),
            jax.ShapeDtypeStruct((N_CLASSES, LANES), F32),
        ],
        compiler_params=_params("arbitrary", "arbitrary"),
        name="moe_pre",
    )(x, sh, sc, g_norm, router_w.T, router_bias.reshape(e, 1))


def _moe_rank_kernel(cls_ref, cnt_ref, dest_ref, meta_ref, pstart_sc, run_sc, *, sub, n_sub, nbp):
    nc = cnt_ref.shape[0]

    @pl.when(pl.program_id(0) == 0)
    def _():
        cnt = cnt_ref[...]
        padded = jnp.ceil(cnt / MOE_BLOCK) * MOE_BLOCK
        blk = lax.broadcasted_iota(I32, (1, nbp), 1).astype(F32) * MOE_BLOCK
        running = jnp.zeros((1, LANES), F32)
        block_c = jnp.zeros((1, nbp), F32)
        for c in range(nc):
            pstart_sc[c:c + 1, :] = running
            running = running + padded[c:c + 1, :]
            block_c = block_c + (running[:, 0:1] <= blk).astype(F32)
        run_sc[...] = jnp.zeros(run_sc.shape, F32)
        block_c = jnp.minimum(block_c, nc - 1.0)
        group = jnp.floor(block_c / PAIRS_PER_GROUP)
        pair = block_c - group * PAIRS_PER_GROUP
        lo = (pair >= 3.0).astype(F32) + (pair >= 5.0).astype(F32)
        hi = jnp.where(pair == 0.0, 1.0, jnp.where((pair == 1.0) | (pair == 3.0), 2.0, 3.0))
        n_used = jnp.broadcast_to(running[:, 0:1] / MOE_BLOCK, (1, nbp))
        meta_ref[...] = jnp.concatenate(
            [group * EXPERTS_PER_GROUP + lo, group * EXPERTS_PER_GROUP + hi, n_used], axis=0).astype(I32)

    upper = (lax.broadcasted_iota(I32, (sub, sub), 0) < lax.broadcasted_iota(I32, (sub, sub), 1)).astype(BF16)
    cid = lax.broadcasted_iota(I32, (nc, sub), 0)
    for j in range(n_sub):
        oh = (cid == cls_ref[:, j * sub:(j + 1) * sub]).astype(F32)
        rank = jnp.dot(oh.astype(BF16), upper, preferred_element_type=F32)
        base = pstart_sc[:, 0:1] + run_sc[:, 0:1]
        dest = jnp.sum(oh * (rank + base), axis=0, keepdims=True)
        dest_ref[:, j * sub:(j + 1) * sub] = dest.astype(I32)
        run_sc[...] += jnp.sum(oh, axis=1, keepdims=True)


def _moe_rank(cls, cnt, n_blocks):
    t_all = cls.shape[1]
    sub = 256
    tr = min(2048, t_all)
    nbp = -(-n_blocks // LANES) * LANES
    nc = cnt.shape[0]
    return pl.pallas_call(
        functools.partial(_moe_rank_kernel, sub=sub, n_sub=tr // sub, nbp=nbp),
        grid=(t_all // tr,),
        in_specs=[
            pl.BlockSpec((1, tr), lambda i: (0, i)),
            pl.BlockSpec((nc, LANES), lambda i: (0, 0)),
        ],
        out_specs=[
            pl.BlockSpec((1, tr), lambda i: (0, i)),
            pl.BlockSpec((3, nbp), lambda i: (0, 0)),
        ],
        out_shape=[
            jax.ShapeDtypeStruct((1, t_all), I32),
            jax.ShapeDtypeStruct((3, nbp), I32),
        ],
        scratch_shapes=[pltpu.VMEM((nc, LANES), F32), pltpu.VMEM((nc, LANES), F32)],
        compiler_params=_params("arbitrary"),
        name="moe_rank",
    )(cls, cnt)


def _tile_copy(src_ref, s, dst_ref, d, sem, rows):
    return pltpu.make_async_copy(src_ref.at[pl.ds(pl.multiple_of(s * rows, rows), rows)],
                                 dst_ref.at[pl.ds(pl.multiple_of(d * rows, rows), rows)], sem)


def _moe_dispatch_kernel(dest_hbm, h_ref, xs_in, xs_out, idx_smem, idx_sem, sem, *, tm, pieces):
    del xs_in
    i = pl.program_id(0)
    cp = pltpu.make_async_copy(dest_hbm.at[i], idx_smem, idx_sem)
    cp.start()
    cp.wait()

    def issue(t, carry):
        _tile_copy(h_ref, t, xs_out, idx_smem[0, t], sem, pieces).start()
        return carry

    lax.fori_loop(0, tm, issue, 0, unroll=8)

    def drain(t, carry):
        _tile_copy(h_ref, 0, xs_out, 0, sem, pieces).wait()
        return carry

    lax.fori_loop(0, tm, drain, 0, unroll=8)


def _moe_dispatch(dest_tiles, h, rows, tm, pieces):
    t_all = h.shape[0] // pieces
    zeros = jnp.zeros((rows * pieces, LANES), U32)
    return pl.pallas_call(
        functools.partial(_moe_dispatch_kernel, tm=tm, pieces=pieces),
        grid=(t_all // tm,),
        in_specs=[
            pl.BlockSpec(memory_space=pl.ANY),
            pl.BlockSpec((tm * pieces, LANES), lambda i: (i, 0)),
            pl.BlockSpec(memory_space=pl.ANY),
        ],
        out_specs=pl.BlockSpec(memory_space=pl.ANY),
        out_shape=jax.ShapeDtypeStruct((rows * pieces, LANES), U32),
        scratch_shapes=[pltpu.SMEM((1, tm), I32), pltpu.SemaphoreType.DMA(()), pltpu.SemaphoreType.DMA(())],
        input_output_aliases={2: 0},
        compiler_params=_params("arbitrary"),
        name="moe_dispatch",
    )(dest_tiles, h, zeros)


def _moe_expert_kernel(meta_ref, x_ref, wga_ref, wua_ref, wda_ref, wgb_ref, wub_ref, wdb_ref, y_ref, xb_sc,
                       *, pieces):
    i = pl.program_id(0)
    rows = xb_sc.shape[0]

    @pl.when(i < meta_ref[2, 0])
    def _():
        half = pieces // 2
        for s in range(half):
            first, second = _unpack_bf16_pair(x_ref[pl.ds(s, rows, stride=half), :])
            xb_sc[:, s * LANES:(s + 1) * LANES] = first.astype(BF16)
            xb_sc[:, (half + s) * LANES:(half + s + 1) * LANES] = second.astype(BF16)
        x = xb_sc[...]
        halves = []
        for wg_ref, wu_ref, wd_ref in ((wga_ref, wua_ref, wda_ref), (wgb_ref, wub_ref, wdb_ref)):
            gate = jnp.dot(x, wg_ref[0], preferred_element_type=F32)
            up = jnp.dot(x, wu_ref[0], preferred_element_type=F32)
            hidden = (_silu(gate) * up).astype(BF16)
            y = jnp.dot(hidden, wd_ref[0], preferred_element_type=F32)
            halves.append(y)
        word = _pack_bf16_pair(halves[0], halves[1])
        for s in range(pieces):
            y_ref[pl.ds(s, rows, stride=pieces), :] = word[:, s * LANES:(s + 1) * LANES]

    @pl.when(i >= meta_ref[2, 0])
    def _():
        y_ref[...] = jnp.zeros(y_ref.shape, U32)


def _moe_experts(meta, xs, w_gate, w_up, w_down, n_blocks, pieces):
    d, f = w_gate.shape[1], w_gate.shape[2]
    wspec = lambda k, a, c: pl.BlockSpec((1, a, c), lambda i, meta: (meta[k, i], 0, 0))
    return pl.pallas_call(
        functools.partial(_moe_expert_kernel, pieces=pieces),
        grid_spec=pltpu.PrefetchScalarGridSpec(
            num_scalar_prefetch=1,
            grid=(n_blocks,),
            in_specs=[pl.BlockSpec((MOE_BLOCK * pieces // 2, LANES), lambda i, meta: (i, 0)),
                      wspec(0, d, f), wspec(0, d, f), wspec(0, f, d),
                      wspec(1, d, f), wspec(1, d, f), wspec(1, f, d)],
            out_specs=pl.BlockSpec((MOE_BLOCK * pieces, LANES), lambda i, meta: (i, 0)),
            scratch_shapes=[pltpu.VMEM((MOE_BLOCK, d), BF16)],
        ),
        out_shape=jax.ShapeDtypeStruct((2 * xs.shape[0], LANES), U32),
        compiler_params=_params("arbitrary"),
        name="moe_experts",
    )(meta, xs, w_gate, w_up, w_down, w_gate, w_up, w_down)


def _combine_tile(dest_hbm, y_hbm, w_ref, x_ref, g_ref, o_ref, y_sc, idx_smem, idx_sem, sem,
                  *, tile, n_tiles, tm, pieces):
    slot = tile % 2
    other = 1 - slot

    def idx_copy(tl, sl):
        return pltpu.make_async_copy(dest_hbm.at[tl], idx_smem.at[pl.ds(sl, 1)], idx_sem.at[sl])

    def gather(sl):
        def issue(t, carry):
            _tile_copy(y_hbm, idx_smem[sl, t], y_sc.at[sl], t, sem.at[sl], pieces).start()
            return carry

        lax.fori_loop(0, tm, issue, 0, unroll=8)

    @pl.when(tile == 0)
    def _():
        idx_copy(0, 0).start()
        idx_copy(0, 0).wait()
        gather(0)
        if n_tiles > 1:
            idx_copy(1, 1).start()

    @pl.when(tile + 1 < n_tiles)
    def _():
        idx_copy(tile + 1, other).wait()
        gather(other)

        @pl.when(tile + 2 < n_tiles)
        def _():
            idx_copy(tile + 2, slot).start()

    def drain(t, carry):
        _tile_copy(y_hbm, 0, y_sc.at[slot], 0, sem.at[slot], pieces).wait()
        return carry

    lax.fori_loop(0, tm, drain, 0, unroll=8)
    w = w_ref[...]
    w_lo, w_hi = w[:, 0:1], w[:, 1:2]
    for s in range(pieces):
        cols = slice(s * LANES, (s + 1) * LANES)
        y_lo, y_hi = _unpack_bf16_pair(y_sc[slot, pl.ds(s, tm, stride=pieces), :])
        moe = w_lo * y_lo + w_hi * y_hi
        o_ref[0, :, cols] = x_ref[0, :, cols] + g_ref[0, :, cols] * moe


def _moe_combine_kernel(dest_hbm, y_hbm, w_ref, x_ref, g_ref, o_ref, y_sc, idx_smem, idx_sem, sem,
                        *, tm, nt, n_tiles, pieces):
    _combine_tile(dest_hbm, y_hbm, w_ref, x_ref, g_ref, o_ref, y_sc, idx_smem, idx_sem, sem,
                  tile=pl.program_id(0) * nt + pl.program_id(1), n_tiles=n_tiles, tm=tm, pieces=pieces)


def _combine_scratch(tm, pieces):
    return [pltpu.VMEM((2, tm * pieces, LANES), U32), pltpu.SMEM((2, tm), I32),
            pltpu.SemaphoreType.DMA((2,)), pltpu.SemaphoreType.DMA((2,))]


def _moe_combine(dest_tiles, y, w_col, x, gate, tm, pieces):
    b, s, d = x.shape
    nt = s // tm
    return pl.pallas_call(
        functools.partial(_moe_combine_kernel, tm=tm, nt=nt, n_tiles=b * nt, pieces=pieces),
        grid=(b, nt),
        in_specs=[
            pl.BlockSpec(memory_space=pl.ANY),
            pl.BlockSpec(memory_space=pl.ANY),
            pl.BlockSpec((tm, 2), lambda bi, i: (bi * nt + i, 0)),
            pl.BlockSpec((1, tm, d), lambda bi, i: (bi, i, 0)),
            pl.BlockSpec((1, 1, d), lambda bi, i: (bi, 0, 0)),
        ],
        out_specs=pl.BlockSpec((1, tm, d), lambda bi, i: (bi, i, 0)),
        out_shape=jax.ShapeDtypeStruct((b, s, d), F32),
        scratch_shapes=_combine_scratch(tm, pieces),
        compiler_params=_params("arbitrary", "arbitrary"),
        name="moe_combine",
    )(dest_tiles, y, w_col, x, gate)


def _moe_layer(x, sh, sc, gate, g_norm, router_w, router_bias, w_gate, w_up, w_down):
    b, s, d = x.shape
    t_all = b * s
    pieces = d // LANES
    tm = min(512, s)
    n_blocks = -(-t_all // MOE_BLOCK) + N_CLASSES
    rows = n_blocks * MOE_BLOCK
    h, cls, w, cnt = _moe_pre(x, sh, sc, g_norm, router_w, router_bias)
    dest, meta = _moe_rank(cls, cnt, n_blocks)
    dest_tiles = dest.reshape(t_all // tm, 1, tm)
    xs = _moe_dispatch(dest_tiles, h, rows, tm, pieces // 2)
    y = _moe_experts(meta, xs, w_gate.astype(BF16), w_up.astype(BF16), w_down.astype(BF16), n_blocks, pieces)
    return dest_tiles, y, w.T, gate


def _final_norm_kernel(x_ref, g_ref, o_ref):
    x = x_ref[0]
    ms = jnp.mean(x * x, axis=-1, keepdims=True)
    o_ref[0] = x * lax.rsqrt(ms + NORM_EPS) * g_ref[...]


def _final_norm(x, g, tm=1024):
    b, s, d = x.shape
    tm = min(tm, s)
    return pl.pallas_call(
        _final_norm_kernel,
        grid=(b, s // tm),
        in_specs=[
            pl.BlockSpec((1, tm, d), lambda bi, i: (bi, i, 0)),
            pl.BlockSpec((1, d), lambda bi, i: (0, 0)),
        ],
        out_specs=pl.BlockSpec((1, tm, d), lambda bi, i: (bi, i, 0)),
        out_shape=jax.ShapeDtypeStruct((b, s, d), F32),
        compiler_params=_params("parallel", "arbitrary"),
        name="final_norm",
    )(x, g.reshape(1, d))


def kernel(x, c, ada_w, ada_b, norm1_g, norm2_g, router_w, router_bias, moe_w_gate, moe_w_up, moe_w_down, dil_w_in, dil_w_out, diff_w_in, diff_lam_q1, diff_lam_k1, diff_lam_q2, diff_lam_k2, diff_head_norm_g, diff_w_out, ssm_w_in, ssm_conv_w, ssm_conv_b, ssm_dt_bias, ssm_A_log, ssm_D, ssm_norm_g, ssm_w_out, final_norm_g):
    b, s, d = x.shape
    depth = ada_w.shape[0]
    mod = _ada_mod(c, ada_w, ada_b).reshape(depth, b, 6, 1, d)
    tables = _rope_tables(s)
    pending = None
    for i in range(depth):
        sh1, sc1, g1, sh2, sc2, g2 = (mod[i, :, j] for j in range(6))
        n1 = norm1_g[i].reshape(1, d)
        kind, j = i % 3, i // 3
        if kind == 0:
            x = _dilated_layer(x, pending, sh1, sc1, g1, n1, dil_w_in[j], dil_w_out[j], tables)
        elif kind == 1:
            x = _diff_layer(x, pending, sh1, sc1, g1, n1, diff_w_in[j], diff_lam_q1[j], diff_lam_k1[j],
                            diff_lam_q2[j], diff_lam_k2[j], diff_head_norm_g[j], diff_w_out[j], tables, i)
        else:
            x = _ssd_layer(x, pending, sh1, sc1, g1, n1, ssm_w_in[j], ssm_conv_w[j], ssm_conv_b[j],
                           ssm_dt_bias[j], ssm_A_log[j], ssm_D[j], ssm_norm_g[j], ssm_w_out[j])
        pending = _moe_layer(x, sh2, sc2, g2, norm2_g[i].reshape(1, d), router_w, router_bias,
                             moe_w_gate[i], moe_w_up[i], moe_w_down[i])
    dest_tiles, y, w_col, gate = pending
    x = _moe_combine(dest_tiles, y, w_col, x, gate, dest_tiles.shape[-1], d // LANES)
    return _final_norm(x, final_norm_g)
```

```python
import functools
import math

import jax
import jax.numpy as jnp
from jax import lax
from jax.experimental import pallas as pl
from jax.experimental.pallas import tpu as pltpu

F32 = jnp.float32
BF16 = jnp.bfloat16
I32 = jnp.int32
U32 = jnp.uint32
HIGHEST = lax.Precision.HIGHEST

LANES = 128
SUBLANES = 8
VMEM_LIMIT_BYTES = 56 * 1024 * 1024

NORM_EPS = 1e-6
ROPE_THETA = 500000.0
ROPE_FRACTION = 4
HEAD_DIM = 64
ATTN_BLOCK = 128
DIL_CONFIGS = ((128, 1), (512, 4), (2048, 16))
DIFF_NORM_EPS = 1e-5
SSM_HEAD_DIM = 64
SSM_GROUPS = 4
SSM_STATE = 128
SSM_CONV = 4
SSM_CHUNK = 256
N_EXPERTS = 16
N_EXPERT_GROUPS = 4
EXPERTS_PER_GROUP = 4
PAIRS_PER_GROUP = 6
N_CLASSES = N_EXPERT_GROUPS * PAIRS_PER_GROUP
MOE_BLOCK = 256
COL_CHUNK = 512
NEG = -1e30

_NT = (((1,), (1,)), ((), ()))
_TN = (((0,), (0,)), ((), ()))


def _params(*sem):
    return pltpu.CompilerParams(dimension_semantics=sem, vmem_limit_bytes=VMEM_LIMIT_BYTES)


def _silu(v):
    return v * (0.5 * jnp.tanh(0.5 * v) + 0.5)


def _pack_bf16_pair(a, b):
    hi = lax.bitcast_convert_type(a.astype(BF16).astype(F32), U32)
    lo = lax.bitcast_convert_type(b.astype(BF16).astype(F32), U32)
    return hi | (lo >> 16)


def _unpack_bf16_pair(word):
    return (lax.bitcast_convert_type(word & jnp.uint32(0xFFFF0000), F32),
            lax.bitcast_convert_type(word << 16, F32))


def _ada_kernel(c_ref, w_ref, b_ref, o_ref):
    cond = _silu(c_ref[...])
    o_ref[0] = jnp.dot(cond, w_ref[0], preferred_element_type=F32, precision=HIGHEST) + b_ref[0]


def _ada_mod(c, ada_w, ada_b):
    depth, d, n = ada_w.shape
    b = c.shape[0]
    tn = 1536
    return pl.pallas_call(
        _ada_kernel,
        grid=(depth, n // tn),
        in_specs=[
            pl.BlockSpec((b, d), lambda i, j: (0, 0)),
            pl.BlockSpec((1, d, tn), lambda i, j: (i, 0, j)),
            pl.BlockSpec((1, 1, tn), lambda i, j: (i, 0, j)),
        ],
        out_specs=pl.BlockSpec((1, b, tn), lambda i, j: (i, 0, j)),
        out_shape=jax.ShapeDtypeStruct((depth, b, n), F32),
        compiler_params=_params("arbitrary", "arbitrary"),
        name="ada_mod",
    )(c, ada_w, ada_b.reshape(depth, 1, n))


def _rope_tables(seq):
    r = HEAD_DIM // ROPE_FRACTION
    half = r // 2
    inv = jnp.power(ROPE_THETA, -jnp.arange(half, dtype=F32) * 2.0 / r)
    ang = jnp.arange(seq, dtype=F32)[:, None] * inv[None, :]
    cos, sin = jnp.cos(ang), jnp.sin(ang)
    ones = jnp.ones((seq, HEAD_DIM - r), F32)
    zeros = jnp.zeros((seq, HEAD_DIM - r), F32)
    zh = jnp.zeros((seq, half), F32)
    cos_t = jnp.concatenate([cos, cos, ones], axis=1)
    sin_a = jnp.concatenate([zh, sin, zeros], axis=1)
    sin_b = jnp.concatenate([-sin, zh, zeros], axis=1)
    rep = LANES // HEAD_DIM
    lane_form = tuple(jnp.tile(t, (1, rep)) for t in (cos_t, sin_a, sin_b))
    return lane_form, tuple(t.T for t in lane_form)


def _modulated_norm(x, g, sc, sh):
    ms = jnp.mean(x * x, axis=-1, keepdims=True)
    return x * lax.rsqrt(ms + NORM_EPS) * g * (1.0 + sc) + sh


def _rope(a, cos, sa, sb, scale, axis):
    half = HEAD_DIM // ROPE_FRACTION // 2
    r = a * cos + pltpu.roll(a, half, axis) * sa + pltpu.roll(a, LANES - half, axis) * sb
    return r if scale == 1.0 else r * scale


def _proj_kernel(*refs, chunks, tplan, n_out, use_rope, has_perm, pending, n_tiles):
    x_ref, sh_ref, sc_ref, g_ref, w_ref = refs[:5]
    pos = 5
    if use_rope:
        cos_ref, sa_ref, sb_ref = refs[pos:pos + 3]
        pos += 3
    if tplan is not None:
        wt_ref, cost_ref, sat_ref, sbt_ref = refs[pos:pos + 4]
        pos += 4
    if pending:
        dest_hbm, y_hbm, wcol_ref, g2_ref = refs[pos:pos + 4]
        pos += 4
    out_refs = refs[pos:pos + n_out]
    pos += n_out
    if pending:
        xnew_ref = refs[pos]
        pos += 1
    perm_sc = None
    if has_perm:
        perm_sc = refs[pos]
        pos += 1
    tm = x_ref.shape[1]
    issue_part = None
    if pending:
        issue_part = _combine_tile(dest_hbm, y_hbm, wcol_ref, x_ref, g2_ref, xnew_ref, *refs[pos:pos + 4],
                                   tile=pl.program_id(0) * pl.num_programs(1) + pl.program_id(1),
                                   n_tiles=n_tiles, tm=tm,
                                   pieces=x_ref.shape[2] // LANES, n_parts=len(chunks))
        x_ref = xnew_ref
    h = _modulated_norm(x_ref[0], g_ref[...], sc_ref[0], sh_ref[0]).astype(BF16)
    tiles_per_chunk = COL_CHUNK // LANES
    for c, (oi, off, scale, dil) in enumerate(chunks):
        if issue_part is not None:
            issue_part(c)
        o_ref = out_refs[oi]
        acc = jnp.dot(h, w_ref[:, c * COL_CHUNK:(c + 1) * COL_CHUNK], preferred_element_type=F32)
        tiles = None
        if scale is not None:
            cos, sa, sb = cos_ref[...], sa_ref[...], sb_ref[...]
            tiles = [_rope(acc[:, s * LANES:(s + 1) * LANES], cos, sa, sb, scale, 1)
                     for s in range(tiles_per_chunk)]
        if dil == 1:
            if tiles is None:
                o_ref[0, :, off:off + COL_CHUNK] = acc.astype(o_ref.dtype)
            else:
                for s, tl in enumerate(tiles):
                    o_ref[0, :, off + s * LANES:off + (s + 1) * LANES] = tl.astype(o_ref.dtype)
            continue
        if tiles is None:
            tiles = [acc[:, s * LANES:(s + 1) * LANES] for s in range(tiles_per_chunk)]
        for s, tl in enumerate(tiles):
            perm_sc[s] = tl
        for rho in range(dil):
            for s in range(tiles_per_chunk):
                o_ref[0, rho, :, off + s * LANES:off + (s + 1) * LANES] = (
                    perm_sc[s, pl.ds(rho, tm // dil, stride=dil), :].astype(o_ref.dtype))
    if tplan is not None:
        oi, scale = tplan
        o_ref = out_refs[oi]
        acct = lax.dot_general(wt_ref[...], h, _NT, preferred_element_type=F32)
        cost, sat, sbt = cost_ref[...], sat_ref[...], sbt_ref[...]
        for s in range(acct.shape[0] // LANES):
            rows = slice(s * LANES, (s + 1) * LANES)
            o_ref[0, rows, :] = _rope(acct[rows, :], cost, sat, sbt, scale, 0).astype(o_ref.dtype)


def _proj(x, sh, sc, g, w, chunks, outs, tables, wt=None, tplan=None, pending=None, tm=512):
    b, s, d = x.shape
    n = w.shape[1]
    tm = min(tm, s)
    use_rope = any(c[2] is not None for c in chunks)
    has_perm = any(c[3] > 1 for c in chunks)
    in_specs = [
        pl.BlockSpec((1, tm, d), lambda bi, i: (bi, i, 0)),
        pl.BlockSpec((1, 1, d), lambda bi, i: (bi, 0, 0)),
        pl.BlockSpec((1, 1, d), lambda bi, i: (bi, 0, 0)),
        pl.BlockSpec((1, d), lambda bi, i: (0, 0)),
        pl.BlockSpec((d, n), lambda bi, i: (0, 0), pipeline_mode=pl.Buffered(1)),
    ]
    args = [x, sh, sc, g, w]
    if use_rope:
        in_specs += [pl.BlockSpec((tm, LANES), lambda bi, i: (i, 0))] * 3
        args += list(tables[0])
    if tplan is not None:
        in_specs.append(pl.BlockSpec(wt.shape, lambda bi, i: (0, 0), pipeline_mode=pl.Buffered(1)))
        in_specs += [pl.BlockSpec((LANES, tm), lambda bi, i: (0, i))] * 3
        args += [wt] + list(tables[1])
    if pending is not None:
        dest_tiles, y, w_col, gate2 = pending
        assert dest_tiles.shape[-1] == tm
        nt = s // tm
        in_specs += [pl.BlockSpec(memory_space=pl.ANY), pl.BlockSpec(memory_space=pl.ANY),
                     pl.BlockSpec((tm, 2), lambda bi, i: (bi * nt + i, 0)),
                     pl.BlockSpec((1, 1, d), lambda bi, i: (bi, 0, 0))]
        args += [dest_tiles, y, w_col, gate2]
    out_specs, out_shape = [], []
    for layout, wd, dt, dil in outs:
        if layout == "tok":
            out_specs.append(pl.BlockSpec((1, tm, wd), lambda bi, i: (bi, i, 0)))
            out_shape.append(jax.ShapeDtypeStruct((b, s, wd), dt))
        elif layout == "res":
            out_specs.append(pl.BlockSpec((1, dil, tm // dil, wd), lambda bi, i: (bi, 0, i, 0)))
            out_shape.append(jax.ShapeDtypeStruct((b, dil, s // dil, wd), dt))
        else:
            out_specs.append(pl.BlockSpec((1, wd, tm), lambda bi, i: (bi, 0, i)))
            out_shape.append(jax.ShapeDtypeStruct((b, wd, s), dt))
    scratch = [pltpu.VMEM((COL_CHUNK // LANES, tm, LANES), F32)] if has_perm else []
    if pending is not None:
        out_specs.append(pl.BlockSpec((1, tm, d), lambda bi, i: (bi, i, 0)))
        out_shape.append(jax.ShapeDtypeStruct((b, s, d), F32))
        scratch += _combine_scratch(tm, d // LANES)
    return pl.pallas_call(
        functools.partial(_proj_kernel, chunks=tuple(chunks), tplan=tplan, n_out=len(outs),
                          use_rope=use_rope, has_perm=has_perm, pending=pending is not None,
                          n_tiles=b * (s // tm)),
        grid=(b, s // tm),
        in_specs=in_specs,
        out_specs=out_specs,
        out_shape=out_shape,
        scratch_shapes=scratch,
        compiler_params=_params(*(("arbitrary", "arbitrary") if pending is not None else ("parallel", "arbitrary"))),
        name="norm_proj",
    )(*args)


def _outproj_kernel(y_ref, w_ref, x_ref, g_ref, o_ref):
    y = jnp.dot(y_ref[0], w_ref[...], preferred_element_type=F32)
    o_ref[0] = x_ref[0] + g_ref[0] * y


def _outproj(y, w, x, gate, tm=512):
    b, s, d = x.shape
    k = y.shape[-1]
    tm = min(tm, s)
    return pl.pallas_call(
        _outproj_kernel,
        grid=(b, s // tm),
        in_specs=[
            pl.BlockSpec((1, tm, k), lambda bi, i: (bi, i, 0)),
            pl.BlockSpec((k, d), lambda bi, i: (0, 0), pipeline_mode=pl.Buffered(1)),
            pl.BlockSpec((1, tm, d), lambda bi, i: (bi, i, 0)),
            pl.BlockSpec((1, 1, d), lambda bi, i: (bi, 0, 0)),
        ],
        out_specs=pl.BlockSpec((1, tm, d), lambda bi, i: (bi, i, 0)),
        out_shape=jax.ShapeDtypeStruct((b, s, d), F32),
        compiler_params=_params("parallel", "arbitrary"),
        name="out_proj",
    )(y, w, x, gate)


def _dil_kernel(q_ref, kc_ref, kp_ref, vc_ref, vp_ref, o_ref, lse_ref, kbuf, vbuf, *, tq, back, heads):
    n = pl.program_id(2)
    qb = ATTN_BLOCK
    kbuf[0:qb] = kp_ref[0, 0]
    kbuf[qb:] = kc_ref[0, 0]
    vbuf[0:qb] = vp_ref[0, 0]
    vbuf[qb:] = vc_ref[0, 0]
    rows = heads * qb
    qi = lax.broadcasted_iota(I32, (rows, 2 * qb), 0) & (qb - 1)
    kj = lax.broadcasted_iota(I32, (rows, 2 * qb), 1)
    rel = kj - qi
    band = (rel >= qb - back) & (rel <= qb)
    lane = lax.broadcasted_iota(I32, (qb, LANES), 1)
    low_half = lane < HEAD_DIM
    zero_q = jnp.zeros((qb, LANES), BF16)
    ones_k = jnp.ones((2 * qb, LANES), BF16)
    for j in range(tq // qb):
        first_key = n * tq + (j - 1) * qb
        valid = band & (kj + first_key >= 0)
        parts = []
        for hp in range(heads // 2):
            cols = slice(hp * LANES, (hp + 1) * LANES)
            q2 = q_ref[0, 0, j * qb:(j + 1) * qb, cols]
            q_stack = jnp.concatenate([jnp.where(low_half, q2, zero_q), jnp.where(low_half, zero_q, q2)], axis=0)
            parts.append(lax.dot_general(q_stack, kbuf[j * qb:(j + 2) * qb, cols], _NT,
                                         preferred_element_type=F32))
        s = jnp.where(valid, jnp.concatenate(parts, axis=0), NEG)
        m = jnp.max(s, axis=-1, keepdims=True)
        pb = jnp.exp2(s - m).astype(BF16)
        l = jnp.dot(pb, ones_k, preferred_element_type=F32)
        inv = 1.0 / l
        lse = (jnp.broadcast_to(m, l.shape) + jnp.log2(l)) * math.log(2.0)
        lse_tile = jnp.zeros((qb, LANES), F32)
        for hp in range(heads // 2):
            cols = slice(hp * LANES, (hp + 1) * LANES)
            ra = slice(2 * hp * qb, (2 * hp + 1) * qb)
            rb = slice((2 * hp + 1) * qb, (2 * hp + 2) * qb)
            pv = jnp.dot(pb[2 * hp * qb:(2 * hp + 2) * qb], vbuf[j * qb:(j + 2) * qb, cols],
                         preferred_element_type=F32)
            o = jnp.where(low_half, pv[:qb] * inv[ra], pv[qb:] * inv[rb])
            o_ref[0, 0, j * qb:(j + 1) * qb, cols] = o.astype(o_ref.dtype)
            lse_tile = jnp.where(lane == 2 * hp, lse[ra], lse_tile)
            lse_tile = jnp.where(lane == 2 * hp + 1, lse[rb], lse_tile)
        lse_ref[0, 0, j * qb:(j + 1) * qb, :] = lse_tile


def _dil_group(proj, window, dilation):
    b, dil, ln, c = proj.shape
    back = window // dilation
    assert dil == dilation and back <= ATTN_BLOCK and ln % ATTN_BLOCK == 0
    width = c // 3
    heads = width // HEAD_DIM
    tq = min(512, ln)
    sub = tq // ATTN_BLOCK

    def cur(col):
        return pl.BlockSpec((1, 1, tq, width), lambda bi, r, n: (bi, r, n, col))

    def prev(col):
        return pl.BlockSpec((1, 1, ATTN_BLOCK, width),
                            lambda bi, r, n: (bi, r, jnp.maximum(n * sub - 1, 0), col))

    return pl.pallas_call(
        functools.partial(_dil_kernel, tq=tq, back=back, heads=heads),
        grid=(b, dil, ln // tq),
        in_specs=[cur(0), cur(1), prev(1), cur(2), prev(2)],
        out_specs=[
            pl.BlockSpec((1, 1, tq, width), lambda bi, r, n: (bi, r, n, 0)),
            pl.BlockSpec((1, 1, tq, LANES), lambda bi, r, n: (bi, r, n, 0)),
        ],
        out_shape=[
            jax.ShapeDtypeStruct((b, dil, ln, width), BF16),
            jax.ShapeDtypeStruct((b, dil, ln, LANES), F32),
        ],
        scratch_shapes=[pltpu.VMEM((tq + ATTN_BLOCK, width), BF16)] * 2,
        compiler_params=_params("parallel", "parallel", "arbitrary"),
        name=f"dil_attn_d{dilation}",
    )(proj, proj, proj, proj, proj)


def _dil_out_kernel(o0_ref, o1_ref, o2_ref, l0_ref, l1_ref, l2_ref, w_ref, x_ref, g_ref, out_ref,
                    o_sc, l1_sc, l2_sc):
    tm = x_ref.shape[1]

    def to_token_order(src_ref, dst_sc):
        dil = src_ref.shape[1]
        for rho in range(dil):
            src = src_ref[0, rho].astype(F32)
            for s in range(dst_sc.shape[0]):
                dst_sc[s, pl.ds(rho, tm // dil, stride=dil), :] = src[:, s * LANES:(s + 1) * LANES]
        return jnp.concatenate([dst_sc[s] for s in range(dst_sc.shape[0])], axis=1)

    ls = [l0_ref[0, 0], to_token_order(l1_ref, l1_sc), to_token_order(l2_ref, l2_sc)]
    mx =jnp.maximum(jnp.maximum(ls[0], ls[1]), ls[2])
    es = [jnp.exp(v - mx) for v in ls]
    inv = 1.0 / (es[0] + es[1] + es[2])
    width = o0_ref.shape[-1]
    expand = (lax.broadcasted_iota(I32, (LANES, width), 0)
              == lax.broadcasted_iota(I32, (LANES, width), 1) // HEAD_DIM).astype(BF16)
    o = jnp.zeros((tm, width), F32)
    for gi, (e, o_ref) in enumerate(zip(es, (o0_ref, o1_ref, o2_ref))):
        alpha = e * inv
        hi = alpha.astype(BF16)
        lo = (alpha - hi.astype(F32)).astype(BF16)
        a_full = (jnp.dot(hi, expand, preferred_element_type=F32)
                  + jnp.dot(lo, expand, preferred_element_type=F32))
        og = o_ref[0, 0].astype(F32) if gi == 0 else to_token_order(o_ref, o_sc)
        o = o + a_full * og
    y = jnp.dot(o.astype(BF16), w_ref[...], preferred_element_type=F32)
    out_ref[0] = x_ref[0] + g_ref[0] * y


def _dil_out(os_, lses, w, x, gate, tm=512):
    b, s, d = x.shape
    width = os_[0].shape[-1]
    tm = min(tm, s)
    tok = lambda wd: pl.BlockSpec((1, tm, wd), lambda bi, i: (bi, i, 0))
    res = lambda a: pl.BlockSpec((1, a.shape[1], tm // a.shape[1], a.shape[3]), lambda bi, i: (bi, 0, i, 0))
    return pl.pallas_call(
        _dil_out_kernel,
        grid=(b, s // tm),
        in_specs=[res(a) for a in os_] + [res(a) for a in lses] + [
            pl.BlockSpec((width, d), lambda bi, i: (0, 0)),
            tok(d),
            pl.BlockSpec((1, 1, d), lambda bi, i: (bi, 0, 0)),
        ],
        out_specs=tok(d),
        out_shape=jax.ShapeDtypeStruct((b, s, d), F32),
        scratch_shapes=[pltpu.VMEM((width // LANES, tm, LANES), F32), pltpu.VMEM((1, tm, LANES), F32),
                        pltpu.VMEM((1, tm, LANES), F32)],
        compiler_params=_params("parallel", "arbitrary"),
        name="dil_out",
    )(*os_, *lses, w, x, gate)


def _dilated_layer(x, pending, sh, sc, gate, g_norm, w_in, w_out, tables):
    b, s, _ = x.shape
    n = w_in.shape[1]
    gw = n // len(DIL_CONFIGS)
    chunks, outs = [], []
    for g, (_, dilation) in enumerate(DIL_CONFIGS):
        for kind in range(3):
            scale = (HEAD_DIM ** -0.5 * math.log2(math.e), 1.0, None)[kind]
            chunks.append((g, kind * COL_CHUNK, scale, dilation))
        outs.append(("tok" if dilation == 1 else "res", gw, BF16, dilation))
    projs = list(_proj(x, sh, sc, g_norm, w_in.astype(BF16), chunks, outs, tables, pending=pending))
    if pending is not None:
        x = projs.pop()
    os_, lses = [], []
    for proj, (window, dilation) in zip(projs, DIL_CONFIGS):
        o, lse = _dil_group(proj.reshape(b, dilation, s // dilation, gw), window, dilation)
        os_.append(o)
        lses.append(lse)
    return _dil_out(os_, lses, w_out.astype(BF16), x, gate)


def _diff_kernel(qi_tab, ki_tab, q_ref, kt_ref, v_ref, lq1_ref, lk1_ref, lq2_ref, lk2_ref, hg_ref, o_ref,
                 m_sc, acc_sc, *, t, r, lam_init):
    pair = pl.program_id(2)
    qi = qi_tab[pair]
    ki = ki_tab[pair]
    vw = v_ref.shape[-1]

    @pl.when(ki == 0)
    def _():
        m_sc[...] = jnp.full(m_sc.shape, NEG, F32)
        acc_sc[...] = jnp.zeros(acc_sc.shape, F32)

    def step(diagonal, r):
        kt = kt_ref[0]
        v_aug = jnp.concatenate([v_ref[0], jnp.ones((t, LANES), BF16)], axis=1)
        lane = lax.broadcasted_iota(I32, (r, vw), 1)
        work = [(mi, c) for mi in range(2) for c in range(t // r)]

        def scores(mi, c):
            in_half = (lane >= mi * HEAD_DIM) & (lane < (mi + 1) * HEAD_DIM)
            ncols = (c + 1) * r if diagonal else t
            q_c = jnp.where(in_half, q_ref[0, c * r:(c + 1) * r, :], jnp.zeros((r, vw), BF16))
            return jnp.dot(q_c, kt[:, :ncols], preferred_element_type=F32)

        ahead = 3
        pending = [scores(*wk) for wk in work[:ahead]]
        for idx, (mi, c) in enumerate(work):
            s = pending.pop(0)
            if idx + ahead < len(work):
                pending.append(scores(*work[idx + ahead]))
            ncols = s.shape[1]
            if diagonal:
                col = lax.broadcasted_iota(I32, (r, ncols), 1)
                row = lax.broadcasted_iota(I32, (r, ncols), 0) + c * r
                s = jnp.where(col <= row, s, NEG)
            tiles = [s[:, j * LANES:(j + 1) * LANES] for j in range(ncols // LANES)]
            tmax = functools.reduce(jnp.maximum, tiles)
            srows = slice(mi * t + c * r, mi * t + (c + 1) * r)
            m_prev = m_sc[srows, :]
            m_new = jnp.maximum(m_prev, jnp.max(tmax, axis=-1, keepdims=True))
            alpha = jnp.exp2(m_prev - m_new)
            p = jnp.concatenate([jnp.exp2(tl - m_new) for tl in tiles], axis=1).astype(BF16)
            pv = jnp.dot(p, v_aug[:ncols], preferred_element_type=F32)
            acc_sc[srows, :] = jnp.concatenate([alpha, alpha], axis=1) * acc_sc[srows, :] + pv
            m_sc[srows, :] = m_new

    @pl.when(ki < qi)
    def _():
        step(False, 2 * r)

    @pl.when(ki == qi)
    def _():
        step(True, 2 * r)
        lam = (jnp.exp(jnp.sum(lq1_ref[...] * lk1_ref[...], axis=-1, keepdims=True))
               - jnp.exp(jnp.sum(lq2_ref[...] * lk2_ref[...], axis=-1, keepdims=True)) + lam_init)
        o = acc_sc[:t, :vw] / acc_sc[:t, vw:] - lam * (acc_sc[t:, :vw] / acc_sc[t:, vw:])
        ms = jnp.mean(o * o, axis=-1, keepdims=True)
        o = o * lax.rsqrt(ms + DIFF_NORM_EPS) * hg_ref[...] * (1.0 - lam_init)
        o_ref[0] = o.astype(o_ref.dtype)


def _diff_attention(q, kt, v, lam_q1, lam_k1, lam_q2, lam_k2, head_g, lam_init, t=1024, r=128):
    b, s, d = q.shape
    vw = 2 * HEAD_DIM
    assert vw == LANES
    heads = d // vw
    t = min(t, s)
    r = min(r, t // 2)
    nq = s // t
    pairs = [(qi, ki) for qi in range(nq) for ki in range(qi + 1)]
    qi_tab = jnp.asarray([p[0] for p in pairs], I32)
    ki_tab = jnp.asarray([p[1] for p in pairs], I32)
    vec = lambda n: pl.BlockSpec((1, n), lambda bi, h, p, qt, kt_: (0, 0))
    return pl.pallas_call(
        functools.partial(_diff_kernel, t=t, r=r, lam_init=lam_init),
        grid_spec=pltpu.PrefetchScalarGridSpec(
            num_scalar_prefetch=2,
            grid=(b, heads, len(pairs)),
            in_specs=[
                pl.BlockSpec((1, t, vw), lambda bi, h, p, qt, kt_: (bi, qt[p], h)),
                pl.BlockSpec((1, vw, t), lambda bi, h, p, qt, kt_: (bi, h, kt_[p])),
                pl.BlockSpec((1, t, vw), lambda bi, h, p, qt, kt_: (bi, kt_[p], h)),
                vec(HEAD_DIM), vec(HEAD_DIM), vec(HEAD_DIM), vec(HEAD_DIM), vec(vw),
            ],
            out_specs=pl.BlockSpec((1, t, vw), lambda bi, h, p, qt, kt_: (bi, qt[p], h)),
            scratch_shapes=[
                pltpu.VMEM((2 * t, LANES), F32),
                pltpu.VMEM((2 * t, vw + LANES), F32),
            ],
        ),
        out_shape=jax.ShapeDtypeStruct((b, s, d), BF16),
        compiler_params=_params("parallel", "parallel", "arbitrary"),
        name="diff_attn",
    )(qi_tab, ki_tab, q, kt, v, lam_q1.reshape(1, -1), lam_k1.reshape(1, -1), lam_q2.reshape(1, -1),
      lam_k2.reshape(1, -1), head_g.reshape(1, -1))


def _diff_layer(x, pending, sh, sc, gate, g_norm, w_in, lam_q1, lam_k1, lam_q2, lam_k2, head_g, w_out,
                tables, layer_idx):
    d = w_in.shape[1] // 3
    nch = d // COL_CHUNK
    q_scale = HEAD_DIM ** -0.5 * math.log2(math.e)
    chunks = ([(0, c * COL_CHUNK, q_scale, 1) for c in range(nch)]
              + [(1, c * COL_CHUNK, None, 1) for c in range(nch)])
    w_qv = jnp.concatenate([w_in[:, :d], w_in[:, 2 * d:]], axis=1).astype(BF16)
    w_kt = w_in[:, d:2 * d].T.astype(BF16)
    q, v, kt, *rest = _proj(x, sh, sc, g_norm, w_qv, chunks,
                            [("tok", d, BF16, 1), ("tok", d, BF16, 1), ("T", d, BF16, 1)], tables,
                            wt=w_kt, tplan=(2, 1.0), pending=pending)
    if pending is not None:
        x = rest[0]
    lam_init = 0.8 - 0.6 * math.exp(-0.3 * layer_idx)
    o = _diff_attention(q, kt, v, lam_q1, lam_k1, lam_q2, lam_k2, head_g, lam_init)
    return _outproj(o, w_out.astype(BF16), x, gate)


def _causal_conv_silu(cur_ref, ext_sc, w_ref, b_ref):
    cur = cur_ref[0].astype(F32)
    rows = cur.shape[0]
    ext_sc[SUBLANES:, :] = cur
    w = w_ref[...]
    acc = cur * w[SSM_CONV - 1:SSM_CONV] + b_ref[...]
    for k in range(1, SSM_CONV):
        acc = acc + ext_sc[SUBLANES - k:SUBLANES - k + rows, :] * w[SSM_CONV - 1 - k:SSM_CONV - k]
    ext_sc[0:SUBLANES, :] = cur[rows - SUBLANES:rows]
    return _silu(acc)


def _ssd_kernel(z_ref, x_ref, b_ref, c_ref, dt_ref, wx_ref, wb_ref, wc_ref, bx_ref, bb_ref, bc_ref,
                dtb_ref, alog_ref, dskip_ref, ng_ref, y_ref,
                state_sc, tx_sc, tb_sc, tc_sc, *, hpg):
    ci = pl.program_id(2)
    L = x_ref.shape[1]
    P = SSM_HEAD_DIM
    gw = hpg * P

    @pl.when(ci == 0)
    def _():
        state_sc[...] = jnp.zeros(state_sc.shape, F32)
        tx_sc[0:SUBLANES, :] = jnp.zeros((SUBLANES, tx_sc.shape[1]), F32)
        tb_sc[0:SUBLANES, :] = jnp.zeros((SUBLANES, tb_sc.shape[1]), F32)
        tc_sc[0:SUBLANES, :] = jnp.zeros((SUBLANES, tc_sc.shape[1]), F32)

    xs = _causal_conv_silu(x_ref, tx_sc, wx_ref, bx_ref)
    bm = _causal_conv_silu(b_ref, tb_sc, wb_ref, bb_ref).astype(BF16)
    cm = _causal_conv_silu(c_ref, tc_sc, wc_ref, bc_ref).astype(BF16)

    raw = dt_ref[0] + dtb_ref[...]
    dt = jnp.maximum(raw, 0.0) + jnp.log(1.0 + jnp.exp(-jnp.abs(raw)))
    a = -jnp.exp(alog_ref[...])
    da = dt * a
    ti = lax.broadcasted_iota(I32, (L, L), 0)
    si = lax.broadcasted_iota(I32, (L, L), 1)
    causal = ti >= si
    acum = jnp.dot(causal.astype(F32), da, preferred_element_type=F32, precision=HIGHEST)
    acum2 = acum * math.log2(math.e)
    acum2_t = acum2.T
    last = acum[L - 1:L, :]
    H = L // 2
    tri = causal[:H, :H]
    expand = (lax.broadcasted_iota(I32, (LANES, gw), 0)
              == lax.broadcasted_iota(I32, (LANES, gw), 1) // P).astype(BF16)

    def per_head_lanes(a):
        hi = a.astype(BF16)
        lo = (a - hi.astype(F32)).astype(BF16)
        return (jnp.dot(hi, expand, preferred_element_type=F32)
                + jnp.dot(lo, expand, preferred_element_type=F32))

    dx = per_head_lanes(dt) * xs
    dxb = dx.astype(BF16)
    eacum_e = per_head_lanes(jnp.exp(acum))
    wdx = (per_head_lanes(jnp.exp(last - acum)) * dx).astype(BF16)

    cb = lax.dot_general(cm, bm, _NT, preferred_element_type=F32)
    y_inter = jnp.dot(cm, state_sc[...].astype(BF16), preferred_element_type=F32)

    low_half = lax.broadcasted_iota(I32, (L, LANES), 1) < P
    y_pairs = []
    for hp in range(hpg // 2):
        cols = slice(hp * LANES, (hp + 1) * LANES)
        ys = []
        for k in (2 * hp, 2 * hp + 1):
            col = acum2[:, k:k + 1]
            row = acum2_t[k:k + 1, :]
            m00 = cb[:H, :H] * jnp.exp2(jnp.where(tri, col[:H] - row[:, :H], NEG))
            m10 = cb[H:, :H] * jnp.exp2(col[H:] - row[:, :H])
            m11 = cb[H:, H:] * jnp.exp2(jnp.where(tri, col[H:] - row[:, H:], NEG))
            top = jnp.dot(m00.astype(BF16), dxb[:H, cols], preferred_element_type=F32)
            bot = (jnp.dot(m10.astype(BF16), dxb[:H, cols], preferred_element_type=F32)
                   + jnp.dot(m11.astype(BF16), dxb[H:, cols], preferred_element_type=F32))
            ys.append(jnp.concatenate([top, bot], axis=0))
        y_pairs.append(jnp.where(low_half, ys[0], ys[1]))

    state_sc[...] = (state_sc[...] * eacum_e[L - 1:L, :]
                     + lax.dot_general(bm, wdx, _TN, preferred_element_type=F32))

    y = jnp.concatenate(y_pairs, axis=1) + y_inter * eacum_e + dskip_ref[...] * xs
    y = y * _silu(z_ref[0].astype(F32))
    ms = jnp.mean(y * y, axis=-1, keepdims=True)
    y_ref[0] = (y * lax.rsqrt(ms + NORM_EPS) * ng_ref[...]).astype(y_ref.dtype)


def _ssd_scan(main, dt, conv_w, conv_b, dt_bias, a_log, d_skip, norm_g):
    b, s, _ = main.shape
    G, N, P, L = SSM_GROUPS, SSM_STATE, SSM_HEAD_DIM, SSM_CHUNK
    di = norm_g.shape[-1]
    gw = di // G
    hpg = gw // P
    assert s % L == 0
    xoff = di // gw
    boff = 2 * di // N
    coff = boff + G
    seq = lambda wd, off: pl.BlockSpec((1, L, wd), lambda bi, g, c: (bi, c, off + g))
    par = lambda r, wd, off: pl.BlockSpec((r, wd), lambda bi, g, c: (0, off + g))
    return pl.pallas_call(
        functools.partial(_ssd_kernel, hpg=hpg),
        grid=(b, G, s // L),
        in_specs=[
            seq(gw, 0), seq(gw, xoff), seq(N, boff), seq(N, coff), seq(LANES, 0),
            par(SSM_CONV, gw, 0), par(SSM_CONV, N, di // N), par(SSM_CONV, N, di // N + G),
            par(1, gw, 0), par(1, N, di // N), par(1, N, di // N + G),
            par(1, LANES, 0), par(1, LANES, 0), par(1, gw, 0), par(1, gw, 0),
        ],
        out_specs=seq(gw, 0),
        out_shape=jax.ShapeDtypeStruct((b, s, di), BF16),
        scratch_shapes=[
            pltpu.VMEM((N, gw), F32),
            pltpu.VMEM((SUBLANES + L, gw), F32),
            pltpu.VMEM((SUBLANES + L, N), F32),
            pltpu.VMEM((SUBLANES + L, N), F32),
        ],
        compiler_params=_params("parallel", "parallel", "arbitrary"),
        name="ssd_scan",
    )(main, main, main, main, dt, conv_w, conv_w, conv_w, conv_b, conv_b, conv_b,
      dt_bias, a_log, d_skip, norm_g)


def _pad_heads(v, groups):
    hpg = v.shape[0] // groups
    return jnp.pad(v.reshape(groups, hpg), ((0, 0), (0, LANES - hpg))).reshape(1, groups * LANES)


def _ssd_layer(x, pending, sh, sc, gate, g_norm, w_in, conv_w, conv_b, dt_bias, a_log, d_skip, norm_g, w_out):
    G = SSM_GROUPS
    di = norm_g.shape[0]
    heads = dt_bias.shape[0]
    hpg = heads // G
    n_main = 2 * di + 2 * G * SSM_STATE
    d = w_in.shape[0]
    w_dt = w_in[:, n_main:].reshape(d, G, hpg)
    w_dt = jnp.pad(w_dt, ((0, 0), (0, 0), (0, LANES - hpg))).reshape(d, G * LANES)
    w = jnp.concatenate([w_in[:, :n_main], w_dt], axis=1).astype(BF16)
    chunks = [(0, c * COL_CHUNK, None, 1) for c in range(n_main // COL_CHUNK)] + [(1, 0, None, 1)]
    main, dt, *rest = _proj(x, sh, sc, g_norm, w, chunks, [("tok", n_main, BF16, 1), ("tok", G * LANES, F32, 1)],
                            None, pending=pending)
    if pending is not None:
        x = rest[0]
    y = _ssd_scan(main, dt, conv_w, conv_b.reshape(1, -1), _pad_heads(dt_bias, G), _pad_heads(a_log, G),
                  jnp.repeat(d_skip, SSM_HEAD_DIM).reshape(1, -1), norm_g.reshape(1, -1))
    return _outproj(y, w_out.astype(BF16), x, gate)


def _moe_pre_kernel(x_ref, sh_ref, sc_ref, g_ref, rwt_ref, rb_ref, h_ref, cls_ref, w_ref, cnt_ref):
    first = (pl.program_id(0) == 0) & (pl.program_id(1) == 0)

    @pl.when(first)
    def _():
        cnt_ref[...] = jnp.zeros(cnt_ref.shape, F32)

    tm, d = x_ref.shape[1], x_ref.shape[2]
    pieces = d // LANES
    h = _modulated_norm(x_ref[0], g_ref[...], sc_ref[0], sh_ref[0])
    half = pieces // 2
    for s in range(half):
        h_ref[pl.ds(s, tm, stride=half), :] = _pack_bf16_pair(h[:, s * LANES:(s + 1) * LANES],
                                                              h[:, (half + s) * LANES:(half + s + 1) * LANES])
    logits = lax.dot_general(rwt_ref[...], h, _NT, preferred_element_type=F32, precision=HIGHEST)
    scores = jax.nn.sigmoid(logits)
    biased = scores + rb_ref[...]
    row = lambda a, e: a[e:e + 1, :]
    epg = EXPERTS_PER_GROUP

    best, g_sel = None, None
    for g in range(N_EXPERT_GROUPS):
        v = [row(biased, g * epg + j) for j in range(epg)]
        gs = None
        for i in range(epg):
            for j in range(i + 1, epg):
                pair = v[i] + v[j]
                gs = pair if gs is None else jnp.maximum(gs, pair)
        if best is None:
            best, g_sel = gs, jnp.zeros(gs.shape, I32)
        else:
            better = gs > best
            best = jnp.where(better, gs, best)
            g_sel = jnp.where(better, g, g_sel)

    def pick(a, j):
        out = row(a, j)
        for g in range(1, N_EXPERT_GROUPS):
            out = jnp.where(g_sel == g, row(a, g * epg + j), out)
        return out

    vb = [pick(biased, j) for j in range(epg)]
    vs = [pick(scores, j) for j in range(epg)]

    def argmax_first(vals, exclude):
        bv, bi, bs = None, None, None
        for j in range(epg):
            cand = vals[j] if exclude is None else jnp.where(exclude == j, -jnp.inf, vals[j])
            if bv is None:
                bv, bi, bs = cand, jnp.zeros(cand.shape, I32), vs[0]
            else:
                better = cand > bv
                bv = jnp.where(better, cand, bv)
                bi = jnp.where(better, j, bi)
                bs = jnp.where(better, vs[j], bs)
        return bi, bs

    i1, s1 = argmax_first(vb, None)
    i2, s2 = argmax_first(vb, i1)
    tot = s1 + s2
    first_lo = i1 < i2
    lo = jnp.where(first_lo, i1, i2)
    hi = jnp.where(first_lo, i2, i1)
    pair_base = jnp.where(lo == 0, 0, jnp.where(lo == 1, 3, 5))
    cls = g_sel * PAIRS_PER_GROUP + pair_base + (hi - lo - 1)
    cls_ref[...] = cls
    w_ref[...] = jnp.concatenate([jnp.where(first_lo, s1, s2) / tot, jnp.where(first_lo, s2, s1) / tot], axis=0)
    cid = lax.broadcasted_iota(I32, (cnt_ref.shape[0], cls.shape[1]), 0)
    cnt_ref[...] += jnp.sum((cid == cls).astype(F32), axis=1, keepdims=True)


def _moe_pre(x, sh, sc, g_norm, router_w, router_bias, tm=512):
    b, s, d = x.shape
    tm = min(tm, s)
    nt = s // tm
    t_all = b * s
    e = router_w.shape[1]
    pieces = d // LANES
    return pl.pallas_call(
        _moe_pre_kernel,
        grid=(b, nt),
        in_specs=[
            pl.BlockSpec((1, tm, d), lambda bi, i: (bi, i, 0)),
            pl.BlockSpec((1, 1, d), lambda bi, i: (bi, 0, 0)),
            pl.BlockSpec((1, 1, d), lambda bi, i: (bi, 0, 0)),
            pl.BlockSpec((1, d), lambda bi, i: (0, 0)),
            pl.BlockSpec((e, d), lambda bi, i: (0, 0)),
            pl.BlockSpec((e, 1), lambda bi, i: (0, 0)),
        ],
        out_specs=[
            pl.BlockSpec((tm * pieces // 2, LANES), lambda bi, i: (bi * nt + i, 0)),
            pl.BlockSpec((1, tm), lambda bi, i: (0, bi * nt + i)),
            pl.BlockSpec((2, tm), lambda bi, i: (0, bi * nt + i)),
            pl.BlockSpec((N_CLASSES, LANES), lambda bi, i: (0, 0)),
        ],
        out_shape=[
            jax.ShapeDtypeStruct((t_all * pieces // 2, LANES), U32),
            jax.ShapeDtypeStruct((1, t_all), I32),
            jax.ShapeDtypeStruct((2, t_all), F32),
            jax.ShapeDtypeStruct((N_CLASSES, LANES), F32),
        ],
        compiler_params=_params("arbitrary", "arbitrary"),
        name="moe_pre",
    )(x, sh, sc, g_norm, router_w.T, router_bias.reshape(e, 1))


def _moe_rank_kernel(cls_ref, cnt_ref, dest_ref, meta_ref, pstart_sc, run_sc, *, sub, n_sub, nbp):
    nc = cnt_ref.shape[0]

    @pl.when(pl.program_id(0) == 0)
    def _():
        cnt = cnt_ref[...]
        padded = jnp.ceil(cnt / MOE_BLOCK) * MOE_BLOCK
        blk = lax.broadcasted_iota(I32, (1, nbp), 1).astype(F32) * MOE_BLOCK
        running = jnp.zeros((1, LANES), F32)
        block_c = jnp.zeros((1, nbp), F32)
        for c in range(nc):
            pstart_sc[c:c + 1, :] = running
            running = running + padded[c:c + 1, :]
            block_c = block_c + (running[:, 0:1] <= blk).astype(F32)
        run_sc[...] = jnp.zeros(run_sc.shape, F32)
        block_c = jnp.minimum(block_c, nc - 1.0)
        group = jnp.floor(block_c / PAIRS_PER_GROUP)
        pair = block_c - group * PAIRS_PER_GROUP
        lo = (pair >= 3.0).astype(F32) + (pair >= 5.0).astype(F32)
        hi = jnp.where(pair == 0.0, 1.0, jnp.where((pair == 1.0) | (pair == 3.0), 2.0, 3.0))
        n_used = jnp.broadcast_to(running[:, 0:1] / MOE_BLOCK, (1, nbp))
        meta_ref[...] = jnp.concatenate(
            [group * EXPERTS_PER_GROUP + lo, group * EXPERTS_PER_GROUP + hi, n_used], axis=0).astype(I32)

    upper = (lax.broadcasted_iota(I32, (sub, sub), 0) < lax.broadcasted_iota(I32, (sub, sub), 1)).astype(BF16)
    cid = lax.broadcasted_iota(I32, (nc, sub), 0)
    for j in range(n_sub):
        oh = (cid == cls_ref[:, j * sub:(j + 1) * sub]).astype(F32)
        rank = jnp.dot(oh.astype(BF16), upper, preferred_element_type=F32)
        base = pstart_sc[:, 0:1] + run_sc[:, 0:1]
        dest = jnp.sum(oh * (rank + base), axis=0, keepdims=True)
        dest_ref[:, j * sub:(j + 1) * sub] = dest.astype(I32)
        run_sc[...] += jnp.sum(oh, axis=1, keepdims=True)


def _moe_rank(cls, cnt, n_blocks):
    t_all = cls.shape[1]
    sub = 256
    tr = min(2048, t_all)
    nbp = -(-n_blocks // LANES) * LANES
    nc = cnt.shape[0]
    return pl.pallas_call(
        functools.partial(_moe_rank_kernel, sub=sub, n_sub=tr // sub, nbp=nbp),
        grid=(t_all // tr,),
        in_specs=[
            pl.BlockSpec((1, tr), lambda i: (0, i)),
            pl.BlockSpec((nc, LANES), lambda i: (0, 0)),
        ],
        out_specs=[
            pl.BlockSpec((1, tr), lambda i: (0, i)),
            pl.BlockSpec((3, nbp), lambda i: (0, 0)),
        ],
        out_shape=[
            jax.ShapeDtypeStruct((1, t_all), I32),
            jax.ShapeDtypeStruct((3, nbp), I32),
        ],
        scratch_shapes=[pltpu.VMEM((nc, LANES), F32), pltpu.VMEM((nc, LANES), F32)],
        compiler_params=_params("arbitrary"),
        name="moe_rank",
    )(cls, cnt)


def _tile_copy(src_ref, s, dst_ref, d, sem, rows):
    return pltpu.make_async_copy(src_ref.at[pl.ds(pl.multiple_of(s * rows, rows), rows)],
                                 dst_ref.at[pl.ds(pl.multiple_of(d * rows, rows), rows)], sem)


def _moe_dispatch_kernel(dest_hbm, h_ref, xs_in, xs_out, idx_smem, idx_sem, sem, *, tm, pieces):
    del xs_in
    i = pl.program_id(0)
    cp = pltpu.make_async_copy(dest_hbm.at[i], idx_smem, idx_sem)
    cp.start()
    cp.wait()

    def issue(t, carry):
        _tile_copy(h_ref, t, xs_out, idx_smem[0, t], sem, pieces).start()
        return carry

    lax.fori_loop(0, tm, issue, 0, unroll=8)

    def drain(t, carry):
        _tile_copy(h_ref, 0, xs_out, 0, sem, pieces).wait()
        return carry

    lax.fori_loop(0, tm, drain, 0, unroll=8)


def _moe_dispatch(dest_tiles, h, rows, tm, pieces):
    t_all = h.shape[0] // pieces
    zeros = jnp.zeros((rows * pieces, LANES), U32)
    return pl.pallas_call(
        functools.partial(_moe_dispatch_kernel, tm=tm, pieces=pieces),
        grid=(t_all // tm,),
        in_specs=[
            pl.BlockSpec(memory_space=pl.ANY),
            pl.BlockSpec((tm * pieces, LANES), lambda i: (i, 0)),
            pl.BlockSpec(memory_space=pl.ANY),
        ],
        out_specs=pl.BlockSpec(memory_space=pl.ANY),
        out_shape=jax.ShapeDtypeStruct((rows * pieces, LANES), U32),
        scratch_shapes=[pltpu.SMEM((1, tm), I32), pltpu.SemaphoreType.DMA(()), pltpu.SemaphoreType.DMA(())],
        input_output_aliases={2: 0},
        compiler_params=_params("arbitrary"),
        name="moe_dispatch",
    )(dest_tiles, h, zeros)


def _moe_expert_kernel(meta_ref, x_ref, wga_ref, wua_ref, wda_ref, wgb_ref, wub_ref, wdb_ref, y_ref, xb_sc,
                       *, pieces):
    i = pl.program_id(0)
    rows = xb_sc.shape[0]

    @pl.when(i < meta_ref[2, 0])
    def _():
        half = pieces // 2
        for s in range(half):
            first, second = _unpack_bf16_pair(x_ref[pl.ds(s, rows, stride=half), :])
            xb_sc[:, s * LANES:(s + 1) * LANES] = first.astype(BF16)
            xb_sc[:, (half + s) * LANES:(half + s + 1) * LANES] = second.astype(BF16)
        x = xb_sc[...]
        halves = []
        for wg_ref, wu_ref, wd_ref in ((wga_ref, wua_ref, wda_ref), (wgb_ref, wub_ref, wdb_ref)):
            gate = jnp.dot(x, wg_ref[0], preferred_element_type=F32)
            up = jnp.dot(x, wu_ref[0], preferred_element_type=F32)
            hidden = (_silu(gate) * up).astype(BF16)
            y = jnp.dot(hidden, wd_ref[0], preferred_element_type=F32)
            halves.append(y)
        word = _pack_bf16_pair(halves[0], halves[1])
        for s in range(pieces):
            y_ref[pl.ds(s, rows, stride=pieces), :] = word[:, s * LANES:(s + 1) * LANES]

    @pl.when(i >= meta_ref[2, 0])
    def _():
        y_ref[...] = jnp.zeros(y_ref.shape, U32)


def _moe_experts(meta, xs, w_gate, w_up, w_down, n_blocks, pieces):
    d, f = w_gate.shape[1], w_gate.shape[2]
    wspec = lambda k, a, c: pl.BlockSpec((1, a, c), lambda i, meta: (meta[k, i], 0, 0))
    return pl.pallas_call(
        functools.partial(_moe_expert_kernel, pieces=pieces),
        grid_spec=pltpu.PrefetchScalarGridSpec(
            num_scalar_prefetch=1,
            grid=(n_blocks,),
            in_specs=[pl.BlockSpec((MOE_BLOCK * pieces // 2, LANES), lambda i, meta: (i, 0)),
                      wspec(0, d, f), wspec(0, d, f), wspec(0, f, d),
                      wspec(1, d, f), wspec(1, d, f), wspec(1, f, d)],
            out_specs=pl.BlockSpec((MOE_BLOCK * pieces, LANES), lambda i, meta: (i, 0)),
            scratch_shapes=[pltpu.VMEM((MOE_BLOCK, d), BF16)],
        ),
        out_shape=jax.ShapeDtypeStruct((2 * xs.shape[0], LANES), U32),
        compiler_params=_params("arbitrary"),
        name="moe_experts",
    )(meta, xs, w_gate, w_up, w_down, w_gate, w_up, w_down)


def _combine_tile(dest_hbm, y_hbm, w_ref, x_ref, g_ref, o_ref, y_sc, idx_smem, idx_sem, sem,
                  *, tile, n_tiles, tm, pieces, n_parts=1):
    slot = tile % 2
    other = 1 - slot

    def idx_copy(tl, sl):
        return pltpu.make_async_copy(dest_hbm.at[tl], idx_smem.at[pl.ds(sl, 1)], idx_sem.at[sl])

    def gather(sl, lo, hi):
        def issue(t, carry):
            _tile_copy(y_hbm, idx_smem[sl, t], y_sc.at[sl], t, sem.at[sl], pieces).start()
            return carry

        lax.fori_loop(lo, hi, issue, 0, unroll=8)

    @pl.when(tile == 0)
    def _():
        idx_copy(0, 0).start()
        idx_copy(0, 0).wait()
        gather(0, 0, tm)
        if n_tiles > 1:
            idx_copy(1, 1).start()

    has_next = tile + 1 < n_tiles

    @pl.when(has_next)
    def _():
        idx_copy(tile + 1, other).wait()

    def drain(sl):
        def wait_one(t, carry):
            _tile_copy(y_hbm, 0, y_sc.at[sl], 0, sem.at[sl], pieces).wait()
            return carry

        lax.fori_loop(0, tm, wait_one, 0, unroll=8)

    def issue_part(j):
        if n_parts == 1:
            @pl.when(has_next)
            def _():
                gather(other, 0, tm)
        elif n_tiles > 1:
            for t in range(j * tm // n_parts, (j + 1) * tm // n_parts):
                _tile_copy(y_hbm, idx_smem[other, t], y_sc.at[other], t, sem.at[other], pieces).start()
        if j == n_parts - 1:
            @pl.when(tile + 2 < n_tiles)
            def _():
                idx_copy(tile + 2, slot).start()

            if n_parts > 1 and n_tiles > 1:
                @pl.when(tile + 1 == n_tiles)
                def _():
                    drain(other)

    if n_parts == 1:
        issue_part(0)
    drain(slot)
    w = w_ref[...]
    w_lo, w_hi = w[:, 0:1], w[:, 1:2]
    for s in range(pieces):
        cols = slice(s * LANES, (s + 1) * LANES)
        y_lo, y_hi = _unpack_bf16_pair(y_sc[slot, pl.ds(s, tm, stride=pieces), :])
        moe = w_lo * y_lo + w_hi * y_hi
        o_ref[0, :, cols] = x_ref[0, :, cols] + g_ref[0, :, cols] * moe
    return issue_part


def _moe_combine_kernel(dest_hbm, y_hbm, w_ref, x_ref, g_ref, o_ref, y_sc, idx_smem, idx_sem, sem,
                        *, tm, nt, n_tiles, pieces):
    _combine_tile(dest_hbm, y_hbm, w_ref, x_ref, g_ref, o_ref, y_sc, idx_smem, idx_sem, sem,
                  tile=pl.program_id(0) * nt + pl.program_id(1), n_tiles=n_tiles, tm=tm, pieces=pieces)


def _combine_scratch(tm, pieces):
    return [pltpu.VMEM((2, tm * pieces, LANES), U32), pltpu.SMEM((2, tm), I32),
            pltpu.SemaphoreType.DMA((2,)), pltpu.SemaphoreType.DMA((2,))]


def _moe_combine(dest_tiles, y, w_col, x, gate, tm, pieces):
    b, s, d = x.shape
    nt = s // tm
    return pl.pallas_call(
        functools.partial(_moe_combine_kernel, tm=tm, nt=nt, n_tiles=b * nt, pieces=pieces),
        grid=(b, nt),
        in_specs=[
            pl.BlockSpec(memory_space=pl.ANY),
            pl.BlockSpec(memory_space=pl.ANY),
            pl.BlockSpec((tm, 2), lambda bi, i: (bi * nt + i, 0)),
            pl.BlockSpec((1, tm, d), lambda bi, i: (bi, i, 0)),
            pl.BlockSpec((1, 1, d), lambda bi, i: (bi, 0, 0)),
        ],
        out_specs=pl.BlockSpec((1, tm, d), lambda bi, i: (bi, i, 0)),
        out_shape=jax.ShapeDtypeStruct((b, s, d), F32),
        scratch_shapes=_combine_scratch(tm, pieces),
        compiler_params=_params("arbitrary", "arbitrary"),
        name="moe_combine",
    )(dest_tiles, y, w_col, x, gate)


def _moe_layer(x, sh, sc, gate, g_norm, router_w, router_bias, w_gate, w_up, w_down):
    b, s, d = x.shape
    t_all = b * s
    pieces = d // LANES
    tm = min(512, s)
    n_blocks = -(-t_all // MOE_BLOCK) + N_CLASSES
    rows = n_blocks * MOE_BLOCK
    h, cls, w, cnt = _moe_pre(x, sh, sc, g_norm, router_w, router_bias)
    dest, meta = _moe_rank(cls, cnt, n_blocks)
    dest_tiles = dest.reshape(t_all // tm, 1, tm)
    xs = _moe_dispatch(dest_tiles, h, rows, tm, pieces // 2)
    y = _moe_experts(meta, xs, w_gate.astype(BF16), w_up.astype(BF16), w_down.astype(BF16), n_blocks, pieces)
    return dest_tiles, y, w.T, gate


def _final_norm_kernel(x_ref, g_ref, o_ref):
    x = x_ref[0]
    ms = jnp.mean(x * x, axis=-1, keepdims=True)
    o_ref[0] = x * lax.rsqrt(ms + NORM_EPS) * g_ref[...]


def _final_norm(x, g, tm=1024):
    b, s, d = x.shape
    tm = min(tm, s)
    return pl.pallas_call(
        _final_norm_kernel,
        grid=(b, s // tm),
        in_specs=[
            pl.BlockSpec((1, tm, d), lambda bi, i: (bi, i, 0)),
            pl.BlockSpec((1, d), lambda bi, i: (0, 0)),
        ],
        out_specs=pl.BlockSpec((1, tm, d), lambda bi, i: (bi, i, 0)),
        out_shape=jax.ShapeDtypeStruct((b, s, d), F32),
        compiler_params=_params("parallel", "arbitrary"),
        name="final_norm",
    )(x, g.reshape(1, d))


def kernel(x, c, ada_w, ada_b, norm1_g, norm2_g, router_w, router_bias, moe_w_gate, moe_w_up, moe_w_down, dil_w_in, dil_w_out, diff_w_in, diff_lam_q1, diff_lam_k1, diff_lam_q2, diff_lam_k2, diff_head_norm_g, diff_w_out, ssm_w_in, ssm_conv_w, ssm_conv_b, ssm_dt_bias, ssm_A_log, ssm_D, ssm_norm_g, ssm_w_out, final_norm_g):
    b, s, d = x.shape
    depth = ada_w.shape[0]
    mod = _ada_mod(c, ada_w, ada_b).reshape(depth, b, 6, 1, d)
    tables = _rope_tables(s)
    pending = None
    for i in range(depth):
        sh1, sc1, g1, sh2, sc2, g2 = (mod[i, :, j] for j in range(6))
        n1 = norm1_g[i].reshape(1, d)
        kind, j = i % 3, i // 3
        if kind == 0:
            x = _dilated_layer(x, pending, sh1, sc1, g1, n1, dil_w_in[j], dil_w_out[j], tables)
        elif kind == 1:
            x = _diff_layer(x, pending, sh1, sc1, g1, n1, diff_w_in[j], diff_lam_q1[j], diff_lam_k1[j],
                            diff_lam_q2[j], diff_lam_k2[j], diff_head_norm_g[j], diff_w_out[j], tables, i)
        else:
            x = _ssd_layer(x, pending, sh1, sc1, g1, n1, ssm_w_in[j], ssm_conv_w[j], ssm_conv_b[j],
                           ssm_dt_bias[j], ssm_A_log[j], ssm_D[j], ssm_norm_g[j], ssm_w_out[j])
        pending = _moe_layer(x, sh2, sc2, g2, norm2_g[i].reshape(1, d), router_w, router_bias,
                             moe_w_gate[i], moe_w_up[i], moe_w_down[i])
    dest_tiles, y, w_col, gate = pending
    x = _moe_combine(dest_tiles, y, w_col, x, gate, dest_tiles.shape[-1], d // LANES)
    return _final_norm(x, final_norm_g)
```

```python
import functools
import math

import jax
import jax.numpy as jnp
from jax import lax
from jax.experimental import pallas as pl
from jax.experimental.pallas import tpu as pltpu

F32 = jnp.float32
BF16 = jnp.bfloat16
I32 = jnp.int32
U32 = jnp.uint32
HIGHEST = lax.Precision.HIGHEST

LANES = 128
SUBLANES = 8
VMEM_LIMIT_BYTES = 56 * 1024 * 1024

NORM_EPS = 1e-6
ROPE_THETA = 500000.0
ROPE_FRACTION = 4
HEAD_DIM = 64
ATTN_BLOCK = 128
DIL_CONFIGS = ((128, 1), (512, 4), (2048, 16))
DIFF_NORM_EPS = 1e-5
SSM_HEAD_DIM = 64
SSM_GROUPS = 4
SSM_STATE = 128
SSM_CONV = 4
SSM_CHUNK = 256
N_EXPERTS = 16
N_EXPERT_GROUPS = 4
EXPERTS_PER_GROUP = 4
PAIRS_PER_GROUP = 6
N_CLASSES = N_EXPERT_GROUPS * PAIRS_PER_GROUP
MOE_BLOCK = 256
COL_CHUNK = 512
NEG = -1e30

_NT = (((1,), (1,)), ((), ()))
_TN = (((0,), (0,)), ((), ()))


def _params(*sem):
    return pltpu.CompilerParams(dimension_semantics=sem, vmem_limit_bytes=VMEM_LIMIT_BYTES)


def _silu(v):
    return v * (0.5 * jnp.tanh(0.5 * v) + 0.5)


def _pack_bf16_pair(a, b):
    hi = lax.bitcast_convert_type(a.astype(BF16).astype(F32), U32)
    lo = lax.bitcast_convert_type(b.astype(BF16).astype(F32), U32)
    return hi | (lo >> 16)


def _unpack_bf16_pair(word):
    return (lax.bitcast_convert_type(word & jnp.uint32(0xFFFF0000), F32),
            lax.bitcast_convert_type(word << 16, F32))


def _ada_kernel(c_ref, w_ref, b_ref, o_ref):
    cond = _silu(c_ref[...])
    o_ref[0] = jnp.dot(cond, w_ref[0], preferred_element_type=F32, precision=HIGHEST) + b_ref[0]


def _ada_mod(c, ada_w, ada_b):
    depth, d, n = ada_w.shape
    b = c.shape[0]
    tn = 1536
    return pl.pallas_call(
        _ada_kernel,
        grid=(depth, n // tn),
        in_specs=[
            pl.BlockSpec((b, d), lambda i, j: (0, 0)),
            pl.BlockSpec((1, d, tn), lambda i, j: (i, 0, j)),
            pl.BlockSpec((1, 1, tn), lambda i, j: (i, 0, j)),
        ],
        out_specs=pl.BlockSpec((1, b, tn), lambda i, j: (i, 0, j)),
        out_shape=jax.ShapeDtypeStruct((depth, b, n), F32),
        compiler_params=_params("arbitrary", "arbitrary"),
        name="ada_mod",
    )(c, ada_w, ada_b.reshape(depth, 1, n))


def _rope_tables(seq):
    r = HEAD_DIM // ROPE_FRACTION
    half = r // 2
    inv = jnp.power(ROPE_THETA, -jnp.arange(half, dtype=F32) * 2.0 / r)
    ang = jnp.arange(seq, dtype=F32)[:, None] * inv[None, :]
    cos, sin = jnp.cos(ang), jnp.sin(ang)
    ones = jnp.ones((seq, HEAD_DIM - r), F32)
    zeros = jnp.zeros((seq, HEAD_DIM - r), F32)
    zh = jnp.zeros((seq, half), F32)
    cos_t = jnp.concatenate([cos, cos, ones], axis=1)
    sin_a = jnp.concatenate([zh, sin, zeros], axis=1)
    sin_b = jnp.concatenate([-sin, zh, zeros], axis=1)
    rep = LANES // HEAD_DIM
    lane_form = tuple(jnp.tile(t, (1, rep)) for t in (cos_t, sin_a, sin_b))
    return lane_form, tuple(t.T for t in lane_form)


def _modulated_norm(x, g, sc, sh):
    ms = jnp.mean(x * x, axis=-1, keepdims=True)
    return x * lax.rsqrt(ms + NORM_EPS) * g * (1.0 + sc) + sh


def _rope(a, cos, sa, sb, scale, axis):
    half = HEAD_DIM // ROPE_FRACTION // 2
    r = a * cos + pltpu.roll(a, half, axis) * sa + pltpu.roll(a, LANES - half, axis) * sb
    return r if scale == 1.0 else r * scale


def _proj_kernel(*refs, chunks, tplan, n_out, use_rope, has_perm, pending, n_tiles):
    x_ref, sh_ref, sc_ref, g_ref, w_ref = refs[:5]
    pos = 5
    if use_rope:
        cos_ref, sa_ref, sb_ref = refs[pos:pos + 3]
        pos += 3
    if tplan is not None:
        wt_ref, cost_ref, sat_ref, sbt_ref = refs[pos:pos + 4]
        pos += 4
    if pending:
        dest_hbm, y_hbm, wcol_ref, g2_ref = refs[pos:pos + 4]
        pos += 4
    out_refs = refs[pos:pos + n_out]
    pos += n_out
    if pending:
        xnew_ref = refs[pos]
        pos += 1
    perm_sc = None
    if has_perm:
        perm_sc = refs[pos]
        pos += 1
    tm = x_ref.shape[1]
    issue_part = None
    if pending:
        issue_part = _combine_tile(dest_hbm, y_hbm, wcol_ref, x_ref, g2_ref, xnew_ref, *refs[pos:pos + 4],
                                   tile=pl.program_id(0) * pl.num_programs(1) + pl.program_id(1),
                                   n_tiles=n_tiles, tm=tm,
                                   pieces=x_ref.shape[2] // LANES, n_parts=len(chunks))
        x_ref = xnew_ref
    h = _modulated_norm(x_ref[0], g_ref[...], sc_ref[0], sh_ref[0]).astype(BF16)
    tiles_per_chunk = COL_CHUNK // LANES
    for c, (oi, off, scale, dil) in enumerate(chunks):
        if issue_part is not None:
            issue_part(c)
        o_ref = out_refs[oi]
        acc = jnp.dot(h, w_ref[:, c * COL_CHUNK:(c + 1) * COL_CHUNK], preferred_element_type=F32)
        tiles = None
        if scale is not None:
            cos, sa, sb = cos_ref[...], sa_ref[...], sb_ref[...]
            tiles = [_rope(acc[:, s * LANES:(s + 1) * LANES], cos, sa, sb, scale, 1)
                     for s in range(tiles_per_chunk)]
        if dil == 1:
            if tiles is None:
                o_ref[0, :, off:off + COL_CHUNK] = acc.astype(o_ref.dtype)
            else:
                for s, tl in enumerate(tiles):
                    o_ref[0, :, off + s * LANES:off + (s + 1) * LANES] = tl.astype(o_ref.dtype)
            continue
        if tiles is None:
            tiles = [acc[:, s * LANES:(s + 1) * LANES] for s in range(tiles_per_chunk)]
        for s, tl in enumerate(tiles):
            perm_sc[s] = tl
        for rho in range(dil):
            for s in range(tiles_per_chunk):
                o_ref[0, rho, :, off + s * LANES:off + (s + 1) * LANES] = (
                    perm_sc[s, pl.ds(rho, tm // dil, stride=dil), :].astype(o_ref.dtype))
    if tplan is not None:
        oi, scale = tplan
        o_ref = out_refs[oi]
        acct = lax.dot_general(wt_ref[...], h, _NT, preferred_element_type=F32)
        cost, sat, sbt = cost_ref[...], sat_ref[...], sbt_ref[...]
        for s in range(acct.shape[0] // LANES):
            rows = slice(s * LANES, (s + 1) * LANES)
            o_ref[0, rows, :] = _rope(acct[rows, :], cost, sat, sbt, scale, 0).astype(o_ref.dtype)


def _proj(x, sh, sc, g, w, chunks, outs, tables, wt=None, tplan=None, pending=None, tm=512):
    b, s, d = x.shape
    n = w.shape[1]
    tm = min(tm, s)
    use_rope = any(c[2] is not None for c in chunks)
    has_perm = any(c[3] > 1 for c in chunks)
    in_specs = [
        pl.BlockSpec((1, tm, d), lambda bi, i: (bi, i, 0)),
        pl.BlockSpec((1, 1, d), lambda bi, i: (bi, 0, 0)),
        pl.BlockSpec((1, 1, d), lambda bi, i: (bi, 0, 0)),
        pl.BlockSpec((1, d), lambda bi, i: (0, 0)),
        pl.BlockSpec((d, n), lambda bi, i: (0, 0), pipeline_mode=pl.Buffered(1)),
    ]
    args = [x, sh, sc, g, w]
    if use_rope:
        in_specs += [pl.BlockSpec((tm, LANES), lambda bi, i: (i, 0))] * 3
        args += list(tables[0])
    if tplan is not None:
        in_specs.append(pl.BlockSpec(wt.shape, lambda bi, i: (0, 0), pipeline_mode=pl.Buffered(1)))
        in_specs += [pl.BlockSpec((LANES, tm), lambda bi, i: (0, i))] * 3
        args += [wt] + list(tables[1])
    if pending is not None:
        dest_tiles, y, w_col, gate2 = pending
        assert dest_tiles.shape[-1] == tm
        nt = s // tm
        in_specs += [pl.BlockSpec(memory_space=pl.ANY), pl.BlockSpec(memory_space=pl.ANY),
                     pl.BlockSpec((tm, 2), lambda bi, i: (bi * nt + i, 0)),
                     pl.BlockSpec((1, 1, d), lambda bi, i: (bi, 0, 0))]
        args += [dest_tiles, y, w_col, gate2]
    out_specs, out_shape = [], []
    for layout, wd, dt, dil in outs:
        if layout == "tok":
            out_specs.append(pl.BlockSpec((1, tm, wd), lambda bi, i: (bi, i, 0)))
            out_shape.append(jax.ShapeDtypeStruct((b, s, wd), dt))
        elif layout == "res":
            out_specs.append(pl.BlockSpec((1, dil, tm // dil, wd), lambda bi, i: (bi, 0, i, 0)))
            out_shape.append(jax.ShapeDtypeStruct((b, dil, s // dil, wd), dt))
        else:
            out_specs.append(pl.BlockSpec((1, wd, tm), lambda bi, i: (bi, 0, i)))
            out_shape.append(jax.ShapeDtypeStruct((b, wd, s), dt))
    scratch = [pltpu.VMEM((COL_CHUNK // LANES, tm, LANES), F32)] if has_perm else []
    if pending is not None:
        out_specs.append(pl.BlockSpec((1, tm, d), lambda bi, i: (bi, i, 0)))
        out_shape.append(jax.ShapeDtypeStruct((b, s, d), F32))
        scratch += _combine_scratch(tm, d // LANES)
    return pl.pallas_call(
        functools.partial(_proj_kernel, chunks=tuple(chunks), tplan=tplan, n_out=len(outs),
                          use_rope=use_rope, has_perm=has_perm, pending=pending is not None,
                          n_tiles=b * (s // tm)),
        grid=(b, s // tm),
        in_specs=in_specs,
        out_specs=out_specs,
        out_shape=out_shape,
        scratch_shapes=scratch,
        compiler_params=_params(*(("arbitrary", "arbitrary") if pending is not None else ("parallel", "arbitrary"))),
        name="norm_proj",
    )(*args)


def _outproj_kernel(y_ref, w_ref, x_ref, g_ref, o_ref):
    y = jnp.dot(y_ref[0], w_ref[...], preferred_element_type=F32)
    o_ref[0] = x_ref[0] + g_ref[0] * y


def _outproj(y, w, x, gate, tm=512):
    b, s, d = x.shape
    k = y.shape[-1]
    tm = min(tm, s)
    return pl.pallas_call(
        _outproj_kernel,
        grid=(b, s // tm),
        in_specs=[
            pl.BlockSpec((1, tm, k), lambda bi, i: (bi, i, 0)),
            pl.BlockSpec((k, d), lambda bi, i: (0, 0), pipeline_mode=pl.Buffered(1)),
            pl.BlockSpec((1, tm, d), lambda bi, i: (bi, i, 0)),
            pl.BlockSpec((1, 1, d), lambda bi, i: (bi, 0, 0)),
        ],
        out_specs=pl.BlockSpec((1, tm, d), lambda bi, i: (bi, i, 0)),
        out_shape=jax.ShapeDtypeStruct((b, s, d), F32),
        compiler_params=_params("parallel", "arbitrary"),
        name="out_proj",
    )(y, w, x, gate)


def _dil_kernel(q_ref, kc_ref, kp_ref, vc_ref, vp_ref, o_ref, lse_ref, kbuf, vbuf, *, tq, back, heads):
    n = pl.program_id(2)
    qb = ATTN_BLOCK
    kbuf[0:qb] = kp_ref[0, 0]
    kbuf[qb:] = kc_ref[0, 0]
    vbuf[0:qb] = vp_ref[0, 0]
    vbuf[qb:] = vc_ref[0, 0]
    rows = heads * qb
    qi = lax.broadcasted_iota(I32, (rows, 2 * qb), 0) & (qb - 1)
    kj = lax.broadcasted_iota(I32, (rows, 2 * qb), 1)
    rel = kj - qi
    band = (rel >= qb - back) & (rel <= qb)
    lane = lax.broadcasted_iota(I32, (qb, LANES), 1)
    low_half = lane < HEAD_DIM
    zero_q = jnp.zeros((qb, LANES), BF16)
    ones_k = jnp.ones((2 * qb, LANES), BF16)
    for j in range(tq // qb):
        first_key = n * tq + (j - 1) * qb
        valid = band & (kj + first_key >= 0)
        parts = []
        for hp in range(heads // 2):
            cols = slice(hp * LANES, (hp + 1) * LANES)
            q2 = q_ref[0, 0, j * qb:(j + 1) * qb, cols]
            q_stack = jnp.concatenate([jnp.where(low_half, q2, zero_q), jnp.where(low_half, zero_q, q2)], axis=0)
            parts.append(lax.dot_general(q_stack, kbuf[j * qb:(j + 2) * qb, cols], _NT,
                                         preferred_element_type=F32))
        s = jnp.where(valid, jnp.concatenate(parts, axis=0), NEG)
        m = jnp.max(s, axis=-1, keepdims=True)
        pb = jnp.exp2(s - m).astype(BF16)
        l = jnp.dot(pb, ones_k, preferred_element_type=F32)
        inv = 1.0 / l
        lse = (jnp.broadcast_to(m, l.shape) + jnp.log2(l)) * math.log(2.0)
        lse_tile = jnp.zeros((qb, LANES), F32)
        for hp in range(heads // 2):
            cols = slice(hp * LANES, (hp + 1) * LANES)
            ra = slice(2 * hp * qb, (2 * hp + 1) * qb)
            rb = slice((2 * hp + 1) * qb, (2 * hp + 2) * qb)
            pv = jnp.dot(pb[2 * hp * qb:(2 * hp + 2) * qb], vbuf[j * qb:(j + 2) * qb, cols],
                         preferred_element_type=F32)
            o = jnp.where(low_half, pv[:qb] * inv[ra], pv[qb:] * inv[rb])
            o_ref[0, 0, j * qb:(j + 1) * qb, cols] = o.astype(o_ref.dtype)
            lse_tile = jnp.where(lane == 2 * hp, lse[ra], lse_tile)
            lse_tile = jnp.where(lane == 2 * hp + 1, lse[rb], lse_tile)
        lse_ref[0, 0, j * qb:(j + 1) * qb, :] = lse_tile


def _dil_group(proj, window, dilation):
    b, dil, ln, c = proj.shape
    back = window // dilation
    assert dil == dilation and back <= ATTN_BLOCK and ln % ATTN_BLOCK == 0
    width = c // 3
    heads = width // HEAD_DIM
    tq = min(512, ln)
    sub = tq // ATTN_BLOCK

    def cur(col):
        return pl.BlockSpec((1, 1, tq, width), lambda bi, r, n: (bi, r, n, col))

    def prev(col):
        return pl.BlockSpec((1, 1, ATTN_BLOCK, width),
                            lambda bi, r, n: (bi, r, jnp.maximum(n * sub - 1, 0), col))

    return pl.pallas_call(
        functools.partial(_dil_kernel, tq=tq, back=back, heads=heads),
        grid=(b, dil, ln // tq),
        in_specs=[cur(0), cur(1), prev(1), cur(2), prev(2)],
        out_specs=[
            pl.BlockSpec((1, 1, tq, width), lambda bi, r, n: (bi, r, n, 0)),
            pl.BlockSpec((1, 1, tq, LANES), lambda bi, r, n: (bi, r, n, 0)),
        ],
        out_shape=[
            jax.ShapeDtypeStruct((b, dil, ln, width), BF16),
            jax.ShapeDtypeStruct((b, dil, ln, LANES), F32),
        ],
        scratch_shapes=[pltpu.VMEM((tq + ATTN_BLOCK, width), BF16)] * 2,
        compiler_params=_params("parallel", "parallel", "arbitrary"),
        name=f"dil_attn_d{dilation}",
    )(proj, proj, proj, proj, proj)


def _dil_out_kernel(o0_ref, o1_ref, o2_ref, l0_ref, l1_ref, l2_ref, w_ref, x_ref, g_ref, out_ref,
                    o_sc, l1_sc, l2_sc):
    tm = x_ref.shape[1]

    def to_token_order(src_ref, dst_sc):
        dil = src_ref.shape[1]
        for rho in range(dil):
            src = src_ref[0, rho].astype(F32)
            for s in range(dst_sc.shape[0]):
                dst_sc[s, pl.ds(rho, tm // dil, stride=dil), :] = src[:, s * LANES:(s + 1) * LANES]
        return jnp.concatenate([dst_sc[s] for s in range(dst_sc.shape[0])], axis=1)

    ls = [l0_ref[0, 0], to_token_order(l1_ref, l1_sc), to_token_order(l2_ref, l2_sc)]
    mx =jnp.maximum(jnp.maximum(ls[0], ls[1]), ls[2])
    es = [jnp.exp(v - mx) for v in ls]
    inv = 1.0 / (es[0] + es[1] + es[2])
    width = o0_ref.shape[-1]
    expand = (lax.broadcasted_iota(I32, (LANES, width), 0)
              == lax.broadcasted_iota(I32, (LANES, width), 1) // HEAD_DIM).astype(BF16)
    o = jnp.zeros((tm, width), F32)
    for gi, (e, o_ref) in enumerate(zip(es, (o0_ref, o1_ref, o2_ref))):
        alpha = e * inv
        hi = alpha.astype(BF16)
        lo = (alpha - hi.astype(F32)).astype(BF16)
        a_full = (jnp.dot(hi, expand, preferred_element_type=F32)
                  + jnp.dot(lo, expand, preferred_element_type=F32))
        og = o_ref[0, 0].astype(F32) if gi == 0 else to_token_order(o_ref, o_sc)
        o = o + a_full * og
    y = jnp.dot(o.astype(BF16), w_ref[...], preferred_element_type=F32)
    out_ref[0] = x_ref[0] + g_ref[0] * y


def _dil_out(os_, lses, w, x, gate, tm=512):
    b, s, d = x.shape
    width = os_[0].shape[-1]
    tm = min(tm, s)
    tok = lambda wd: pl.BlockSpec((1, tm, wd), lambda bi, i: (bi, i, 0))
    res = lambda a: pl.BlockSpec((1, a.shape[1], tm // a.shape[1], a.shape[3]), lambda bi, i: (bi, 0, i, 0))
    return pl.pallas_call(
        _dil_out_kernel,
        grid=(b, s // tm),
        in_specs=[res(a) for a in os_] + [res(a) for a in lses] + [
            pl.BlockSpec((width, d), lambda bi, i: (0, 0)),
            tok(d),
            pl.BlockSpec((1, 1, d), lambda bi, i: (bi, 0, 0)),
        ],
        out_specs=tok(d),
        out_shape=jax.ShapeDtypeStruct((b, s, d), F32),
        scratch_shapes=[pltpu.VMEM((width // LANES, tm, LANES), F32), pltpu.VMEM((1, tm, LANES), F32),
                        pltpu.VMEM((1, tm, LANES), F32)],
        compiler_params=_params("parallel", "arbitrary"),
        name="dil_out",
    )(*os_, *lses, w, x, gate)


def _dilated_layer(x, pending, sh, sc, gate, g_norm, w_in, w_out, tables):
    b, s, _ = x.shape
    n = w_in.shape[1]
    gw = n // len(DIL_CONFIGS)
    chunks, outs = [], []
    for g, (_, dilation) in enumerate(DIL_CONFIGS):
        for kind in range(3):
            scale = (HEAD_DIM ** -0.5 * math.log2(math.e), 1.0, None)[kind]
            chunks.append((g, kind * COL_CHUNK, scale, dilation))
        outs.append(("tok" if dilation == 1 else "res", gw, BF16, dilation))
    projs = list(_proj(x, sh, sc, g_norm, w_in.astype(BF16), chunks, outs, tables, pending=pending))
    if pending is not None:
        x = projs.pop()
    os_, lses = [], []
    for proj, (window, dilation) in zip(projs, DIL_CONFIGS):
        o, lse = _dil_group(proj.reshape(b, dilation, s // dilation, gw), window, dilation)
        os_.append(o)
        lses.append(lse)
    return _dil_out(os_, lses, w_out.astype(BF16), x, gate)


def _diff_kernel(qi_tab, ki_tab, q_ref, kt_ref, v_ref, lq1_ref, lk1_ref, lq2_ref, lk2_ref, hg_ref, o_ref,
                 m_sc, acc_sc, *, t, r, lam_init):
    pair = pl.program_id(2)
    qi = qi_tab[pair]
    ki = ki_tab[pair]
    vw = v_ref.shape[-1]

    @pl.when(ki == 0)
    def _():
        m_sc[...] = jnp.full(m_sc.shape, NEG, F32)
        acc_sc[...] = jnp.zeros(acc_sc.shape, F32)

    def step(diagonal, r):
        kt = kt_ref[0]
        v_aug = jnp.concatenate([v_ref[0], jnp.ones((t, LANES), BF16)], axis=1)
        lane = lax.broadcasted_iota(I32, (r, vw), 1)
        work = [(mi, c) for mi in range(2) for c in range(t // r)]

        def scores(mi, c):
            in_half = (lane >= mi * HEAD_DIM) & (lane < (mi + 1) * HEAD_DIM)
            ncols = (c + 1) * r if diagonal else t
            q_c = jnp.where(in_half, q_ref[0, c * r:(c + 1) * r, :], jnp.zeros((r, vw), BF16))
            return jnp.dot(q_c, kt[:, :ncols], preferred_element_type=F32)

        ahead = 3
        pending = [scores(*wk) for wk in work[:ahead]]
        for idx, (mi, c) in enumerate(work):
            s = pending.pop(0)
            if idx + ahead < len(work):
                pending.append(scores(*work[idx + ahead]))
            ncols = s.shape[1]
            if diagonal:
                col = lax.broadcasted_iota(I32, (r, ncols), 1)
                row = lax.broadcasted_iota(I32, (r, ncols), 0) + c * r
                s = jnp.where(col <= row, s, NEG)
            tiles = [s[:, j * LANES:(j + 1) * LANES] for j in range(ncols // LANES)]
            tmax = functools.reduce(jnp.maximum, tiles)
            srows = slice(mi * t + c * r, mi * t + (c + 1) * r)
            m_prev = m_sc[srows, :]
            m_new = jnp.maximum(m_prev, jnp.max(tmax, axis=-1, keepdims=True))
            alpha = jnp.exp2(m_prev - m_new)
            p = jnp.concatenate([jnp.exp2(tl - m_new) for tl in tiles], axis=1).astype(BF16)
            pv = jnp.dot(p, v_aug[:ncols], preferred_element_type=F32)
            acc_sc[srows, :] = jnp.concatenate([alpha, alpha], axis=1) * acc_sc[srows, :] + pv
            m_sc[srows, :] = m_new

    @pl.when(ki < qi)
    def _():
        step(False, 2 * r)

    @pl.when(ki == qi)
    def _():
        step(True, 2 * r)
        lam = (jnp.exp(jnp.sum(lq1_ref[...] * lk1_ref[...], axis=-1, keepdims=True))
               - jnp.exp(jnp.sum(lq2_ref[...] * lk2_ref[...], axis=-1, keepdims=True)) + lam_init)
        o = acc_sc[:t, :vw] / acc_sc[:t, vw:] - lam * (acc_sc[t:, :vw] / acc_sc[t:, vw:])
        ms = jnp.mean(o * o, axis=-1, keepdims=True)
        o = o * lax.rsqrt(ms + DIFF_NORM_EPS) * hg_ref[...] * (1.0 - lam_init)
        o_ref[0] = o.astype(o_ref.dtype)


def _diff_attention(q, kt, v, lam_q1, lam_k1, lam_q2, lam_k2, head_g, lam_init, t=2048, r=128):
    b, s, d = q.shape
    vw = 2 * HEAD_DIM
    assert vw == LANES
    heads = d // vw
    t = min(t, s)
    r = min(r, t // 2)
    nq = s // t
    pairs = [(qi, ki) for qi in range(nq) for ki in range(qi + 1)]
    qi_tab = jnp.asarray([p[0] for p in pairs], I32)
    ki_tab = jnp.asarray([p[1] for p in pairs], I32)
    vec = lambda n: pl.BlockSpec((1, n), lambda bi, h, p, qt, kt_: (0, 0))
    return pl.pallas_call(
        functools.partial(_diff_kernel, t=t, r=r, lam_init=lam_init),
        grid_spec=pltpu.PrefetchScalarGridSpec(
            num_scalar_prefetch=2,
            grid=(b, heads, len(pairs)),
            in_specs=[
                pl.BlockSpec((1, t, vw), lambda bi, h, p, qt, kt_: (bi, qt[p], h)),
                pl.BlockSpec((1, vw, t), lambda bi, h, p, qt, kt_: (bi, h, kt_[p])),
                pl.BlockSpec((1, t, vw), lambda bi, h, p, qt, kt_: (bi, kt_[p], h)),
                vec(HEAD_DIM), vec(HEAD_DIM), vec(HEAD_DIM), vec(HEAD_DIM), vec(vw),
            ],
            out_specs=pl.BlockSpec((1, t, vw), lambda bi, h, p, qt, kt_: (bi, qt[p], h)),
            scratch_shapes=[
                pltpu.VMEM((2 * t, LANES), F32),
                pltpu.VMEM((2 * t, vw + LANES), F32),
            ],
        ),
        out_shape=jax.ShapeDtypeStruct((b, s, d), BF16),
        compiler_params=_params("parallel", "parallel", "arbitrary"),
        name="diff_attn",
    )(qi_tab, ki_tab, q, kt, v, lam_q1.reshape(1, -1), lam_k1.reshape(1, -1), lam_q2.reshape(1, -1),
      lam_k2.reshape(1, -1), head_g.reshape(1, -1))


def _diff_layer(x, pending, sh, sc, gate, g_norm, w_in, lam_q1, lam_k1, lam_q2, lam_k2, head_g, w_out,
                tables, layer_idx):
    d = w_in.shape[1] // 3
    nch = d // COL_CHUNK
    q_scale = HEAD_DIM ** -0.5 * math.log2(math.e)
    chunks = ([(0, c * COL_CHUNK, q_scale, 1) for c in range(nch)]
              + [(1, c * COL_CHUNK, None, 1) for c in range(nch)])
    w_qv = jnp.concatenate([w_in[:, :d], w_in[:, 2 * d:]], axis=1).astype(BF16)
    w_kt = w_in[:, d:2 * d].T.astype(BF16)
    q, v, kt, *rest = _proj(x, sh, sc, g_norm, w_qv, chunks,
                            [("tok", d, BF16, 1), ("tok", d, BF16, 1), ("T", d, BF16, 1)], tables,
                            wt=w_kt, tplan=(2, 1.0), pending=pending)
    if pending is not None:
        x = rest[0]
    lam_init = 0.8 - 0.6 * math.exp(-0.3 * layer_idx)
    o = _diff_attention(q, kt, v, lam_q1, lam_k1, lam_q2, lam_k2, head_g, lam_init)
    return _outproj(o, w_out.astype(BF16), x, gate)


def _causal_conv_silu(cur_ref, ext_sc, w_ref, b_ref):
    cur = cur_ref[0].astype(F32)
    rows = cur.shape[0]
    ext_sc[SUBLANES:, :] = cur
    w = w_ref[...]
    acc = cur * w[SSM_CONV - 1:SSM_CONV] + b_ref[...]
    for k in range(1, SSM_CONV):
        acc = acc + ext_sc[SUBLANES - k:SUBLANES - k + rows, :] * w[SSM_CONV - 1 - k:SSM_CONV - k]
    ext_sc[0:SUBLANES, :] = cur[rows - SUBLANES:rows]
    return _silu(acc)


def _ssd_kernel(z_ref, x_ref, b_ref, c_ref, dt_ref, wx_ref, wb_ref, wc_ref, bx_ref, bb_ref, bc_ref,
                dtb_ref, alog_ref, dskip_ref, ng_ref, y_ref,
                state_sc, tx_sc, tb_sc, tc_sc, *, hpg):
    ci = pl.program_id(2)
    L = x_ref.shape[1]
    P = SSM_HEAD_DIM
    gw = hpg * P

    @pl.when(ci == 0)
    def _():
        state_sc[...] = jnp.zeros(state_sc.shape, F32)
        tx_sc[0:SUBLANES, :] = jnp.zeros((SUBLANES, tx_sc.shape[1]), F32)
        tb_sc[0:SUBLANES, :] = jnp.zeros((SUBLANES, tb_sc.shape[1]), F32)
        tc_sc[0:SUBLANES, :] = jnp.zeros((SUBLANES, tc_sc.shape[1]), F32)

    xs = _causal_conv_silu(x_ref, tx_sc, wx_ref, bx_ref)
    bm = _causal_conv_silu(b_ref, tb_sc, wb_ref, bb_ref).astype(BF16)
    cm = _causal_conv_silu(c_ref, tc_sc, wc_ref, bc_ref).astype(BF16)

    raw = dt_ref[0] + dtb_ref[...]
    dt = jnp.maximum(raw, 0.0) + jnp.log(1.0 + jnp.exp(-jnp.abs(raw)))
    a = -jnp.exp(alog_ref[...])
    da = dt * a
    ti = lax.broadcasted_iota(I32, (L, L), 0)
    si = lax.broadcasted_iota(I32, (L, L), 1)
    causal = ti >= si
    acum = jnp.dot(causal.astype(F32), da, preferred_element_type=F32, precision=HIGHEST)
    acum2 = acum * math.log2(math.e)
    acum2_t = acum2.T
    last = acum[L - 1:L, :]
    H = L // 2
    tri = causal[:H, :H]
    expand = (lax.broadcasted_iota(I32, (LANES, gw), 0)
              == lax.broadcasted_iota(I32, (LANES, gw), 1) // P).astype(BF16)

    def per_head_lanes(a):
        hi = a.astype(BF16)
        lo = (a - hi.astype(F32)).astype(BF16)
        return (jnp.dot(hi, expand, preferred_element_type=F32)
                + jnp.dot(lo, expand, preferred_element_type=F32))

    dx = per_head_lanes(dt) * xs
    dxb = dx.astype(BF16)
    eacum_e = per_head_lanes(jnp.exp(acum))
    wdx = (per_head_lanes(jnp.exp(last - acum)) * dx).astype(BF16)

    cb = lax.dot_general(cm, bm, _NT, preferred_element_type=F32)
    y_inter = jnp.dot(cm, state_sc[...].astype(BF16), preferred_element_type=F32)

    low_half = lax.broadcasted_iota(I32, (L, LANES), 1) < P
    y_pairs = []
    for hp in range(hpg // 2):
        cols = slice(hp * LANES, (hp + 1) * LANES)
        ys = []
        for k in (2 * hp, 2 * hp + 1):
            col = acum2[:, k:k + 1]
            row = acum2_t[k:k + 1, :]
            m00 = cb[:H, :H] * jnp.exp2(jnp.where(tri, col[:H] - row[:, :H], NEG))
            m10 = cb[H:, :H] * jnp.exp2(col[H:] - row[:, :H])
            m11 = cb[H:, H:] * jnp.exp2(jnp.where(tri, col[H:] - row[:, H:], NEG))
            top = jnp.dot(m00.astype(BF16), dxb[:H, cols], preferred_element_type=F32)
            bot = (jnp.dot(m10.astype(BF16), dxb[:H, cols], preferred_element_type=F32)
                   + jnp.dot(m11.astype(BF16), dxb[H:, cols], preferred_element_type=F32))
            ys.append(jnp.concatenate([top, bot], axis=0))
        y_pairs.append(jnp.where(low_half, ys[0], ys[1]))

    state_sc[...] = (state_sc[...] * eacum_e[L - 1:L, :]
                     + lax.dot_general(bm, wdx, _TN, preferred_element_type=F32))

    y = jnp.concatenate(y_pairs, axis=1) + y_inter * eacum_e + dskip_ref[...] * xs
    y = y * _silu(z_ref[0].astype(F32))
    ms = jnp.mean(y * y, axis=-1, keepdims=True)
    y_ref[0] = (y * lax.rsqrt(ms + NORM_EPS) * ng_ref[...]).astype(y_ref.dtype)


def _ssd_scan(main, dt, conv_w, conv_b, dt_bias, a_log, d_skip, norm_g):
    b, s, _ = main.shape
    G, N, P, L = SSM_GROUPS, SSM_STATE, SSM_HEAD_DIM, SSM_CHUNK
    di = norm_g.shape[-1]
    gw = di // G
    hpg = gw // P
    assert s % L == 0
    xoff = di // gw
    boff = 2 * di // N
    coff = boff + G
    seq = lambda wd, off: pl.BlockSpec((1, L, wd), lambda bi, g, c: (bi, c, off + g))
    par = lambda r, wd, off: pl.BlockSpec((r, wd), lambda bi, g, c: (0, off + g))
    return pl.pallas_call(
        functools.partial(_ssd_kernel, hpg=hpg),
        grid=(b, G, s // L),
        in_specs=[
            seq(gw, 0), seq(gw, xoff), seq(N, boff), seq(N, coff), seq(LANES, 0),
            par(SSM_CONV, gw, 0), par(SSM_CONV, N, di // N), par(SSM_CONV, N, di // N + G),
            par(1, gw, 0), par(1, N, di // N), par(1, N, di // N + G),
            par(1, LANES, 0), par(1, LANES, 0), par(1, gw, 0), par(1, gw, 0),
        ],
        out_specs=seq(gw, 0),
        out_shape=jax.ShapeDtypeStruct((b, s, di), BF16),
        scratch_shapes=[
            pltpu.VMEM((N, gw), F32),
            pltpu.VMEM((SUBLANES + L, gw), F32),
            pltpu.VMEM((SUBLANES + L, N), F32),
            pltpu.VMEM((SUBLANES + L, N), F32),
        ],
        compiler_params=_params("parallel", "parallel", "arbitrary"),
        name="ssd_scan",
    )(main, main, main, main, dt, conv_w, conv_w, conv_w, conv_b, conv_b, conv_b,
      dt_bias, a_log, d_skip, norm_g)


def _pad_heads(v, groups):
    hpg = v.shape[0] // groups
    return jnp.pad(v.reshape(groups, hpg), ((0, 0), (0, LANES - hpg))).reshape(1, groups * LANES)


def _ssd_layer(x, pending, sh, sc, gate, g_norm, w_in, conv_w, conv_b, dt_bias, a_log, d_skip, norm_g, w_out):
    G = SSM_GROUPS
    di = norm_g.shape[0]
    heads = dt_bias.shape[0]
    hpg = heads // G
    n_main = 2 * di + 2 * G * SSM_STATE
    d = w_in.shape[0]
    w_dt = w_in[:, n_main:].reshape(d, G, hpg)
    w_dt = jnp.pad(w_dt, ((0, 0), (0, 0), (0, LANES - hpg))).reshape(d, G * LANES)
    w = jnp.concatenate([w_in[:, :n_main], w_dt], axis=1).astype(BF16)
    chunks = [(0, c * COL_CHUNK, None, 1) for c in range(n_main // COL_CHUNK)] + [(1, 0, None, 1)]
    main, dt, *rest = _proj(x, sh, sc, g_norm, w, chunks, [("tok", n_main, BF16, 1), ("tok", G * LANES, F32, 1)],
                            None, pending=pending)
    if pending is not None:
        x = rest[0]
    y = _ssd_scan(main, dt, conv_w, conv_b.reshape(1, -1), _pad_heads(dt_bias, G), _pad_heads(a_log, G),
                  jnp.repeat(d_skip, SSM_HEAD_DIM).reshape(1, -1), norm_g.reshape(1, -1))
    return _outproj(y, w_out.astype(BF16), x, gate)


def _moe_pre_kernel(x_ref, sh_ref, sc_ref, g_ref, rwt_ref, rb_ref, h_ref, cls_ref, w_ref, cnt_ref):
    first = (pl.program_id(0) == 0) & (pl.program_id(1) == 0)

    @pl.when(first)
    def _():
        cnt_ref[...] = jnp.zeros(cnt_ref.shape, F32)

    tm, d = x_ref.shape[1], x_ref.shape[2]
    pieces = d // LANES
    h = _modulated_norm(x_ref[0], g_ref[...], sc_ref[0], sh_ref[0])
    half = pieces // 2
    for s in range(half):
        h_ref[pl.ds(s, tm, stride=half), :] = _pack_bf16_pair(h[:, s * LANES:(s + 1) * LANES],
                                                              h[:, (half + s) * LANES:(half + s + 1) * LANES])
    logits = lax.dot_general(rwt_ref[...], h, _NT, preferred_element_type=F32, precision=HIGHEST)
    scores = jax.nn.sigmoid(logits)
    biased = scores + rb_ref[...]
    row = lambda a, e: a[e:e + 1, :]
    epg = EXPERTS_PER_GROUP

    best, g_sel = None, None
    for g in range(N_EXPERT_GROUPS):
        v = [row(biased, g * epg + j) for j in range(epg)]
        gs = None
        for i in range(epg):
            for j in range(i + 1, epg):
                pair = v[i] + v[j]
                gs = pair if gs is None else jnp.maximum(gs, pair)
        if best is None:
            best, g_sel = gs, jnp.zeros(gs.shape, I32)
        else:
            better = gs > best
            best = jnp.where(better, gs, best)
            g_sel = jnp.where(better, g, g_sel)

    def pick(a, j):
        out = row(a, j)
        for g in range(1, N_EXPERT_GROUPS):
            out = jnp.where(g_sel == g, row(a, g * epg + j), out)
        return out

    vb = [pick(biased, j) for j in range(epg)]
    vs = [pick(scores, j) for j in range(epg)]

    def argmax_first(vals, exclude):
        bv, bi, bs = None, None, None
        for j in range(epg):
            cand = vals[j] if exclude is None else jnp.where(exclude == j, -jnp.inf, vals[j])
            if bv is None:
                bv, bi, bs = cand, jnp.zeros(cand.shape, I32), vs[0]
            else:
                better = cand > bv
                bv = jnp.where(better, cand, bv)
                bi = jnp.where(better, j, bi)
                bs = jnp.where(better, vs[j], bs)
        return bi, bs

    i1, s1 = argmax_first(vb, None)
    i2, s2 = argmax_first(vb, i1)
    tot = s1 + s2
    first_lo = i1 < i2
    lo = jnp.where(first_lo, i1, i2)
    hi = jnp.where(first_lo, i2, i1)
    pair_base = jnp.where(lo == 0, 0, jnp.where(lo == 1, 3, 5))
    cls = g_sel * PAIRS_PER_GROUP + pair_base + (hi - lo - 1)
    cls_ref[...] = cls
    w_ref[...] = jnp.concatenate([jnp.where(first_lo, s1, s2) / tot, jnp.where(first_lo, s2, s1) / tot], axis=0)
    cid = lax.broadcasted_iota(I32, (cnt_ref.shape[0], cls.shape[1]), 0)
    cnt_ref[...] += jnp.sum((cid == cls).astype(F32), axis=1, keepdims=True)


def _moe_pre(x, sh, sc, g_norm, router_w, router_bias, tm=512):
    b, s, d = x.shape
    tm = min(tm, s)
    nt = s // tm
    t_all = b * s
    e = router_w.shape[1]
    pieces = d // LANES
    return pl.pallas_call(
        _moe_pre_kernel,
        grid=(b, nt),
        in_specs=[
            pl.BlockSpec((1, tm, d), lambda bi, i: (bi, i, 0)),
            pl.BlockSpec((1, 1, d), lambda bi, i: (bi, 0, 0)),
            pl.BlockSpec((1, 1, d), lambda bi, i: (bi, 0, 0)),
            pl.BlockSpec((1, d), lambda bi, i: (0, 0)),
            pl.BlockSpec((e, d), lambda bi, i: (0, 0)),
            pl.BlockSpec((e, 1), lambda bi, i: (0, 0)),
        ],
        out_specs=[
            pl.BlockSpec((tm * pieces // 2, LANES), lambda bi, i: (bi * nt + i, 0)),
            pl.BlockSpec((1, tm), lambda bi, i: (0, bi * nt + i)),
            pl.BlockSpec((2, tm), lambda bi, i: (0, bi * nt + i)),
            pl.BlockSpec((N_CLASSES, LANES), lambda bi, i: (0, 0)),
        ],
        out_shape=[
            jax.ShapeDtypeStruct((t_all * pieces // 2, LANES), U32),
            jax.ShapeDtypeStruct((1, t_all), I32),
            jax.ShapeDtypeStruct((2, t_all), F32),
            jax.ShapeDtypeStruct((N_CLASSES, LANES), F32),
        ],
        compiler_params=_params("arbitrary", "arbitrary"),
        name="moe_pre",
    )(x, sh, sc, g_norm, router_w.T, router_bias.reshape(e, 1))


def _moe_rank_kernel(cls_ref, cnt_ref, dest_ref, meta_ref, pstart_sc, run_sc, *, sub, n_sub, nbp):
    nc = cnt_ref.shape[0]

    @pl.when(pl.program_id(0) == 0)
    def _():
        cnt = cnt_ref[...]
        padded = jnp.ceil(cnt / MOE_BLOCK) * MOE_BLOCK
        blk = lax.broadcasted_iota(I32, (1, nbp), 1).astype(F32) * MOE_BLOCK
        running = jnp.zeros((1, LANES), F32)
        block_c = jnp.zeros((1, nbp), F32)
        for c in range(nc):
            pstart_sc[c:c + 1, :] = running
            running = running + padded[c:c + 1, :]
            block_c = block_c + (running[:, 0:1] <= blk).astype(F32)
        run_sc[...] = jnp.zeros(run_sc.shape, F32)
        block_c = jnp.minimum(block_c, nc - 1.0)
        group = jnp.floor(block_c / PAIRS_PER_GROUP)
        pair = block_c - group * PAIRS_PER_GROUP
        lo = (pair >= 3.0).astype(F32) + (pair >= 5.0).astype(F32)
        hi = jnp.where(pair == 0.0, 1.0, jnp.where((pair == 1.0) | (pair == 3.0), 2.0, 3.0))
        n_used = jnp.broadcast_to(running[:, 0:1] / MOE_BLOCK, (1, nbp))
        meta_ref[...] = jnp.concatenate(
            [group * EXPERTS_PER_GROUP + lo, group * EXPERTS_PER_GROUP + hi, n_used], axis=0).astype(I32)

    upper = (lax.broadcasted_iota(I32, (sub, sub), 0) < lax.broadcasted_iota(I32, (sub, sub), 1)).astype(BF16)
    cid = lax.broadcasted_iota(I32, (nc, sub), 0)
    for j in range(n_sub):
        oh = (cid == cls_ref[:, j * sub:(j + 1) * sub]).astype(F32)
        rank = jnp.dot(oh.astype(BF16), upper, preferred_element_type=F32)
        base = pstart_sc[:, 0:1] + run_sc[:, 0:1]
        dest = jnp.sum(oh * (rank + base), axis=0, keepdims=True)
        dest_ref[:, j * sub:(j + 1) * sub] = dest.astype(I32)
        run_sc[...] += jnp.sum(oh, axis=1, keepdims=True)


def _moe_rank(cls, cnt, n_blocks):
    t_all = cls.shape[1]
    sub = 256
    tr = min(2048, t_all)
    nbp = -(-n_blocks // LANES) * LANES
    nc = cnt.shape[0]
    return pl.pallas_call(
        functools.partial(_moe_rank_kernel, sub=sub, n_sub=tr // sub, nbp=nbp),
        grid=(t_all // tr,),
        in_specs=[
            pl.BlockSpec((1, tr), lambda i: (0, i)),
            pl.BlockSpec((nc, LANES), lambda i: (0, 0)),
        ],
        out_specs=[
            pl.BlockSpec((1, tr), lambda i: (0, i)),
            pl.BlockSpec((3, nbp), lambda i: (0, 0)),
        ],
        out_shape=[
            jax.ShapeDtypeStruct((1, t_all), I32),
            jax.ShapeDtypeStruct((3, nbp), I32),
        ],
        scratch_shapes=[pltpu.VMEM((nc, LANES), F32), pltpu.VMEM((nc, LANES), F32)],
        compiler_params=_params("arbitrary"),
        name="moe_rank",
    )(cls, cnt)


def _tile_copy(src_ref, s, dst_ref, d, sem, rows):
    return pltpu.make_async_copy(src_ref.at[pl.ds(pl.multiple_of(s * rows, rows), rows)],
                                 dst_ref.at[pl.ds(pl.multiple_of(d * rows, rows), rows)], sem)


def _moe_dispatch_kernel(dest_hbm, h_ref, xs_in, xs_out, idx_smem, idx_sem, sem, *, tm, pieces):
    del xs_in
    i = pl.program_id(0)
    cp = pltpu.make_async_copy(dest_hbm.at[i], idx_smem, idx_sem)
    cp.start()
    cp.wait()

    def issue(t, carry):
        _tile_copy(h_ref, t, xs_out, idx_smem[0, t], sem, pieces).start()
        return carry

    lax.fori_loop(0, tm, issue, 0, unroll=8)

    def drain(t, carry):
        _tile_copy(h_ref, 0, xs_out, 0, sem, pieces).wait()
        return carry

    lax.fori_loop(0, tm, drain, 0, unroll=8)


def _moe_dispatch(dest_tiles, h, rows, tm, pieces):
    t_all = h.shape[0] // pieces
    zeros = jnp.zeros((rows * pieces, LANES), U32)
    return pl.pallas_call(
        functools.partial(_moe_dispatch_kernel, tm=tm, pieces=pieces),
        grid=(t_all // tm,),
        in_specs=[
            pl.BlockSpec(memory_space=pl.ANY),
            pl.BlockSpec((tm * pieces, LANES), lambda i: (i, 0)),
            pl.BlockSpec(memory_space=pl.ANY),
        ],
        out_specs=pl.BlockSpec(memory_space=pl.ANY),
        out_shape=jax.ShapeDtypeStruct((rows * pieces, LANES), U32),
        scratch_shapes=[pltpu.SMEM((1, tm), I32), pltpu.SemaphoreType.DMA(()), pltpu.SemaphoreType.DMA(())],
        input_output_aliases={2: 0},
        compiler_params=_params("arbitrary"),
        name="moe_dispatch",
    )(dest_tiles, h, zeros)


def _moe_expert_kernel(meta_ref, x_ref, wga_ref, wua_ref, wda_ref, wgb_ref, wub_ref, wdb_ref, y_ref, xb_sc,
                       *, pieces):
    i = pl.program_id(0)
    rows = xb_sc.shape[0]

    @pl.when(i < meta_ref[2, 0])
    def _():
        half = pieces // 2
        for s in range(half):
            first, second = _unpack_bf16_pair(x_ref[pl.ds(s, rows, stride=half), :])
            xb_sc[:, s * LANES:(s + 1) * LANES] = first.astype(BF16)
            xb_sc[:, (half + s) * LANES:(half + s + 1) * LANES] = second.astype(BF16)
        x = xb_sc[...]
        halves = []
        for wg_ref, wu_ref, wd_ref in ((wga_ref, wua_ref, wda_ref), (wgb_ref, wub_ref, wdb_ref)):
            gate = jnp.dot(x, wg_ref[0], preferred_element_type=F32)
            up = jnp.dot(x, wu_ref[0], preferred_element_type=F32)
            hidden = (_silu(gate) * up).astype(BF16)
            y = jnp.dot(hidden, wd_ref[0], preferred_element_type=F32)
            halves.append(y)
        word = _pack_bf16_pair(halves[0], halves[1])
        for s in range(pieces):
            y_ref[pl.ds(s, rows, stride=pieces), :] = word[:, s * LANES:(s + 1) * LANES]

    @pl.when(i >= meta_ref[2, 0])
    def _():
        y_ref[...] = jnp.zeros(y_ref.shape, U32)


def _moe_experts(meta, xs, w_gate, w_up, w_down, n_blocks, pieces):
    d, f = w_gate.shape[1], w_gate.shape[2]
    wspec = lambda k, a, c: pl.BlockSpec((1, a, c), lambda i, meta: (meta[k, i], 0, 0))
    return pl.pallas_call(
        functools.partial(_moe_expert_kernel, pieces=pieces),
        grid_spec=pltpu.PrefetchScalarGridSpec(
            num_scalar_prefetch=1,
            grid=(n_blocks,),
            in_specs=[pl.BlockSpec((MOE_BLOCK * pieces // 2, LANES), lambda i, meta: (i, 0)),
                      wspec(0, d, f), wspec(0, d, f), wspec(0, f, d),
                      wspec(1, d, f), wspec(1, d, f), wspec(1, f, d)],
            out_specs=pl.BlockSpec((MOE_BLOCK * pieces, LANES), lambda i, meta: (i, 0)),
            scratch_shapes=[pltpu.VMEM((MOE_BLOCK, d), BF16)],
        ),
        out_shape=jax.ShapeDtypeStruct((2 * xs.shape[0], LANES), U32),
        compiler_params=_params("arbitrary"),
        name="moe_experts",
    )(meta, xs, w_gate, w_up, w_down, w_gate, w_up, w_down)


def _combine_tile(dest_hbm, y_hbm, w_ref, x_ref, g_ref, o_ref, y_sc, idx_smem, idx_sem, sem,
                  *, tile, n_tiles, tm, pieces, n_parts=1):
    slot = tile % 2
    other = 1 - slot

    def idx_copy(tl, sl):
        return pltpu.make_async_copy(dest_hbm.at[tl], idx_smem.at[pl.ds(sl, 1)], idx_sem.at[sl])

    def gather(sl, lo, hi):
        def issue(t, carry):
            _tile_copy(y_hbm, idx_smem[sl, t], y_sc.at[sl], t, sem.at[sl], pieces).start()
            return carry

        lax.fori_loop(lo, hi, issue, 0, unroll=8)

    @pl.when(tile == 0)
    def _():
        idx_copy(0, 0).start()
        idx_copy(0, 0).wait()
        gather(0, 0, tm)
        if n_tiles > 1:
            idx_copy(1, 1).start()

    has_next = tile + 1 < n_tiles

    @pl.when(has_next)
    def _():
        idx_copy(tile + 1, other).wait()

    def drain(sl):
        def wait_one(t, carry):
            _tile_copy(y_hbm, 0, y_sc.at[sl], 0, sem.at[sl], pieces).wait()
            return carry

        lax.fori_loop(0, tm, wait_one, 0, unroll=8)

    def issue_part(j):
        if n_parts == 1:
            @pl.when(has_next)
            def _():
                gather(other, 0, tm)
        elif n_tiles > 1:
            for t in range(j * tm // n_parts, (j + 1) * tm // n_parts):
                _tile_copy(y_hbm, idx_smem[other, t], y_sc.at[other], t, sem.at[other], pieces).start()
        if j == n_parts - 1:
            @pl.when(tile + 2 < n_tiles)
            def _():
                idx_copy(tile + 2, slot).start()

            if n_parts > 1 and n_tiles > 1:
                @pl.when(tile + 1 == n_tiles)
                def _():
                    drain(other)

    if n_parts == 1:
        issue_part(0)
    drain(slot)
    w = w_ref[...]
    w_lo, w_hi = w[:, 0:1], w[:, 1:2]
    for s in range(pieces):
        cols = slice(s * LANES, (s + 1) * LANES)
        y_lo, y_hi = _unpack_bf16_pair(y_sc[slot, pl.ds(s, tm, stride=pieces), :])
        moe = w_lo * y_lo + w_hi * y_hi
        o_ref[0, :, cols] = x_ref[0, :, cols] + g_ref[0, :, cols] * moe
    return issue_part


def _moe_combine_kernel(dest_hbm, y_hbm, w_ref, x_ref, g_ref, o_ref, y_sc, idx_smem, idx_sem, sem,
                        *, tm, nt, n_tiles, pieces):
    _combine_tile(dest_hbm, y_hbm, w_ref, x_ref, g_ref, o_ref, y_sc, idx_smem, idx_sem, sem,
                  tile=pl.program_id(0) * nt + pl.program_id(1), n_tiles=n_tiles, tm=tm, pieces=pieces)


def _combine_scratch(tm, pieces):
    return [pltpu.VMEM((2, tm * pieces, LANES), U32), pltpu.SMEM((2, tm), I32),
            pltpu.SemaphoreType.DMA((2,)), pltpu.SemaphoreType.DMA((2,))]


def _moe_combine(dest_tiles, y, w_col, x, gate, tm, pieces):
    b, s, d = x.shape
    nt = s // tm
    return pl.pallas_call(
        functools.partial(_moe_combine_kernel, tm=tm, nt=nt, n_tiles=b * nt, pieces=pieces),
        grid=(b, nt),
        in_specs=[
            pl.BlockSpec(memory_space=pl.ANY),
            pl.BlockSpec(memory_space=pl.ANY),
            pl.BlockSpec((tm, 2), lambda bi, i: (bi * nt + i, 0)),
            pl.BlockSpec((1, tm, d), lambda bi, i: (bi, i, 0)),
            pl.BlockSpec((1, 1, d), lambda bi, i: (bi, 0, 0)),
        ],
        out_specs=pl.BlockSpec((1, tm, d), lambda bi, i: (bi, i, 0)),
        out_shape=jax.ShapeDtypeStruct((b, s, d), F32),
        scratch_shapes=_combine_scratch(tm, pieces),
        compiler_params=_params("arbitrary", "arbitrary"),
        name="moe_combine",
    )(dest_tiles, y, w_col, x, gate)


def _moe_layer(x, sh, sc, gate, g_norm, router_w, router_bias, w_gate, w_up, w_down):
    b, s, d = x.shape
    t_all = b * s
    pieces = d // LANES
    tm = min(512, s)
    n_blocks = -(-t_all // MOE_BLOCK) + N_CLASSES
    rows = n_blocks * MOE_BLOCK
    h, cls, w, cnt = _moe_pre(x, sh, sc, g_norm, router_w, router_bias)
    dest, meta = _moe_rank(cls, cnt, n_blocks)
    dest_tiles = dest.reshape(t_all // tm, 1, tm)
    xs = _moe_dispatch(dest_tiles, h, rows, tm, pieces // 2)
    y = _moe_experts(meta, xs, w_gate.astype(BF16), w_up.astype(BF16), w_down.astype(BF16), n_blocks, pieces)
    return dest_tiles, y, w.T, gate


def _final_norm_kernel(x_ref, g_ref, o_ref):
    x = x_ref[0]
    ms = jnp.mean(x * x, axis=-1, keepdims=True)
    o_ref[0] = x * lax.rsqrt(ms + NORM_EPS) * g_ref[...]


def _final_norm(x, g, tm=1024):
    b, s, d = x.shape
    tm = min(tm, s)
    return pl.pallas_call(
        _final_norm_kernel,
        grid=(b, s // tm),
        in_specs=[
            pl.BlockSpec((1, tm, d), lambda bi, i: (bi, i, 0)),
            pl.BlockSpec((1, d), lambda bi, i: (0, 0)),
        ],
        out_specs=pl.BlockSpec((1, tm, d), lambda bi, i: (bi, i, 0)),
        out_shape=jax.ShapeDtypeStruct((b, s, d), F32),
        compiler_params=_params("parallel", "arbitrary"),
        name="final_norm",
    )(x, g.reshape(1, d))


def kernel(x, c, ada_w, ada_b, norm1_g, norm2_g, router_w, router_bias, moe_w_gate, moe_w_up, moe_w_down, dil_w_in, dil_w_out, diff_w_in, diff_lam_q1, diff_lam_k1, diff_lam_q2, diff_lam_k2, diff_head_norm_g, diff_w_out, ssm_w_in, ssm_conv_w, ssm_conv_b, ssm_dt_bias, ssm_A_log, ssm_D, ssm_norm_g, ssm_w_out, final_norm_g):
    b, s, d = x.shape
    depth = ada_w.shape[0]
    mod = _ada_mod(c, ada_w, ada_b).reshape(depth, b, 6, 1, d)
    tables = _rope_tables(s)
    pending = None
    for i in range(depth):
        sh1, sc1, g1, sh2, sc2, g2 = (mod[i, :, j] for j in range(6))
        n1 = norm1_g[i].reshape(1, d)
        kind, j = i % 3, i // 3
        if kind == 0:
            x = _dilated_layer(x, pending, sh1, sc1, g1, n1, dil_w_in[j], dil_w_out[j], tables)
        elif kind == 1:
            x = _diff_layer(x, pending, sh1, sc1, g1, n1, diff_w_in[j], diff_lam_q1[j], diff_lam_k1[j],
                            diff_lam_q2[j], diff_lam_k2[j], diff_head_norm_g[j], diff_w_out[j], tables, i)
        else:
            x = _ssd_layer(x, pending, sh1, sc1, g1, n1, ssm_w_in[j], ssm_conv_w[j], ssm_conv_b[j],
                           ssm_dt_bias[j], ssm_A_log[j], ssm_D[j], ssm_norm_g[j], ssm_w_out[j])
        pending = _moe_layer(x, sh2, sc2, g2, norm2_g[i].reshape(1, d), router_w, router_bias,
                             moe_w_gate[i], moe_w_up[i], moe_w_down[i])
    dest_tiles, y, w_col, gate = pending
    x = _moe_combine(dest_tiles, y, w_col, x, gate, dest_tiles.shape[-1], d // LANES)
    return _final_norm(x, final_norm_g)
```

```python
import functools
import math

import jax
import jax.numpy as jnp
from jax import lax
from jax.experimental import pallas as pl
from jax.experimental.pallas import tpu as pltpu

F32 = jnp.float32
BF16 = jnp.bfloat16
I32 = jnp.int32
U32 = jnp.uint32
HIGHEST = lax.Precision.HIGHEST

LANES = 128
SUBLANES = 8
VMEM_LIMIT_BYTES = 56 * 1024 * 1024

NORM_EPS = 1e-6
ROPE_THETA = 500000.0
ROPE_FRACTION = 4
HEAD_DIM = 64
ATTN_BLOCK = 128
DIL_CONFIGS = ((128, 1), (512, 4), (2048, 16))
DIFF_NORM_EPS = 1e-5
SSM_HEAD_DIM = 64
SSM_GROUPS = 4
SSM_STATE = 128
SSM_CONV = 4
SSM_CHUNK = 256
N_EXPERTS = 16
N_EXPERT_GROUPS = 4
EXPERTS_PER_GROUP = 4
PAIRS_PER_GROUP = 6
N_CLASSES = N_EXPERT_GROUPS * PAIRS_PER_GROUP
MOE_BLOCK = 256
COL_CHUNK = 512
NEG = -1e30

_NT = (((1,), (1,)), ((), ()))
_TN = (((0,), (0,)), ((), ()))


def _params(*sem):
    return pltpu.CompilerParams(dimension_semantics=sem, vmem_limit_bytes=VMEM_LIMIT_BYTES)


def _silu(v):
    return v * (0.5 * jnp.tanh(0.5 * v) + 0.5)


def _pack_bf16_pair(a, b):
    hi = lax.bitcast_convert_type(a.astype(BF16).astype(F32), U32)
    lo = lax.bitcast_convert_type(b.astype(BF16).astype(F32), U32)
    return hi | (lo >> 16)


def _unpack_bf16_pair(word):
    return (lax.bitcast_convert_type(word & jnp.uint32(0xFFFF0000), F32),
            lax.bitcast_convert_type(word << 16, F32))


def _ada_kernel(c_ref, w_ref, b_ref, o_ref):
    cond = _silu(c_ref[...])
    o_ref[0] = jnp.dot(cond, w_ref[0], preferred_element_type=F32, precision=HIGHEST) + b_ref[0]


def _ada_mod(c, ada_w, ada_b):
    depth, d, n = ada_w.shape
    b = c.shape[0]
    tn = 1536
    return pl.pallas_call(
        _ada_kernel,
        grid=(depth, n // tn),
        in_specs=[
            pl.BlockSpec((b, d), lambda i, j: (0, 0)),
            pl.BlockSpec((1, d, tn), lambda i, j: (i, 0, j)),
            pl.BlockSpec((1, 1, tn), lambda i, j: (i, 0, j)),
        ],
        out_specs=pl.BlockSpec((1, b, tn), lambda i, j: (i, 0, j)),
        out_shape=jax.ShapeDtypeStruct((depth, b, n), F32),
        compiler_params=_params("arbitrary", "arbitrary"),
        name="ada_mod",
    )(c, ada_w, ada_b.reshape(depth, 1, n))


def _rope_tables(seq):
    r = HEAD_DIM // ROPE_FRACTION
    half = r // 2
    inv = jnp.power(ROPE_THETA, -jnp.arange(half, dtype=F32) * 2.0 / r)
    ang = jnp.arange(seq, dtype=F32)[:, None] * inv[None, :]
    cos, sin = jnp.cos(ang), jnp.sin(ang)
    ones = jnp.ones((seq, HEAD_DIM - r), F32)
    zeros = jnp.zeros((seq, HEAD_DIM - r), F32)
    zh = jnp.zeros((seq, half), F32)
    cos_t = jnp.concatenate([cos, cos, ones], axis=1)
    sin_a = jnp.concatenate([zh, sin, zeros], axis=1)
    sin_b = jnp.concatenate([-sin, zh, zeros], axis=1)
    rep = LANES // HEAD_DIM
    lane_form = tuple(jnp.tile(t, (1, rep)) for t in (cos_t, sin_a, sin_b))
    return lane_form, tuple(t.T for t in lane_form)


def _modulated_norm(x, g, sc, sh):
    ms = jnp.mean(x * x, axis=-1, keepdims=True)
    return x * lax.rsqrt(ms + NORM_EPS) * g * (1.0 + sc) + sh


def _rope(a, cos, sa, sb, scale, axis):
    half = HEAD_DIM // ROPE_FRACTION // 2
    r = a * cos + pltpu.roll(a, half, axis) * sa + pltpu.roll(a, LANES - half, axis) * sb
    return r if scale == 1.0 else r * scale


def _proj_kernel(*refs, chunks, tplan, n_out, use_rope, has_perm, pending, n_tiles):
    x_ref, sh_ref, sc_ref, g_ref, w_ref = refs[:5]
    pos = 5
    if use_rope:
        cos_ref, sa_ref, sb_ref = refs[pos:pos + 3]
        pos += 3
    if tplan is not None:
        wt_ref, cost_ref, sat_ref, sbt_ref = refs[pos:pos + 4]
        pos += 4
    if pending:
        dest_hbm, y_hbm, wcol_ref, g2_ref = refs[pos:pos + 4]
        pos += 4
    out_refs = refs[pos:pos + n_out]
    pos += n_out
    if pending:
        xnew_ref = refs[pos]
        pos += 1
    perm_sc = None
    if has_perm:
        perm_sc = refs[pos]
        pos += 1
    tm = x_ref.shape[1]
    issue_part = None
    if pending:
        issue_part = _combine_tile(dest_hbm, y_hbm, wcol_ref, x_ref, g2_ref, xnew_ref, *refs[pos:pos + 4],
                                   tile=pl.program_id(0) * pl.num_programs(1) + pl.program_id(1),
                                   n_tiles=n_tiles, tm=tm,
                                   pieces=x_ref.shape[2] // LANES, n_parts=len(chunks))
        x_ref = xnew_ref
    h = _modulated_norm(x_ref[0], g_ref[...], sc_ref[0], sh_ref[0]).astype(BF16)
    tiles_per_chunk = COL_CHUNK // LANES
    for c, (oi, off, scale, dil) in enumerate(chunks):
        if issue_part is not None:
            issue_part(c)
        o_ref = out_refs[oi]
        acc = jnp.dot(h, w_ref[:, c * COL_CHUNK:(c + 1) * COL_CHUNK], preferred_element_type=F32)
        tiles = None
        if scale is not None:
            cos, sa, sb = cos_ref[...], sa_ref[...], sb_ref[...]
            tiles = [_rope(acc[:, s * LANES:(s + 1) * LANES], cos, sa, sb, scale, 1)
                     for s in range(tiles_per_chunk)]
        if dil == 1:
            if tiles is None:
                o_ref[0, :, off:off + COL_CHUNK] = acc.astype(o_ref.dtype)
            else:
                for s, tl in enumerate(tiles):
                    o_ref[0, :, off + s * LANES:off + (s + 1) * LANES] = tl.astype(o_ref.dtype)
            continue
        if tiles is None:
            tiles = [acc[:, s * LANES:(s + 1) * LANES] for s in range(tiles_per_chunk)]
        for s, tl in enumerate(tiles):
            perm_sc[s] = tl
        for rho in range(dil):
            for s in range(tiles_per_chunk):
                o_ref[0, rho, :, off + s * LANES:off + (s + 1) * LANES] = (
                    perm_sc[s, pl.ds(rho, tm // dil, stride=dil), :].astype(o_ref.dtype))
    if tplan is not None:
        oi, scale = tplan
        o_ref = out_refs[oi]
        acct = lax.dot_general(wt_ref[...], h, _NT, preferred_element_type=F32)
        cost, sat, sbt = cost_ref[...], sat_ref[...], sbt_ref[...]
        for s in range(acct.shape[0] // LANES):
            rows = slice(s * LANES, (s + 1) * LANES)
            o_ref[0, rows, :] = _rope(acct[rows, :], cost, sat, sbt, scale, 0).astype(o_ref.dtype)


def _proj(x, sh, sc, g, w, chunks, outs, tables, wt=None, tplan=None, pending=None, tm=512):
    b, s, d = x.shape
    n = w.shape[1]
    tm = min(tm, s)
    use_rope = any(c[2] is not None for c in chunks)
    has_perm = any(c[3] > 1 for c in chunks)
    in_specs = [
        pl.BlockSpec((1, tm, d), lambda bi, i: (bi, i, 0)),
        pl.BlockSpec((1, 1, d), lambda bi, i: (bi, 0, 0)),
        pl.BlockSpec((1, 1, d), lambda bi, i: (bi, 0, 0)),
        pl.BlockSpec((1, d), lambda bi, i: (0, 0)),
        pl.BlockSpec((d, n), lambda bi, i: (0, 0), pipeline_mode=pl.Buffered(1)),
    ]
    args = [x, sh, sc, g, w]
    if use_rope:
        in_specs += [pl.BlockSpec((tm, LANES), lambda bi, i: (i, 0))] * 3
        args += list(tables[0])
    if tplan is not None:
        in_specs.append(pl.BlockSpec(wt.shape, lambda bi, i: (0, 0), pipeline_mode=pl.Buffered(1)))
        in_specs += [pl.BlockSpec((LANES, tm), lambda bi, i: (0, i))] * 3
        args += [wt] + list(tables[1])
    if pending is not None:
        dest_tiles, y, w_col, gate2 = pending
        assert dest_tiles.shape[-1] == tm
        nt = s // tm
        in_specs += [pl.BlockSpec(memory_space=pl.ANY), pl.BlockSpec(memory_space=pl.ANY),
                     pl.BlockSpec((tm, 2), lambda bi, i: (bi * nt + i, 0)),
                     pl.BlockSpec((1, 1, d), lambda bi, i: (bi, 0, 0))]
        args += [dest_tiles, y, w_col, gate2]
    out_specs, out_shape = [], []
    for layout, wd, dt, dil in outs:
        if layout == "tok":
            out_specs.append(pl.BlockSpec((1, tm, wd), lambda bi, i: (bi, i, 0)))
            out_shape.append(jax.ShapeDtypeStruct((b, s, wd), dt))
        elif layout == "res":
            out_specs.append(pl.BlockSpec((1, dil, tm // dil, wd), lambda bi, i: (bi, 0, i, 0)))
            out_shape.append(jax.ShapeDtypeStruct((b, dil, s // dil, wd), dt))
        else:
            out_specs.append(pl.BlockSpec((1, wd, tm), lambda bi, i: (bi, 0, i)))
            out_shape.append(jax.ShapeDtypeStruct((b, wd, s), dt))
    scratch = [pltpu.VMEM((COL_CHUNK // LANES, tm, LANES), F32)] if has_perm else []
    if pending is not None:
        out_specs.append(pl.BlockSpec((1, tm, d), lambda bi, i: (bi, i, 0)))
        out_shape.append(jax.ShapeDtypeStruct((b, s, d), F32))
        scratch += _combine_scratch(tm, d // LANES)
    return pl.pallas_call(
        functools.partial(_proj_kernel, chunks=tuple(chunks), tplan=tplan, n_out=len(outs),
                          use_rope=use_rope, has_perm=has_perm, pending=pending is not None,
                          n_tiles=b * (s // tm)),
        grid=(b, s // tm),
        in_specs=in_specs,
        out_specs=out_specs,
        out_shape=out_shape,
        scratch_shapes=scratch,
        compiler_params=_params(*(("arbitrary", "arbitrary") if pending is not None else ("parallel", "arbitrary"))),
        name="norm_proj",
    )(*args)


def _outproj_kernel(y_ref, w_ref, x_ref, g_ref, o_ref):
    y = jnp.dot(y_ref[0], w_ref[...], preferred_element_type=F32)
    o_ref[0] = x_ref[0] + g_ref[0] * y


def _outproj(y, w, x, gate, tm=512):
    b, s, d = x.shape
    k = y.shape[-1]
    tm = min(tm, s)
    return pl.pallas_call(
        _outproj_kernel,
        grid=(b, s // tm),
        in_specs=[
            pl.BlockSpec((1, tm, k), lambda bi, i: (bi, i, 0)),
            pl.BlockSpec((k, d), lambda bi, i: (0, 0), pipeline_mode=pl.Buffered(1)),
            pl.BlockSpec((1, tm, d), lambda bi, i: (bi, i, 0)),
            pl.BlockSpec((1, 1, d), lambda bi, i: (bi, 0, 0)),
        ],
        out_specs=pl.BlockSpec((1, tm, d), lambda bi, i: (bi, i, 0)),
        out_shape=jax.ShapeDtypeStruct((b, s, d), F32),
        compiler_params=_params("parallel", "arbitrary"),
        name="out_proj",
    )(y, w, x, gate)


def _dil_kernel(q_ref, kc_ref, kp_ref, vc_ref, vp_ref, o_ref, lse_ref, kbuf, vbuf, *, tq, back, heads):
    n = pl.program_id(2)
    qb = ATTN_BLOCK
    kbuf[0:qb] = kp_ref[0, 0]
    kbuf[qb:] = kc_ref[0, 0]
    vbuf[0:qb] = vp_ref[0, 0]
    vbuf[qb:] = vc_ref[0, 0]
    rows = heads * qb
    qi = lax.broadcasted_iota(I32, (rows, 2 * qb), 0) & (qb - 1)
    kj = lax.broadcasted_iota(I32, (rows, 2 * qb), 1)
    rel = kj - qi
    band = (rel >= qb - back) & (rel <= qb)
    lane = lax.broadcasted_iota(I32, (qb, LANES), 1)
    low_half = lane < HEAD_DIM
    zero_q = jnp.zeros((qb, LANES), BF16)
    ones_k = jnp.ones((2 * qb, LANES), BF16)
    for j in range(tq // qb):
        first_key = n * tq + (j - 1) * qb
        valid = band & (kj + first_key >= 0)
        parts = []
        for hp in range(heads // 2):
            cols = slice(hp * LANES, (hp + 1) * LANES)
            q2 = q_ref[0, 0, j * qb:(j + 1) * qb, cols]
            q_stack = jnp.concatenate([jnp.where(low_half, q2, zero_q), jnp.where(low_half, zero_q, q2)], axis=0)
            parts.append(lax.dot_general(q_stack, kbuf[j * qb:(j + 2) * qb, cols], _NT,
                                         preferred_element_type=F32))
        s = jnp.where(valid, jnp.concatenate(parts, axis=0), NEG)
        m = jnp.max(s, axis=-1, keepdims=True)
        pb = jnp.exp2(s - m).astype(BF16)
        l = jnp.dot(pb, ones_k, preferred_element_type=F32)
        inv = 1.0 / l
        lse = (jnp.broadcast_to(m, l.shape) + jnp.log2(l)) * math.log(2.0)
        lse_tile = jnp.zeros((qb, LANES), F32)
        for hp in range(heads // 2):
            cols = slice(hp * LANES, (hp + 1) * LANES)
            ra = slice(2 * hp * qb, (2 * hp + 1) * qb)
            rb = slice((2 * hp + 1) * qb, (2 * hp + 2) * qb)
            pv = jnp.dot(pb[2 * hp * qb:(2 * hp + 2) * qb], vbuf[j * qb:(j + 2) * qb, cols],
                         preferred_element_type=F32)
            o = jnp.where(low_half, pv[:qb] * inv[ra], pv[qb:] * inv[rb])
            o_ref[0, 0, j * qb:(j + 1) * qb, cols] = o.astype(o_ref.dtype)
            lse_tile = jnp.where(lane == 2 * hp, lse[ra], lse_tile)
            lse_tile = jnp.where(lane == 2 * hp + 1, lse[rb], lse_tile)
        lse_ref[0, 0, j * qb:(j + 1) * qb, :] = lse_tile


def _dil_group(proj, window, dilation):
    b, dil, ln, c = proj.shape
    back = window // dilation
    assert dil == dilation and back <= ATTN_BLOCK and ln % ATTN_BLOCK == 0
    width = c // 3
    heads = width // HEAD_DIM
    tq = min(512, ln)
    sub = tq // ATTN_BLOCK

    def cur(col):
        return pl.BlockSpec((1, 1, tq, width), lambda bi, r, n: (bi, r, n, col))

    def prev(col):
        return pl.BlockSpec((1, 1, ATTN_BLOCK, width),
                            lambda bi, r, n: (bi, r, jnp.maximum(n * sub - 1, 0), col))

    return pl.pallas_call(
        functools.partial(_dil_kernel, tq=tq, back=back, heads=heads),
        grid=(b, dil, ln // tq),
        in_specs=[cur(0), cur(1), prev(1), cur(2), prev(2)],
        out_specs=[
            pl.BlockSpec((1, 1, tq, width), lambda bi, r, n: (bi, r, n, 0)),
            pl.BlockSpec((1, 1, tq, LANES), lambda bi, r, n: (bi, r, n, 0)),
        ],
        out_shape=[
            jax.ShapeDtypeStruct((b, dil, ln, width), BF16),
            jax.ShapeDtypeStruct((b, dil, ln, LANES), F32),
        ],
        scratch_shapes=[pltpu.VMEM((tq + ATTN_BLOCK, width), BF16)] * 2,
        compiler_params=_params("parallel", "parallel", "arbitrary"),
        name=f"dil_attn_d{dilation}",
    )(proj, proj, proj, proj, proj)


def _dil_out_kernel(o0_ref, o1_ref, o2_ref, l0_ref, l1_ref, l2_ref, w_ref, x_ref, g_ref, out_ref,
                    o_sc, l1_sc, l2_sc):
    tm = x_ref.shape[1]

    def to_token_order(src_ref, dst_sc):
        dil = src_ref.shape[1]
        for rho in range(dil):
            src = src_ref[0, rho].astype(F32)
            for s in range(dst_sc.shape[0]):
                dst_sc[s, pl.ds(rho, tm // dil, stride=dil), :] = src[:, s * LANES:(s + 1) * LANES]
        return jnp.concatenate([dst_sc[s] for s in range(dst_sc.shape[0])], axis=1)

    ls = [l0_ref[0, 0], to_token_order(l1_ref, l1_sc), to_token_order(l2_ref, l2_sc)]
    mx =jnp.maximum(jnp.maximum(ls[0], ls[1]), ls[2])
    es = [jnp.exp(v - mx) for v in ls]
    inv = 1.0 / (es[0] + es[1] + es[2])
    width = o0_ref.shape[-1]
    expand = (lax.broadcasted_iota(I32, (LANES, width), 0)
              == lax.broadcasted_iota(I32, (LANES, width), 1) // HEAD_DIM).astype(BF16)
    o = jnp.zeros((tm, width), F32)
    for gi, (e, o_ref) in enumerate(zip(es, (o0_ref, o1_ref, o2_ref))):
        alpha = e * inv
        hi = alpha.astype(BF16)
        lo = (alpha - hi.astype(F32)).astype(BF16)
        a_full = (jnp.dot(hi, expand, preferred_element_type=F32)
                  + jnp.dot(lo, expand, preferred_element_type=F32))
        og = o_ref[0, 0].astype(F32) if gi == 0 else to_token_order(o_ref, o_sc)
        o = o + a_full * og
    y = jnp.dot(o.astype(BF16), w_ref[...], preferred_element_type=F32)
    out_ref[0] = x_ref[0] + g_ref[0] * y


def _dil_out(os_, lses, w, x, gate, tm=512):
    b, s, d = x.shape
    width = os_[0].shape[-1]
    tm = min(tm, s)
    tok = lambda wd: pl.BlockSpec((1, tm, wd), lambda bi, i: (bi, i, 0))
    res = lambda a: pl.BlockSpec((1, a.shape[1], tm // a.shape[1], a.shape[3]), lambda bi, i: (bi, 0, i, 0))
    return pl.pallas_call(
        _dil_out_kernel,
        grid=(b, s // tm),
        in_specs=[res(a) for a in os_] + [res(a) for a in lses] + [
            pl.BlockSpec((width, d), lambda bi, i: (0, 0)),
            tok(d),
            pl.BlockSpec((1, 1, d), lambda bi, i: (bi, 0, 0)),
        ],
        out_specs=tok(d),
        out_shape=jax.ShapeDtypeStruct((b, s, d), F32),
        scratch_shapes=[pltpu.VMEM((width // LANES, tm, LANES), F32), pltpu.VMEM((1, tm, LANES), F32),
                        pltpu.VMEM((1, tm, LANES), F32)],
        compiler_params=_params("parallel", "arbitrary"),
        name="dil_out",
    )(*os_, *lses, w, x, gate)


def _dilated_layer(x, pending, sh, sc, gate, g_norm, w_in, w_out, tables):
    b, s, _ = x.shape
    n = w_in.shape[1]
    gw = n // len(DIL_CONFIGS)
    chunks, outs = [], []
    for g, (_, dilation) in enumerate(DIL_CONFIGS):
        for kind in range(3):
            scale = (HEAD_DIM ** -0.5 * math.log2(math.e), 1.0, None)[kind]
            chunks.append((g, kind * COL_CHUNK, scale, dilation))
        outs.append(("tok" if dilation == 1 else "res", gw, BF16, dilation))
    projs = list(_proj(x, sh, sc, g_norm, w_in.astype(BF16), chunks, outs, tables, pending=pending))
    if pending is not None:
        x = projs.pop()
    os_, lses = [], []
    for proj, (window, dilation) in zip(projs, DIL_CONFIGS):
        o, lse = _dil_group(proj.reshape(b, dilation, s // dilation, gw), window, dilation)
        os_.append(o)
        lses.append(lse)
    return _dil_out(os_, lses, w_out.astype(BF16), x, gate)


def _diff_kernel(qi_tab, ki_tab, q_ref, kt_ref, v_ref, lq1_ref, lk1_ref, lq2_ref, lk2_ref, hg_ref, o_ref,
                 m_sc, acc_sc, *, t, r, lam_init):
    pair = pl.program_id(2)
    qi = qi_tab[pair]
    ki = ki_tab[pair]
    vw = v_ref.shape[-1]

    @pl.when(ki == 0)
    def _():
        m_sc[...] = jnp.full(m_sc.shape, NEG, F32)
        acc_sc[...] = jnp.zeros(acc_sc.shape, F32)

    def step(diagonal, r):
        kt = kt_ref[0]
        v_aug = jnp.concatenate([v_ref[0], jnp.ones((t, LANES), BF16)], axis=1)
        lane = lax.broadcasted_iota(I32, (r, vw), 1)
        work = [(mi, c) for mi in range(2) for c in range(t // r)]

        def scores(mi, c):
            in_half = (lane >= mi * HEAD_DIM) & (lane < (mi + 1) * HEAD_DIM)
            ncols = (c + 1) * r if diagonal else t
            q_c = jnp.where(in_half, q_ref[0, c * r:(c + 1) * r, :], jnp.zeros((r, vw), BF16))
            return jnp.dot(q_c, kt[:, :ncols], preferred_element_type=F32)

        ahead = 3
        pending = [scores(*wk) for wk in work[:ahead]]
        for idx, (mi, c) in enumerate(work):
            s = pending.pop(0)
            if idx + ahead < len(work):
                pending.append(scores(*work[idx + ahead]))
            ncols = s.shape[1]
            if diagonal:
                col = lax.broadcasted_iota(I32, (r, ncols), 1)
                row = lax.broadcasted_iota(I32, (r, ncols), 0) + c * r
                s = jnp.where(col <= row, s, NEG)
            tiles = [s[:, j * LANES:(j + 1) * LANES] for j in range(ncols // LANES)]
            tmax = functools.reduce(jnp.maximum, tiles)
            srows = slice(mi * t + c * r, mi * t + (c + 1) * r)
            m_prev = m_sc[srows, :]
            m_new = jnp.maximum(m_prev, jnp.max(tmax, axis=-1, keepdims=True))
            alpha = jnp.exp2(m_prev - m_new)
            p = jnp.concatenate([jnp.exp2(tl - m_new) for tl in tiles], axis=1).astype(BF16)
            pv = jnp.dot(p, v_aug[:ncols], preferred_element_type=F32)
            acc_sc[srows, :] = jnp.concatenate([alpha, alpha], axis=1) * acc_sc[srows, :] + pv
            m_sc[srows, :] = m_new

    @pl.when(ki < qi)
    def _():
        step(False, 2 * r)

    @pl.when(ki == qi)
    def _():
        step(True, 2 * r)
        lam = (jnp.exp(jnp.sum(lq1_ref[...] * lk1_ref[...], axis=-1, keepdims=True))
               - jnp.exp(jnp.sum(lq2_ref[...] * lk2_ref[...], axis=-1, keepdims=True)) + lam_init)
        o = acc_sc[:t, :vw] / acc_sc[:t, vw:] - lam * (acc_sc[t:, :vw] / acc_sc[t:, vw:])
        ms = jnp.mean(o * o, axis=-1, keepdims=True)
        o = o * lax.rsqrt(ms + DIFF_NORM_EPS) * hg_ref[...] * (1.0 - lam_init)
        o_ref[0] = o.astype(o_ref.dtype)


def _diff_attention(q, kt, v, lam_q1, lam_k1, lam_q2, lam_k2, head_g, lam_init, t=2048, r=128):
    b, s, d = q.shape
    vw = 2 * HEAD_DIM
    assert vw == LANES
    heads = d // vw
    t = min(t, s)
    r = min(r, t // 2)
    nq = s // t
    pairs = [(qi, ki) for qi in range(nq) for ki in range(qi + 1)]
    qi_tab = jnp.asarray([p[0] for p in pairs], I32)
    ki_tab = jnp.asarray([p[1] for p in pairs], I32)
    vec = lambda n: pl.BlockSpec((1, n), lambda bi, h, p, qt, kt_: (0, 0))
    return pl.pallas_call(
        functools.partial(_diff_kernel, t=t, r=r, lam_init=lam_init),
        grid_spec=pltpu.PrefetchScalarGridSpec(
            num_scalar_prefetch=2,
            grid=(b, heads, len(pairs)),
            in_specs=[
                pl.BlockSpec((1, t, vw), lambda bi, h, p, qt, kt_: (bi, qt[p], h)),
                pl.BlockSpec((1, vw, t), lambda bi, h, p, qt, kt_: (bi, h, kt_[p])),
                pl.BlockSpec((1, t, vw), lambda bi, h, p, qt, kt_: (bi, kt_[p], h)),
                vec(HEAD_DIM), vec(HEAD_DIM), vec(HEAD_DIM), vec(HEAD_DIM), vec(vw),
            ],
            out_specs=pl.BlockSpec((1, t, vw), lambda bi, h, p, qt, kt_: (bi, qt[p], h)),
            scratch_shapes=[
                pltpu.VMEM((2 * t, LANES), F32),
                pltpu.VMEM((2 * t, vw + LANES), F32),
            ],
        ),
        out_shape=jax.ShapeDtypeStruct((b, s, d), BF16),
        compiler_params=_params("parallel", "parallel", "arbitrary"),
        name="diff_attn",
    )(qi_tab, ki_tab, q, kt, v, lam_q1.reshape(1, -1), lam_k1.reshape(1, -1), lam_q2.reshape(1, -1),
      lam_k2.reshape(1, -1), head_g.reshape(1, -1))


def _diff_layer(x, pending, sh, sc, gate, g_norm, w_in, lam_q1, lam_k1, lam_q2, lam_k2, head_g, w_out,
                tables, layer_idx):
    d = w_in.shape[1] // 3
    nch = d // COL_CHUNK
    q_scale = HEAD_DIM ** -0.5 * math.log2(math.e)
    chunks = ([(0, c * COL_CHUNK, q_scale, 1) for c in range(nch)]
              + [(1, c * COL_CHUNK, None, 1) for c in range(nch)])
    w_qv = jnp.concatenate([w_in[:, :d], w_in[:, 2 * d:]], axis=1).astype(BF16)
    w_kt = w_in[:, d:2 * d].T.astype(BF16)
    q, v, kt, *rest = _proj(x, sh, sc, g_norm, w_qv, chunks,
                            [("tok", d, BF16, 1), ("tok", d, BF16, 1), ("T", d, BF16, 1)], tables,
                            wt=w_kt, tplan=(2, 1.0), pending=pending)
    if pending is not None:
        x = rest[0]
    lam_init = 0.8 - 0.6 * math.exp(-0.3 * layer_idx)
    o = _diff_attention(q, kt, v, lam_q1, lam_k1, lam_q2, lam_k2, head_g, lam_init)
    return _outproj(o, w_out.astype(BF16), x, gate)


def _causal_conv_silu(cur_ref, ext_sc, w_ref, b_ref):
    cur = cur_ref[0].astype(F32)
    rows = cur.shape[0]
    ext_sc[SUBLANES:, :] = cur
    w = w_ref[...]
    acc = cur * w[SSM_CONV - 1:SSM_CONV] + b_ref[...]
    for k in range(1, SSM_CONV):
        acc = acc + ext_sc[SUBLANES - k:SUBLANES - k + rows, :] * w[SSM_CONV - 1 - k:SSM_CONV - k]
    ext_sc[0:SUBLANES, :] = cur[rows - SUBLANES:rows]
    return _silu(acc)


def _ssd_kernel(z_ref, x_ref, b_ref, c_ref, dt_ref, wx_ref, wb_ref, wc_ref, bx_ref, bb_ref, bc_ref,
                dtb_ref, alog_ref, dskip_ref, ng_ref, y_ref,
                state_sc, tx_sc, tb_sc, tc_sc, *, hpg):
    ci = pl.program_id(2)
    L = x_ref.shape[1]
    P = SSM_HEAD_DIM
    gw = hpg * P

    @pl.when(ci == 0)
    def _():
        state_sc[...] = jnp.zeros(state_sc.shape, F32)
        tx_sc[0:SUBLANES, :] = jnp.zeros((SUBLANES, tx_sc.shape[1]), F32)
        tb_sc[0:SUBLANES, :] = jnp.zeros((SUBLANES, tb_sc.shape[1]), F32)
        tc_sc[0:SUBLANES, :] = jnp.zeros((SUBLANES, tc_sc.shape[1]), F32)

    xs = _causal_conv_silu(x_ref, tx_sc, wx_ref, bx_ref)
    bm = _causal_conv_silu(b_ref, tb_sc, wb_ref, bb_ref).astype(BF16)
    cm = _causal_conv_silu(c_ref, tc_sc, wc_ref, bc_ref).astype(BF16)

    raw = dt_ref[0] + dtb_ref[...]
    dt = jnp.maximum(raw, 0.0) + jnp.log(1.0 + jnp.exp(-jnp.abs(raw)))
    a = -jnp.exp(alog_ref[...])
    da = dt * a
    ti = lax.broadcasted_iota(I32, (L, L), 0)
    si = lax.broadcasted_iota(I32, (L, L), 1)
    causal = ti >= si
    acum = jnp.dot(causal.astype(F32), da, preferred_element_type=F32, precision=HIGHEST)
    acum2 = acum * math.log2(math.e)
    acum2_t = acum2.T
    last = acum[L - 1:L, :]
    H = L // 2
    tri = causal[:H, :H]
    expand = (lax.broadcasted_iota(I32, (LANES, gw), 0)
              == lax.broadcasted_iota(I32, (LANES, gw), 1) // P).astype(BF16)

    def per_head_lanes(a):
        hi = a.astype(BF16)
        lo = (a - hi.astype(F32)).astype(BF16)
        return (jnp.dot(hi, expand, preferred_element_type=F32)
                + jnp.dot(lo, expand, preferred_element_type=F32))

    dx = per_head_lanes(dt) * xs
    dxb = dx.astype(BF16)
    eacum_e = per_head_lanes(jnp.exp(acum))
    wdx = (per_head_lanes(jnp.exp(last - acum)) * dx).astype(BF16)

    cb = lax.dot_general(cm, bm, _NT, preferred_element_type=F32)
    y_inter = jnp.dot(cm, state_sc[...].astype(BF16), preferred_element_type=F32)

    low_half = lax.broadcasted_iota(I32, (L, LANES), 1) < P
    y_pairs = []
    for hp in range(hpg // 2):
        cols = slice(hp * LANES, (hp + 1) * LANES)
        ys = []
        for k in (2 * hp, 2 * hp + 1):
            col = acum2[:, k:k + 1]
            row = acum2_t[k:k + 1, :]
            m00 = cb[:H, :H] * jnp.exp2(jnp.where(tri, col[:H] - row[:, :H], NEG))
            m10 = cb[H:, :H] * jnp.exp2(col[H:] - row[:, :H])
            m11 = cb[H:, H:] * jnp.exp2(jnp.where(tri, col[H:] - row[:, H:], NEG))
            top = jnp.dot(m00.astype(BF16), dxb[:H, cols], preferred_element_type=F32)
            bot = (jnp.dot(m10.astype(BF16), dxb[:H, cols], preferred_element_type=F32)
                   + jnp.dot(m11.astype(BF16), dxb[H:, cols], preferred_element_type=F32))
            ys.append(jnp.concatenate([top, bot], axis=0))
        y_pairs.append(jnp.where(low_half, ys[0], ys[1]))

    state_sc[...] = (state_sc[...] * eacum_e[L - 1:L, :]
                     + lax.dot_general(bm, wdx, _TN, preferred_element_type=F32))

    y = jnp.concatenate(y_pairs, axis=1) + y_inter * eacum_e + dskip_ref[...] * xs
    y = y * _silu(z_ref[0].astype(F32))
    ms = jnp.mean(y * y, axis=-1, keepdims=True)
    y_ref[0] = (y * lax.rsqrt(ms + NORM_EPS) * ng_ref[...]).astype(y_ref.dtype)


def _ssd_scan(main, dt, conv_w, conv_b, dt_bias, a_log, d_skip, norm_g):
    b, s, _ = main.shape
    G, N, P, L = SSM_GROUPS, SSM_STATE, SSM_HEAD_DIM, SSM_CHUNK
    di = norm_g.shape[-1]
    gw = di // G
    hpg = gw // P
    assert s % L == 0
    xoff = di // gw
    boff = 2 * di // N
    coff = boff + G
    seq = lambda wd, off: pl.BlockSpec((1, L, wd), lambda bi, g, c: (bi, c, off + g))
    par = lambda r, wd, off: pl.BlockSpec((r, wd), lambda bi, g, c: (0, off + g))
    return pl.pallas_call(
        functools.partial(_ssd_kernel, hpg=hpg),
        grid=(b, G, s // L),
        in_specs=[
            seq(gw, 0), seq(gw, xoff), seq(N, boff), seq(N, coff), seq(LANES, 0),
            par(SSM_CONV, gw, 0), par(SSM_CONV, N, di // N), par(SSM_CONV, N, di // N + G),
            par(1, gw, 0), par(1, N, di // N), par(1, N, di // N + G),
            par(1, LANES, 0), par(1, LANES, 0), par(1, gw, 0), par(1, gw, 0),
        ],
        out_specs=seq(gw, 0),
        out_shape=jax.ShapeDtypeStruct((b, s, di), BF16),
        scratch_shapes=[
            pltpu.VMEM((N, gw), F32),
            pltpu.VMEM((SUBLANES + L, gw), F32),
            pltpu.VMEM((SUBLANES + L, N), F32),
            pltpu.VMEM((SUBLANES + L, N), F32),
        ],
        compiler_params=_params("parallel", "parallel", "arbitrary"),
        name="ssd_scan",
    )(main, main, main, main, dt, conv_w, conv_w, conv_w, conv_b, conv_b, conv_b,
      dt_bias, a_log, d_skip, norm_g)


def _pad_heads(v, groups):
    hpg = v.shape[0] // groups
    return jnp.pad(v.reshape(groups, hpg), ((0, 0), (0, LANES - hpg))).reshape(1, groups * LANES)


def _ssd_layer(x, pending, sh, sc, gate, g_norm, w_in, conv_w, conv_b, dt_bias, a_log, d_skip, norm_g, w_out):
    G = SSM_GROUPS
    di = norm_g.shape[0]
    heads = dt_bias.shape[0]
    hpg = heads // G
    n_main = 2 * di + 2 * G * SSM_STATE
    d = w_in.shape[0]
    w_dt = w_in[:, n_main:].reshape(d, G, hpg)
    w_dt = jnp.pad(w_dt, ((0, 0), (0, 0), (0, LANES - hpg))).reshape(d, G * LANES)
    w = jnp.concatenate([w_in[:, :n_main], w_dt], axis=1).astype(BF16)
    chunks = [(0, c * COL_CHUNK, None, 1) for c in range(n_main // COL_CHUNK)] + [(1, 0, None, 1)]
    main, dt, *rest = _proj(x, sh, sc, g_norm, w, chunks, [("tok", n_main, BF16, 1), ("tok", G * LANES, F32, 1)],
                            None, pending=pending)
    if pending is not None:
        x = rest[0]
    y = _ssd_scan(main, dt, conv_w, conv_b.reshape(1, -1), _pad_heads(dt_bias, G), _pad_heads(a_log, G),
                  jnp.repeat(d_skip, SSM_HEAD_DIM).reshape(1, -1), norm_g.reshape(1, -1))
    return _outproj(y, w_out.astype(BF16), x, gate)


def _moe_pre_kernel(x_ref, sh_ref, sc_ref, g_ref, rwt_ref, rb_ref, h_ref, cls_ref, w_ref, cnt_ref):
    first = (pl.program_id(0) == 0) & (pl.program_id(1) == 0)

    @pl.when(first)
    def _():
        cnt_ref[...] = jnp.zeros(cnt_ref.shape, F32)

    tm, d = x_ref.shape[1], x_ref.shape[2]
    pieces = d // LANES
    h = _modulated_norm(x_ref[0], g_ref[...], sc_ref[0], sh_ref[0])
    half = pieces // 2
    for s in range(half):
        h_ref[pl.ds(s, tm, stride=half), :] = _pack_bf16_pair(h[:, s * LANES:(s + 1) * LANES],
                                                              h[:, (half + s) * LANES:(half + s + 1) * LANES])
    logits = lax.dot_general(rwt_ref[...], h, _NT, preferred_element_type=F32, precision=HIGHEST)
    scores = jax.nn.sigmoid(logits)
    biased = scores + rb_ref[...]
    row = lambda a, e: a[e:e + 1, :]
    epg = EXPERTS_PER_GROUP

    best, g_sel = None, None
    for g in range(N_EXPERT_GROUPS):
        v = [row(biased, g * epg + j) for j in range(epg)]
        gs = None
        for i in range(epg):
            for j in range(i + 1, epg):
                pair = v[i] + v[j]
                gs = pair if gs is None else jnp.maximum(gs, pair)
        if best is None:
            best, g_sel = gs, jnp.zeros(gs.shape, I32)
        else:
            better = gs > best
            best = jnp.where(better, gs, best)
            g_sel = jnp.where(better, g, g_sel)

    def pick(a, j):
        out = row(a, j)
        for g in range(1, N_EXPERT_GROUPS):
            out = jnp.where(g_sel == g, row(a, g * epg + j), out)
        return out

    vb = [pick(biased, j) for j in range(epg)]
    vs = [pick(scores, j) for j in range(epg)]

    def argmax_first(vals, exclude):
        bv, bi, bs = None, None, None
        for j in range(epg):
            cand = vals[j] if exclude is None else jnp.where(exclude == j, -jnp.inf, vals[j])
            if bv is None:
                bv, bi, bs = cand, jnp.zeros(cand.shape, I32), vs[0]
            else:
                better = cand > bv
                bv = jnp.where(better, cand, bv)
                bi = jnp.where(better, j, bi)
                bs = jnp.where(better, vs[j], bs)
        return bi, bs

    i1, s1 = argmax_first(vb, None)
    i2, s2 = argmax_first(vb, i1)
    tot = s1 + s2
    first_lo = i1 < i2
    lo = jnp.where(first_lo, i1, i2)
    hi = jnp.where(first_lo, i2, i1)
    pair_base = jnp.where(lo == 0, 0, jnp.where(lo == 1, 3, 5))
    cls = g_sel * PAIRS_PER_GROUP + pair_base + (hi - lo - 1)
    cls_ref[...] = cls
    w_ref[...] = jnp.concatenate([jnp.where(first_lo, s1, s2) / tot, jnp.where(first_lo, s2, s1) / tot], axis=0)
    cid = lax.broadcasted_iota(I32, (cnt_ref.shape[0], cls.shape[1]), 0)
    cnt_ref[...] += jnp.sum((cid == cls).astype(F32), axis=1, keepdims=True)


def _moe_pre(x, sh, sc, g_norm, router_w, router_bias, tm=512):
    b, s, d = x.shape
    tm = min(tm, s)
    nt = s // tm
    t_all = b * s
    e = router_w.shape[1]
    pieces = d // LANES
    return pl.pallas_call(
        _moe_pre_kernel,
        grid=(b, nt),
        in_specs=[
            pl.BlockSpec((1, tm, d), lambda bi, i: (bi, i, 0)),
            pl.BlockSpec((1, 1, d), lambda bi, i: (bi, 0, 0)),
            pl.BlockSpec((1, 1, d), lambda bi, i: (bi, 0, 0)),
            pl.BlockSpec((1, d), lambda bi, i: (0, 0)),
            pl.BlockSpec((e, d), lambda bi, i: (0, 0)),
            pl.BlockSpec((e, 1), lambda bi, i: (0, 0)),
        ],
        out_specs=[
            pl.BlockSpec((tm * pieces // 2, LANES), lambda bi, i: (bi * nt + i, 0)),
            pl.BlockSpec((1, tm), lambda bi, i: (0, bi * nt + i)),
            pl.BlockSpec((2, tm), lambda bi, i: (0, bi * nt + i)),
            pl.BlockSpec((N_CLASSES, LANES), lambda bi, i: (0, 0)),
        ],
        out_shape=[
            jax.ShapeDtypeStruct((t_all * pieces // 2, LANES), U32),
            jax.ShapeDtypeStruct((1, t_all), I32),
            jax.ShapeDtypeStruct((2, t_all), F32),
            jax.ShapeDtypeStruct((N_CLASSES, LANES), F32),
        ],
        compiler_params=_params("arbitrary", "arbitrary"),
        name="moe_pre",
    )(x, sh, sc, g_norm, router_w.T, router_bias.reshape(e, 1))


def _moe_rank_kernel(cls_ref, cnt_ref, dest_ref, meta_ref, pstart_sc, run_sc, *, sub, n_sub, nbp):
    nc = cnt_ref.shape[0]

    @pl.when(pl.program_id(0) == 0)
    def _():
        cnt = cnt_ref[...]
        padded = jnp.ceil(cnt / MOE_BLOCK) * MOE_BLOCK
        blk = lax.broadcasted_iota(I32, (1, nbp), 1).astype(F32) * MOE_BLOCK
        running = jnp.zeros((1, LANES), F32)
        block_c = jnp.zeros((1, nbp), F32)
        for c in range(nc):
            pstart_sc[c:c + 1, :] = running
            running = running + padded[c:c + 1, :]
            block_c = block_c + (running[:, 0:1] <= blk).astype(F32)
        run_sc[...] = jnp.zeros(run_sc.shape, F32)
        block_c = jnp.minimum(block_c, nc - 1.0)
        group = jnp.floor(block_c / PAIRS_PER_GROUP)
        pair = block_c - group * PAIRS_PER_GROUP
        lo = (pair >= 3.0).astype(F32) + (pair >= 5.0).astype(F32)
        hi = jnp.where(pair == 0.0, 1.0, jnp.where((pair == 1.0) | (pair == 3.0), 2.0, 3.0))
        n_used = jnp.broadcast_to(running[:, 0:1] / MOE_BLOCK, (1, nbp))
        meta_ref[...] = jnp.concatenate(
            [group * EXPERTS_PER_GROUP + lo, group * EXPERTS_PER_GROUP + hi, n_used], axis=0).astype(I32)

    upper = (lax.broadcasted_iota(I32, (sub, sub), 0) < lax.broadcasted_iota(I32, (sub, sub), 1)).astype(BF16)
    cid = lax.broadcasted_iota(I32, (nc, sub), 0)
    for j in range(n_sub):
        oh = (cid == cls_ref[:, j * sub:(j + 1) * sub]).astype(F32)
        rank = jnp.dot(oh.astype(BF16), upper, preferred_element_type=F32)
        base = pstart_sc[:, 0:1] + run_sc[:, 0:1]
        dest = jnp.sum(oh * (rank + base), axis=0, keepdims=True)
        dest_ref[:, j * sub:(j + 1) * sub] = dest.astype(I32)
        run_sc[...] += jnp.sum(oh, axis=1, keepdims=True)


def _moe_rank(cls, cnt, n_blocks):
    t_all = cls.shape[1]
    sub = 256
    tr = min(2048, t_all)
    nbp = -(-n_blocks // LANES) * LANES
    nc = cnt.shape[0]
    return pl.pallas_call(
        functools.partial(_moe_rank_kernel, sub=sub, n_sub=tr // sub, nbp=nbp),
        grid=(t_all // tr,),
        in_specs=[
            pl.BlockSpec((1, tr), lambda i: (0, i)),
            pl.BlockSpec((nc, LANES), lambda i: (0, 0)),
        ],
        out_specs=[
            pl.BlockSpec((1, tr), lambda i: (0, i)),
            pl.BlockSpec((3, nbp), lambda i: (0, 0)),
        ],
        out_shape=[
            jax.ShapeDtypeStruct((1, t_all), I32),
            jax.ShapeDtypeStruct((3, nbp), I32),
        ],
        scratch_shapes=[pltpu.VMEM((nc, LANES), F32), pltpu.VMEM((nc, LANES), F32)],
        compiler_params=_params("arbitrary"),
        name="moe_rank",
    )(cls, cnt)


def _tile_copy(src_ref, s, dst_ref, d, sem, rows):
    return pltpu.make_async_copy(src_ref.at[pl.ds(pl.multiple_of(s * rows, rows), rows)],
                                 dst_ref.at[pl.ds(pl.multiple_of(d * rows, rows), rows)], sem)


def _moe_dispatch_kernel(dest_hbm, h_ref, xs_in, xs_out, idx_smem, idx_sem, sem, *, tm, pieces):
    del xs_in
    i = pl.program_id(0)
    cp = pltpu.make_async_copy(dest_hbm.at[i], idx_smem, idx_sem)
    cp.start()
    cp.wait()

    def issue(pair, carry):
        for k in range(2):
            t = 2 * pair + k
            _tile_copy(h_ref, t, xs_out, idx_smem[0, t], sem, pieces).start(priority=k)
        return carry

    lax.fori_loop(0, tm // 2, issue, 0, unroll=4)

    def drain(t, carry):
        _tile_copy(h_ref, 0, xs_out, 0, sem, pieces).wait()
        return carry

    lax.fori_loop(0, tm, drain, 0, unroll=8)


def _moe_dispatch(dest_tiles, h, rows, tm, pieces):
    t_all = h.shape[0] // pieces
    zeros = jnp.zeros((rows * pieces, LANES), U32)
    return pl.pallas_call(
        functools.partial(_moe_dispatch_kernel, tm=tm, pieces=pieces),
        grid=(t_all // tm,),
        in_specs=[
            pl.BlockSpec(memory_space=pl.ANY),
            pl.BlockSpec((tm * pieces, LANES), lambda i: (i, 0)),
            pl.BlockSpec(memory_space=pl.ANY),
        ],
        out_specs=pl.BlockSpec(memory_space=pl.ANY),
        out_shape=jax.ShapeDtypeStruct((rows * pieces, LANES), U32),
        scratch_shapes=[pltpu.SMEM((1, tm), I32), pltpu.SemaphoreType.DMA(()), pltpu.SemaphoreType.DMA(())],
        input_output_aliases={2: 0},
        compiler_params=_params("arbitrary"),
        name="moe_dispatch",
    )(dest_tiles, h, zeros)


def _moe_expert_kernel(meta_ref, x_ref, wga_ref, wua_ref, wda_ref, wgb_ref, wub_ref, wdb_ref, y_ref, xb_sc,
                       *, pieces):
    i = pl.program_id(0)
    rows = xb_sc.shape[0]

    @pl.when(i < meta_ref[2, 0])
    def _():
        half = pieces // 2
        for s in range(half):
            first, second = _unpack_bf16_pair(x_ref[pl.ds(s, rows, stride=half), :])
            xb_sc[:, s * LANES:(s + 1) * LANES] = first.astype(BF16)
            xb_sc[:, (half + s) * LANES:(half + s + 1) * LANES] = second.astype(BF16)
        x = xb_sc[...]
        halves = []
        for wg_ref, wu_ref, wd_ref in ((wga_ref, wua_ref, wda_ref), (wgb_ref, wub_ref, wdb_ref)):
            gate = jnp.dot(x, wg_ref[0], preferred_element_type=F32)
            up = jnp.dot(x, wu_ref[0], preferred_element_type=F32)
            hidden = (_silu(gate) * up).astype(BF16)
            y = jnp.dot(hidden, wd_ref[0], preferred_element_type=F32)
            halves.append(y)
        word = _pack_bf16_pair(halves[0], halves[1])
        for s in range(pieces):
            y_ref[pl.ds(s, rows, stride=pieces), :] = word[:, s * LANES:(s + 1) * LANES]

    @pl.when(i >= meta_ref[2, 0])
    def _():
        y_ref[...] = jnp.zeros(y_ref.shape, U32)


def _moe_experts(meta, xs, w_gate, w_up, w_down, n_blocks, pieces):
    d, f = w_gate.shape[1], w_gate.shape[2]
    wspec = lambda k, a, c: pl.BlockSpec((1, a, c), lambda i, meta: (meta[k, i], 0, 0))
    return pl.pallas_call(
        functools.partial(_moe_expert_kernel, pieces=pieces),
        grid_spec=pltpu.PrefetchScalarGridSpec(
            num_scalar_prefetch=1,
            grid=(n_blocks,),
            in_specs=[pl.BlockSpec((MOE_BLOCK * pieces // 2, LANES), lambda i, meta: (i, 0)),
                      wspec(0, d, f), wspec(0, d, f), wspec(0, f, d),
                      wspec(1, d, f), wspec(1, d, f), wspec(1, f, d)],
            out_specs=pl.BlockSpec((MOE_BLOCK * pieces, LANES), lambda i, meta: (i, 0)),
            scratch_shapes=[pltpu.VMEM((MOE_BLOCK, d), BF16)],
        ),
        out_shape=jax.ShapeDtypeStruct((2 * xs.shape[0], LANES), U32),
        compiler_params=_params("arbitrary"),
        name="moe_experts",
    )(meta, xs, w_gate, w_up, w_down, w_gate, w_up, w_down)


def _combine_tile(dest_hbm, y_hbm, w_ref, x_ref, g_ref, o_ref, y_sc, idx_smem, idx_sem, sem,
                  *, tile, n_tiles, tm, pieces, n_parts=1):
    slot = tile % 2
    other = 1 - slot

    def idx_copy(tl, sl):
        return pltpu.make_async_copy(dest_hbm.at[tl], idx_smem.at[pl.ds(sl, 1)], idx_sem.at[sl])

    def gather(sl, lo, hi):
        def issue(pair, carry):
            for k in range(2):
                t = lo + 2 * pair + k
                _tile_copy(y_hbm, idx_smem[sl, t], y_sc.at[sl], t, sem.at[sl], pieces).start(priority=k)
            return carry

        lax.fori_loop(0, (hi - lo) // 2, issue, 0, unroll=4)

    @pl.when(tile == 0)
    def _():
        idx_copy(0, 0).start()
        idx_copy(0, 0).wait()
        gather(0, 0, tm)
        if n_tiles > 1:
            idx_copy(1, 1).start()

    has_next = tile + 1 < n_tiles

    @pl.when(has_next)
    def _():
        idx_copy(tile + 1, other).wait()

    def drain(sl):
        def wait_one(t, carry):
            _tile_copy(y_hbm, 0, y_sc.at[sl], 0, sem.at[sl], pieces).wait()
            return carry

        lax.fori_loop(0, tm, wait_one, 0, unroll=8)

    def issue_part(j):
        if n_parts == 1:
            @pl.when(has_next)
            def _():
                gather(other, 0, tm)
        elif n_tiles > 1:
            for t in range(j * tm // n_parts, (j + 1) * tm // n_parts):
                _tile_copy(y_hbm, idx_smem[other, t], y_sc.at[other], t, sem.at[other], pieces).start(priority=t % 2)
        if j == n_parts - 1:
            @pl.when(tile + 2 < n_tiles)
            def _():
                idx_copy(tile + 2, slot).start()

            if n_parts > 1 and n_tiles > 1:
                @pl.when(tile + 1 == n_tiles)
                def _():
                    drain(other)

    if n_parts == 1:
        issue_part(0)
    drain(slot)
    w = w_ref[...]
    w_lo, w_hi = w[:, 0:1], w[:, 1:2]
    for s in range(pieces):
        cols = slice(s * LANES, (s + 1) * LANES)
        y_lo, y_hi = _unpack_bf16_pair(y_sc[slot, pl.ds(s, tm, stride=pieces), :])
        moe = w_lo * y_lo + w_hi * y_hi
        o_ref[0, :, cols] = x_ref[0, :, cols] + g_ref[0, :, cols] * moe
    return issue_part


def _moe_combine_kernel(dest_hbm, y_hbm, w_ref, x_ref, g_ref, o_ref, y_sc, idx_smem, idx_sem, sem,
                        *, tm, nt, n_tiles, pieces):
    _combine_tile(dest_hbm, y_hbm, w_ref, x_ref, g_ref, o_ref, y_sc, idx_smem, idx_sem, sem,
                  tile=pl.program_id(0) * nt + pl.program_id(1), n_tiles=n_tiles, tm=tm, pieces=pieces)


def _combine_scratch(tm, pieces):
    return [pltpu.VMEM((2, tm * pieces, LANES), U32), pltpu.SMEM((2, tm), I32),
            pltpu.SemaphoreType.DMA((2,)), pltpu.SemaphoreType.DMA((2,))]


def _moe_combine(dest_tiles, y, w_col, x, gate, tm, pieces):
    b, s, d = x.shape
    nt = s // tm
    return pl.pallas_call(
        functools.partial(_moe_combine_kernel, tm=tm, nt=nt, n_tiles=b * nt, pieces=pieces),
        grid=(b, nt),
        in_specs=[
            pl.BlockSpec(memory_space=pl.ANY),
            pl.BlockSpec(memory_space=pl.ANY),
            pl.BlockSpec((tm, 2), lambda bi, i: (bi * nt + i, 0)),
            pl.BlockSpec((1, tm, d), lambda bi, i: (bi, i, 0)),
            pl.BlockSpec((1, 1, d), lambda bi, i: (bi, 0, 0)),
        ],
        out_specs=pl.BlockSpec((1, tm, d), lambda bi, i: (bi, i, 0)),
        out_shape=jax.ShapeDtypeStruct((b, s, d), F32),
        scratch_shapes=_combine_scratch(tm, pieces),
        compiler_params=_params("arbitrary", "arbitrary"),
        name="moe_combine",
    )(dest_tiles, y, w_col, x, gate)


def _moe_layer(x, sh, sc, gate, g_norm, router_w, router_bias, w_gate, w_up, w_down):
    b, s, d = x.shape
    t_all = b * s
    pieces = d // LANES
    tm = min(512, s)
    n_blocks = -(-t_all // MOE_BLOCK) + N_CLASSES
    rows = n_blocks * MOE_BLOCK
    h, cls, w, cnt = _moe_pre(x, sh, sc, g_norm, router_w, router_bias)
    dest, meta = _moe_rank(cls, cnt, n_blocks)
    dest_tiles = dest.reshape(t_all // tm, 1, tm)
    xs = _moe_dispatch(dest_tiles, h, rows, tm, pieces // 2)
    y = _moe_experts(meta, xs, w_gate.astype(BF16), w_up.astype(BF16), w_down.astype(BF16), n_blocks, pieces)
    return dest_tiles, y, w.T, gate


def _final_norm_kernel(x_ref, g_ref, o_ref):
    x = x_ref[0]
    ms = jnp.mean(x * x, axis=-1, keepdims=True)
    o_ref[0] = x * lax.rsqrt(ms + NORM_EPS) * g_ref[...]


def _final_norm(x, g, tm=1024):
    b, s, d = x.shape
    tm = min(tm, s)
    return pl.pallas_call(
        _final_norm_kernel,
        grid=(b, s // tm),
        in_specs=[
            pl.BlockSpec((1, tm, d), lambda bi, i: (bi, i, 0)),
            pl.BlockSpec((1, d), lambda bi, i: (0, 0)),
        ],
        out_specs=pl.BlockSpec((1, tm, d), lambda bi, i: (bi, i, 0)),
        out_shape=jax.ShapeDtypeStruct((b, s, d), F32),
        compiler_params=_params("parallel", "arbitrary"),
        name="final_norm",
    )(x, g.reshape(1, d))


def kernel(x, c, ada_w, ada_b, norm1_g, norm2_g, router_w, router_bias, moe_w_gate, moe_w_up, moe_w_down, dil_w_in, dil_w_out, diff_w_in, diff_lam_q1, diff_lam_k1, diff_lam_q2, diff_lam_k2, diff_head_norm_g, diff_w_out, ssm_w_in, ssm_conv_w, ssm_conv_b, ssm_dt_bias, ssm_A_log, ssm_D, ssm_norm_g, ssm_w_out, final_norm_g):
    b, s, d = x.shape
    depth = ada_w.shape[0]
    mod = _ada_mod(c, ada_w, ada_b).reshape(depth, b, 6, 1, d)
    tables = _rope_tables(s)
    pending = None
    for i in range(depth):
        sh1, sc1, g1, sh2, sc2, g2 = (mod[i, :, j] for j in range(6))
        n1 = norm1_g[i].reshape(1, d)
        kind, j = i % 3, i // 3
        if kind == 0:
            x = _dilated_layer(x, pending, sh1, sc1, g1, n1, dil_w_in[j], dil_w_out[j], tables)
        elif kind == 1:
            x = _diff_layer(x, pending, sh1, sc1, g1, n1, diff_w_in[j], diff_lam_q1[j], diff_lam_k1[j],
                            diff_lam_q2[j], diff_lam_k2[j], diff_head_norm_g[j], diff_w_out[j], tables, i)
        else:
            x = _ssd_layer(x, pending, sh1, sc1, g1, n1, ssm_w_in[j], ssm_conv_w[j], ssm_conv_b[j],
                           ssm_dt_bias[j], ssm_A_log[j], ssm_D[j], ssm_norm_g[j], ssm_w_out[j])
        pending = _moe_layer(x, sh2, sc2, g2, norm2_g[i].reshape(1, d), router_w, router_bias,
                             moe_w_gate[i], moe_w_up[i], moe_w_down[i])
    dest_tiles, y, w_col, gate = pending
    x = _moe_combine(dest_tiles, y, w_col, x, gate, dest_tiles.shape[-1], d // LANES)
    return _final_norm(x, final_norm_g)
```
